```python
import math
import jax
import jax.numpy as jnp
from jax import lax
import numpy as np

D_MODEL = 1024
BATCH = 4
SEQ = 4096
DEPTH = 2

N_BRANCH = 4
W_BRANCH = 256
NORM_EPS = 1e-6

S5_GROUP = 16
S5_GROUPS = W_BRANCH // S5_GROUP
S5_STATE = 64

ATT_HEADS = 4
ATT_DIM = W_BRANCH // ATT_HEADS
IDX_HEADS = 4
IDX_DIM = 32
TOPK_MAX = 256
Q_BLOCK = 128
ROPE_THETA = 10000.0

RG_BLOCKS = 4
RG_BDIM = W_BRANCH // RG_BLOCKS
RG_C = 8.0
CONV_WIDTH = 4

RW_HEADS = 4
RW_DIM = W_BRANCH // RW_HEADS
RW_DECAY_LORA = 32
RW_A_LORA = 32
RW_GATE_LORA = 64
RW_GN_EPS = 64e-5
RW_WIDTHS = (W_BRANCH, W_BRANCH, W_BRANCH, RW_DECAY_LORA, RW_A_LORA, RW_GATE_LORA)
RW_COLS = 3 * W_BRANCH + RW_DECAY_LORA + RW_A_LORA + RW_GATE_LORA

IN_WIDTHS = (W_BRANCH,
             W_BRANCH, W_BRANCH, W_BRANCH,
             IDX_HEADS * IDX_DIM, IDX_DIM, IDX_HEADS,
             W_BRANCH, W_BRANCH,
             RW_COLS)
D_IN = 6 * W_BRANCH + IDX_HEADS * IDX_DIM + IDX_DIM + IDX_HEADS + RW_COLS

N_EXPERTS = 16
N_GROUPS = 4
EXP_PER_GROUP = N_EXPERTS // N_GROUPS
TOP_K = 2
D_EXPERT = 512

kernel_name = 'hybrid_gated_s5_dsa_rglru_rwkv7_moe'


def _split(a, widths):
    offs = []
    o = 0
    for w in widths[:-1]:
        o += w
        offs.append(o)
    return jnp.split(a, offs, axis=-1)


def rmsnorm(x, g):
    xf = x.astype(jnp.float32)
    y = xf * lax.rsqrt(jnp.mean(xf * xf, axis=-1, keepdims=True) + NORM_EPS)
    return (y * g.astype(jnp.float32)).astype(x.dtype)


def rope(x, pos):
    half = x.shape[-1] // 2
    inv_freq = ROPE_THETA ** (-jnp.arange(half, dtype=jnp.float32) / half)
    ang = pos.astype(jnp.float32)[:, :, None, None] * inv_freq
    cos, sin = jnp.cos(ang), jnp.sin(ang)
    xf = x.astype(jnp.float32)
    x1, x2 = xf[..., :half], xf[..., half:]
    return jnp.concatenate([x1 * cos - x2 * sin, x2 * cos + x1 * sin], axis=-1).astype(x.dtype)


def linear_scan(a, b):
    def combine(left, right):
        return right[0] * left[0], right[0] * left[1] + right[1]
    return lax.associative_scan(combine, (a, b), axis=1)[1]


def s5_mixer(u, lam_re, lam_im, log_step, b_re, b_im, c_re, c_im, d_skip, glu_w, glu_b):
    bsz, seq, _ = u.shape
    f32 = jnp.float32
    lam = lax.complex(lam_re.astype(f32), lam_im.astype(f32))
    step = jnp.exp(log_step.astype(f32))[:, None]
    lam_bar = jnp.exp(lam * step)
    b_bar = ((lam_bar - 1.0) / lam)[:, :, None] * lax.complex(b_re.astype(f32), b_im.astype(f32))
    ug = u.astype(f32).reshape(bsz, seq, S5_GROUPS, S5_GROUP)
    bu = jnp.einsum('btgc,gpc->btgp', ug.astype(jnp.complex64), b_bar)
    states = linear_scan(jnp.broadcast_to(lam_bar, bu.shape), bu)
    c_mat = lax.complex(c_re.astype(f32), c_im.astype(f32))
    y = jnp.real(jnp.einsum('btgp,gcp->btgc', states, c_mat)) + d_skip.astype(f32).reshape(S5_GROUPS, S5_GROUP) * ug
    y = jax.nn.gelu(y.reshape(bsz, seq, W_BRANCH)).astype(u.dtype)
    return y * jax.nn.sigmoid(y @ glu_w + glu_b)


def dsa_mixer(q, k, v, iq, ik, iw, pos):
    bsz, seq = q.shape[:2]
    f32 = jnp.float32
    topk = min(TOPK_MAX, seq // 4)
    n_blocks = seq // Q_BLOCK
    q = rope(q, pos) * (ATT_DIM ** -0.5)
    k = rope(k, pos)
    iq = rope(iq, pos)
    ik = rope(ik[:, :, None, :], pos)[:, :, 0].astype(f32)
    key_pos = jnp.arange(seq)
    gather = jax.vmap(lambda arr, ix: arr[ix])

    def blocks(a):
        return a.reshape(bsz, n_blocks, Q_BLOCK, *a.shape[2:]).swapaxes(0, 1)

    def one_block(args):
        qb, iqb, iwb, t0 = args
        q_pos = t0 + jnp.arange(Q_BLOCK)
        causal = key_pos[None, :] <= q_pos[:, None]
        dots = jnp.einsum('bqhd,bsd->bqhs', iqb.astype(f32), ik)
        score = jnp.einsum('bqh,bqhs->bqs', iwb.astype(f32), jax.nn.relu(dots))
        score = jnp.where(causal[None], score, -jnp.inf)
        _, idx = lax.top_k(score, topk)
        k_sel = gather(k, idx).astype(f32)
        v_sel = gather(v, idx).astype(f32)
        logits = jnp.einsum('bqhd,bqkhd->bhqk', qb.astype(f32), k_sel)
        valid = (idx <= q_pos[None, :, None])[:, None]
        p = jax.nn.softmax(jnp.where(valid, logits, -jnp.inf), axis=-1)
        return jnp.einsum('bhqk,bqkhd->bqhd', p, v_sel).astype(qb.dtype)

    out = lax.map(one_block, (blocks(q), blocks(iq), blocks(iw), jnp.arange(n_blocks) * Q_BLOCK))
    return out.swapaxes(0, 1).reshape(bsz, seq, ATT_HEADS * ATT_DIM)


def rglru_mixer(xr, gate, conv_w, conv_b, w_r, b_r, w_i, b_i, lam):
    bsz, seq, width = xr.shape
    xc = lax.conv_general_dilated(xr, conv_w[:, None, :], window_strides=(1,),
                                  padding=[(CONV_WIDTH - 1, 0)],
                                  dimension_numbers=('NWC', 'WIO', 'NWC'),
                                  feature_group_count=width) + conv_b
    xc = xc.astype(jnp.float32)
    xb = xc.reshape(bsz, seq, RG_BLOCKS, RG_BDIM)
    r = jax.nn.sigmoid(jnp.einsum('btnc,ncd->btnd', xb, w_r) + b_r).reshape(bsz, seq, width)
    i = jax.nn.sigmoid(jnp.einsum('btnc,ncd->btnd', xb, w_i) + b_i).reshape(bsz, seq, width)
    log_a = -RG_C * r * jax.nn.softplus(-lam.astype(jnp.float32))
    a = jnp.exp(log_a)
    mult = jnp.sqrt(-jnp.expm1(2.0 * log_a))
    h = linear_scan(a, mult * i * xc)
    return (h * jax.nn.gelu(gate.astype(jnp.float32))).astype(xr.dtype)


def rwkv7_mixer(feat, mu, w0, w_up, a0, a_up, g_up, k_k, k_a, r_k, gn_w, gn_b):
    bsz, seq, _ = feat.shape
    f = feat.astype(jnp.float32)
    prev = jnp.pad(f, ((0, 0), (1, 0), (0, 0)))[:, :-1]
    f = f + (prev - f) * mu
    r, k, v, wd, ad, gd = _split(f, RW_WIDTHS)
    w = -jax.nn.softplus(-(w0 + jnp.tanh(wd) @ w_up)) - 0.5
    decay = jnp.exp(-jnp.exp(w))
    a = jax.nn.sigmoid(a0 + ad @ a_up)
    g = jax.nn.sigmoid(gd) @ g_up

    def heads(t):
        return t.reshape(bsz, seq, RW_HEADS, RW_DIM)

    kk = heads(k * k_k)
    kk = kk / jnp.maximum(jnp.linalg.norm(kk, axis=-1, keepdims=True), 1e-12)
    k = k * (1.0 + (a - 1.0) * k_a)
    r_h, k_h, v_h, a_h, d_h = heads(r), heads(k), heads(v), heads(a), heads(decay)

    def step(state, inp):
        r_t, d_t, k_t, v_t, kk_t, a_t = inp
        s_kk = jnp.einsum('bhvk,bhk->bhv', state, kk_t)
        state = (state * d_t[:, :, None, :]
                 - s_kk[..., None] * (kk_t * a_t)[:, :, None, :]
                 + v_t[..., None] * k_t[:, :, None, :])
        return state, jnp.einsum('bhvk,bhk->bhv', state, r_t)

    xs = tuple(jnp.moveaxis(t, 1, 0) for t in (r_h, d_h, k_h, v_h, kk, a_h))
    _, y = lax.scan(step, jnp.zeros((bsz, RW_HEADS, RW_DIM, RW_DIM), jnp.float32), xs)
    y = jnp.moveaxis(y, 0, 1)
    mean = jnp.mean(y, axis=-1, keepdims=True)
    var = jnp.mean(jnp.square(y - mean), axis=-1, keepdims=True)
    y = ((y - mean) * lax.rsqrt(var + RW_GN_EPS)).reshape(bsz, seq, W_BRANCH) * gn_w + gn_b
    bonus = jnp.sum(r_h * k_h * r_k, axis=-1, keepdims=True) * v_h
    y = y + bonus.reshape(bsz, seq, W_BRANCH)
    return (y * g).astype(feat.dtype)


def moe(h, router_w, router_b, w1, w3, w2):
    bsz, seq, d = h.shape
    hf = h.reshape(bsz * seq, d)
    scores = jax.nn.sigmoid(hf.astype(jnp.float32) @ router_w.astype(jnp.float32))
    biased = scores + router_b.astype(jnp.float32)
    group_score = jnp.sum(lax.top_k(biased.reshape(-1, N_GROUPS, EXP_PER_GROUP), TOP_K)[0], axis=-1)
    best_group = jnp.argmax(group_score, axis=-1)
    in_group = (jnp.arange(N_EXPERTS) // EXP_PER_GROUP)[None, :] == best_group[:, None]
    _, idx = lax.top_k(jnp.where(in_group, biased, -jnp.inf), TOP_K)
    sel = jnp.take_along_axis(scores, idx, axis=-1)
    weights = sel / jnp.sum(sel, axis=-1, keepdims=True)
    combine = jnp.sum(jax.nn.one_hot(idx, N_EXPERTS, dtype=jnp.float32) * weights[..., None], axis=1)

    def add_expert(acc, p):
        e_w1, e_w3, e_w2, e_c = p
        y = (jax.nn.silu(hf @ e_w1) * (hf @ e_w3)) @ e_w2
        return acc + e_c[:, None] * y.astype(jnp.float32), None

    out, _ = lax.scan(add_expert, jnp.zeros((bsz * seq, d), jnp.float32), (w1, w3, w2, combine.T))
    return out.reshape(bsz, seq, d).astype(h.dtype)


def _normal(key, shape, scale):
    return scale * jax.random.normal(key, shape, jnp.float32)


def setup_inputs(seed: int = 0) -> dict:
    key = jax.random.key(seed)
    ks = iter(jax.random.split(key, 48))
    L, D, W = DEPTH, D_MODEL, W_BRANCH
    G, P, C = S5_GROUPS, S5_STATE, S5_GROUP
    inp = {}
    inp['x'] = _normal(next(ks), (BATCH, SEQ, D), 1.0)
    inp['c'] = _normal(next(ks), (BATCH, D), 1.0)
    offset = jax.random.randint(next(ks), (BATCH, 1), 0, 2048)
    inp['positions'] = (offset + jnp.arange(SEQ)[None, :]).astype(jnp.int32)
    inp['ada_w'] = _normal(next(ks), (L, D, 6 * D), 0.5 * D ** -0.5)
    inp['ada_b'] = _normal(next(ks), (L, 6 * D), 0.02)
    inp['mix_norm_g'] = 1.0 + _normal(next(ks), (L, D), 0.02)
    inp['w_in'] = _normal(next(ks), (L, D, D_IN), D ** -0.5)
    inp['gate_w'] = _normal(next(ks), (L, D, N_BRANCH * D), D ** -0.5)
    inp['gate_b'] = _normal(next(ks), (L, N_BRANCH * D), 0.02)
    inp['branch_w'] = _normal(next(ks), (L, N_BRANCH, W, D), W ** -0.5)
    inp['out_w'] = _normal(next(ks), (L, D, D), D ** -0.5)
    inp['s5_lam_re'] = -0.5 + _normal(next(ks), (L, G, P), 0.01)
    inp['s5_lam_im'] = math.pi * jnp.arange(P, dtype=jnp.float32) + _normal(next(ks), (L, G, P), 0.01)
    inp['s5_log_step'] = jax.random.uniform(next(ks), (L, G), jnp.float32, math.log(1e-3), math.log(1e-1))
    inp['s5_b_re'] = _normal(next(ks), (L, G, P, C), (2 * C) ** -0.5)
    inp['s5_b_im'] = _normal(next(ks), (L, G, P, C), (2 * C) ** -0.5)
    inp['s5_c_re'] = _normal(next(ks), (L, G, C, P), (2 * P) ** -0.5)
    inp['s5_c_im'] = _normal(next(ks), (L, G, C, P), (2 * P) ** -0.5)
    inp['s5_d'] = _normal(next(ks), (L, W), 1.0)
    inp['s5_glu_w'] = _normal(next(ks), (L, W, W), W ** -0.5)
    inp['s5_glu_b'] = _normal(next(ks), (L, W), 0.02)
    inp['rg_conv_w'] = _normal(next(ks), (L, CONV_WIDTH, W), CONV_WIDTH ** -0.5)
    inp['rg_conv_b'] = _normal(next(ks), (L, W), 0.02)
    inp['rg_wr'] = _normal(next(ks), (L, RG_BLOCKS, RG_BDIM, RG_BDIM), RG_BDIM ** -0.5)
    inp['rg_br'] = _normal(next(ks), (L, RG_BLOCKS, RG_BDIM), 0.02)
    inp['rg_wi'] = _normal(next(ks), (L, RG_BLOCKS, RG_BDIM, RG_BDIM), RG_BDIM ** -0.5)
    inp['rg_bi'] = _normal(next(ks), (L, RG_BLOCKS, RG_BDIM), 0.02)
    a_pow = jax.random.uniform(next(ks), (L, W), jnp.float32, 0.9, 0.999)
    a_base = a_pow ** (1.0 / RG_C)
    inp['rg_lam'] = jnp.log(a_base) - jnp.log1p(-a_base)
    inp['rw_mu'] = jax.random.uniform(next(ks), (L, RW_COLS), jnp.float32, 0.0, 1.0)
    inp['rw_w0'] = jax.random.uniform(next(ks), (L, W), jnp.float32, -6.0, -1.0)
    inp['rw_w_up'] = _normal(next(ks), (L, RW_DECAY_LORA, W), 0.1 * RW_DECAY_LORA ** -0.5)
    inp['rw_a0'] = _normal(next(ks), (L, W), 0.1)
    inp['rw_a_up'] = _normal(next(ks), (L, RW_A_LORA, W), 0.5 * RW_A_LORA ** -0.5)
    inp['rw_g_up'] = _normal(next(ks), (L, RW_GATE_LORA, W), RW_GATE_LORA ** -0.5)
    inp['rw_k_k'] = 0.85 + _normal(next(ks), (L, W), 0.02)
    inp['rw_k_a'] = 1.0 + _normal(next(ks), (L, W), 0.02)
    inp['rw_r_k'] = _normal(next(ks), (L, RW_HEADS, RW_DIM), 0.1)
    inp['rw_gn_w'] = 1.0 + _normal(next(ks), (L, W), 0.02)
    inp['rw_gn_b'] = _normal(next(ks), (L, W), 0.02)
    inp['ffn_norm_g'] = 1.0 + _normal(next(ks), (L, D), 0.02)
    inp['router_w'] = _normal(next(ks), (D, N_EXPERTS), D ** -0.5)
    inp['router_b'] = _normal(next(ks), (N_EXPERTS,), 0.01)
    inp['exp_w1'] = _normal(next(ks), (L, N_EXPERTS, D, D_EXPERT), D ** -0.5)
    inp['exp_w3'] = _normal(next(ks), (L, N_EXPERTS, D, D_EXPERT), D ** -0.5)
    inp['exp_w2'] = _normal(next(ks), (L, N_EXPERTS, D_EXPERT, D), D_EXPERT ** -0.5)
    inp['final_norm_g'] = 1.0 + _normal(next(ks), (D,), 0.02)
    return inp


def reference(x, c, positions, ada_w, ada_b, mix_norm_g, w_in, gate_w, gate_b, branch_w, out_w,
              s5_lam_re, s5_lam_im, s5_log_step, s5_b_re, s5_b_im, s5_c_re, s5_c_im, s5_d, s5_glu_w, s5_glu_b,
              rg_conv_w, rg_conv_b, rg_wr, rg_br, rg_wi, rg_bi, rg_lam,
              rw_mu, rw_w0, rw_w_up, rw_a0, rw_a_up, rw_g_up, rw_k_k, rw_k_a, rw_r_k, rw_gn_w, rw_gn_b,
              ffn_norm_g, router_w, router_b, exp_w1, exp_w3, exp_w2, final_norm_g):
    bsz, seq, d = x.shape
    c_act = jax.nn.silu(c)
    for l in range(DEPTH):
        mod = (c_act @ ada_w[l] + ada_b[l]).reshape(bsz, 6, 1, d)
        shift1, scale1, gate1, shift2, scale2, gate2 = (mod[:, i] for i in range(6))
        h = rmsnorm(x, mix_norm_g[l]) * (1.0 + scale1) + shift1
        u, q, k, v, iq, ik, iw, rg_x, rg_g, rw_f = _split(h @ w_in[l], IN_WIDTHS)
        o_s5 = s5_mixer(u, s5_lam_re[l], s5_lam_im[l], s5_log_step[l], s5_b_re[l], s5_b_im[l],
                        s5_c_re[l], s5_c_im[l], s5_d[l], s5_glu_w[l], s5_glu_b[l])
        o_dsa = dsa_mixer(q.reshape(bsz, seq, ATT_HEADS, ATT_DIM),
                          k.reshape(bsz, seq, ATT_HEADS, ATT_DIM),
                          v.reshape(bsz, seq, ATT_HEADS, ATT_DIM),
                          iq.reshape(bsz, seq, IDX_HEADS, IDX_DIM), ik, iw, positions)
        o_rg = rglru_mixer(rg_x, rg_g, rg_conv_w[l], rg_conv_b[l], rg_wr[l], rg_br[l],
                           rg_wi[l], rg_bi[l], rg_lam[l])
        o_rw = rwkv7_mixer(rw_f, rw_mu[l], rw_w0[l], rw_w_up[l], rw_a0[l], rw_a_up[l], rw_g_up[l],
                           rw_k_k[l], rw_k_a[l], rw_r_k[l], rw_gn_w[l], rw_gn_b[l])
        branches = jnp.stack([o_s5, o_dsa, o_rg, o_rw], axis=2)
        up = jnp.einsum('btnw,nwd->btnd', branches, branch_w[l])
        gates = jax.nn.sigmoid(h @ gate_w[l] + gate_b[l]).reshape(bsz, seq, N_BRANCH, d)
        mixed = jnp.einsum('btnd,btnd->btd', gates, up) @ out_w[l]
        x = x + gate1 * mixed
        h2 = rmsnorm(x, ffn_norm_g[l]) * (1.0 + scale2) + shift2
        x = x + gate2 * moe(h2, router_w, router_b, exp_w1[l], exp_w3[l], exp_w2[l])
    return rmsnorm(x, final_norm_g)
```

```python
import functools
import math

import jax
import jax.numpy as jnp
from jax import lax
from jax.experimental import pallas as pl
from jax.experimental.pallas import tpu as pltpu

F32 = jnp.float32
BF16 = jnp.bfloat16
I32 = jnp.int32

W_BRANCH = 256
N_BRANCH = 4
NORM_EPS = 1e-6
S5_GROUP = 16
S5_GROUPS = W_BRANCH // S5_GROUP
S5_STATE = 64
ATT_HEADS = 4
ATT_DIM = W_BRANCH // ATT_HEADS
IDX_HEADS = 4
IDX_DIM = 32
TOPK_MAX = 256
ROPE_THETA = 10000.0
RG_BLOCKS = 4
RG_C = 8.0
CONV_WIDTH = 4
RW_HEADS = 4
RW_DIM = W_BRANCH // RW_HEADS
RW_DECAY_LORA = 32
RW_A_LORA = 32
RW_GATE_LORA = 64
RW_GN_EPS = 64e-5
RW_COLS = 3 * W_BRANCH + RW_DECAY_LORA + RW_A_LORA + RW_GATE_LORA
N_EXPERTS = 16
N_GROUPS = 4
EXP_PER_GROUP = N_EXPERTS // N_GROUPS
D_EXPERT = 512

V7X_SUBLANES = 8
V7X_LANES = 128
V7X_VMEM_LIMIT_BYTES = 56 * 1024 * 1024

INT_MIN = -(2 ** 31)
NEG_BIG = -1e30


def _cparams(*sem):
    return pltpu.CompilerParams(dimension_semantics=sem, vmem_limit_bytes=V7X_VMEM_LIMIT_BYTES)


def _split_bf16(x):
    hi = x.astype(BF16)
    lo = (x - hi.astype(F32)).astype(BF16)
    return hi, lo


def _dot(a, b):
    return jnp.dot(a, b, preferred_element_type=F32)


def _dot_nt(a, b):
    return lax.dot_general(a, b, (((1,), (1,)), ((), ())), preferred_element_type=F32)


def _dot_tn(a, b):
    return lax.dot_general(a, b, (((0,), (0,)), ((), ())), preferred_element_type=F32)


def _dot3(a, b):
    ah, al = _split_bf16(a)
    bh, bl = _split_bf16(b)
    return _dot(ah, bh) + _dot(ah, bl) + _dot(al, bh)


def _sigmoid(x):
    return 1.0 / (1.0 + jnp.exp(-x))


def _softplus(x):
    return jnp.maximum(x, 0.0) + jnp.log(1.0 + jnp.exp(-jnp.abs(x)))


def _gelu(x):
    c = math.sqrt(2.0 / math.pi)
    return 0.5 * x * (1.0 + jnp.tanh(c * (x + 0.044715 * (x * x * x))))


def _rot_half_cols(w, n_heads, dim):
    lead = w.shape[0]
    w4 = w.reshape(lead, n_heads, 2, dim // 2)
    return jnp.concatenate([-w4[:, :, 1:2], w4[:, :, 0:1]], axis=2).reshape(lead, n_heads * dim)


def _ada_kernel(c_ref, w_ref, b_ref, o_ref):
    c = c_ref[...]
    ca = c * _sigmoid(c)
    o_ref[0] = _dot3(ca, w_ref[0]) + b_ref[0]


def _ada_mod(c, ada_w, ada_b):
    depth, d, d6 = ada_w.shape
    bsz = c.shape[0]
    rows = -(-bsz // V7X_SUBLANES) * V7X_SUBLANES
    cp = jnp.zeros((rows, d), F32).at[:bsz].set(c)
    out = pl.pallas_call(
        _ada_kernel,
        grid=(depth, d6 // d),
        in_specs=[pl.BlockSpec((rows, d), lambda l, j: (0, 0)),
                  pl.BlockSpec((1, d, d), lambda l, j: (l, 0, j)),
                  pl.BlockSpec((1, 1, d), lambda l, j: (l, 0, j))],
        out_specs=pl.BlockSpec((1, rows, d), lambda l, j: (l, 0, j)),
        out_shape=jax.ShapeDtypeStruct((depth, rows, d6), F32),
        compiler_params=_cparams("parallel", "parallel"),
        name="ada_mod",
    )(cp, ada_w, ada_b.reshape(depth, 1, d6))
    return out[:, :bsz].reshape(depth, bsz, 6, d)


def _rope_kernel(pos_ref, fa_ref, fi_ref, ca_ref, sa_ref, ci_ref, si_ref):
    pos = pos_ref[...]
    ang_a = pos * fa_ref[...]
    ang_i = pos * fi_ref[...]
    ca_ref[...] = jnp.cos(ang_a)
    sa_ref[...] = jnp.sin(ang_a)
    ci_ref[...] = jnp.cos(ang_i)
    si_ref[...] = jnp.sin(ang_i)


def _rope_tables(positions):
    n = positions.size
    pos = positions.reshape(n, 1).astype(F32)

    def freq_row(dim):
        half = dim // 2
        inv = ROPE_THETA ** (-jnp.arange(half, dtype=F32) / half)
        return jnp.tile(inv, V7X_LANES // half).reshape(1, V7X_LANES)

    tm = min(n, 2048)
    row = pl.BlockSpec((tm, V7X_LANES), lambda i: (i, 0))
    frq = pl.BlockSpec((1, V7X_LANES), lambda i: (0, 0))
    shp = jax.ShapeDtypeStruct((n, V7X_LANES), F32)
    return pl.pallas_call(
        _rope_kernel,
        grid=(n // tm,),
        in_specs=[pl.BlockSpec((tm, 1), lambda i: (i, 0)), frq, frq],
        out_specs=[row, row, row, row],
        out_shape=[shp, shp, shp, shp],
        compiler_params=_cparams("parallel"),
        name="rope_tables",
    )(pos, freq_row(ATT_DIM), freq_row(IDX_DIM))


def _modulated_norm(x, g, scale, shift):
    ms = jnp.mean(x * x, axis=-1, keepdims=True)
    return (x * lax.rsqrt(ms + NORM_EPS)) * g * (1.0 + scale) + shift


def _inproj_kernel(x_ref, mod_ref, g_ref, wm_ref, wih_ref, wil_ref,
                   ca_ref, sa_ref, ci_ref, si_ref,
                   u_ref, q_ref, k_ref, v_ref, iqh_ref, iql_ref, ikh_ref, ikl_ref, iw_ref,
                   rgx_ref, rgg_ref, rw_ref):
    w = W_BRANCH
    mod = mod_ref[0]
    h = _modulated_norm(x_ref[...], g_ref[...], mod[1:2], mod[0:1])
    hh, hl = _split_bf16(h)
    main = _dot(hh, wm_ref[...])
    idx = _dot(hh, wih_ref[...]) + _dot(hh, wil_ref[...]) + _dot(hl, wih_ref[...])
    ca = jnp.concatenate([ca_ref[...], ca_ref[...]], axis=1)
    sa = jnp.concatenate([sa_ref[...], sa_ref[...]], axis=1)
    u_ref[...] = main[:, 0:w]
    q_ref[...] = (main[:, w:2 * w] * ca + main[:, 2 * w:3 * w] * sa).astype(BF16)
    k_ref[...] = (main[:, 3 * w:4 * w] * ca + main[:, 4 * w:5 * w] * sa).astype(BF16)
    v_ref[...] = main[:, 5 * w:6 * w].astype(BF16)
    rgx_ref[...] = main[:, 6 * w:7 * w]
    rgg_ref[...] = main[:, 7 * w:8 * w]
    rw_ref[...] = main[:, 8 * w:8 * w + RW_COLS]
    ci = ci_ref[...]
    si = si_ref[...]
    iq = idx[:, 0:128] * ci + idx[:, 128:256] * si
    ik = idx[:, 256:288] * ci[:, 0:IDX_DIM] + idx[:, 288:320] * si[:, 0:IDX_DIM]
    iqh, iql = _split_bf16(iq)
    ikh, ikl = _split_bf16(ik)
    iqh_ref[...] = iqh
    iql_ref[...] = iql
    ikh_ref[...] = ikh
    ikl_ref[...] = ikl
    iw_ref[...] = idx[:, 320:320 + IDX_HEADS]


def _inproj_weights(w_in):
    w = W_BRANCH
    o = 0
    parts = {}
    for name, width in (("u", w), ("q", w), ("k", w), ("v", w), ("iq", IDX_HEADS * IDX_DIM),
                        ("ik", IDX_DIM), ("iw", IDX_HEADS), ("rgx", w), ("rgg", w), ("rw", RW_COLS)):
        parts[name] = w_in[:, o:o + width]
        o += width
    wq = parts["q"] * (ATT_DIM ** -0.5 * math.log2(math.e))
    main = jnp.concatenate([parts["u"], wq, _rot_half_cols(wq, ATT_HEADS, ATT_DIM),
                            parts["k"], _rot_half_cols(parts["k"], ATT_HEADS, ATT_DIM),
                            parts["v"], parts["rgx"], parts["rgg"], parts["rw"]], axis=1)
    pad = (-main.shape[1]) % V7X_LANES
    main = jnp.pad(main, ((0, 0), (0, pad)))
    idx = jnp.concatenate([parts["iq"], _rot_half_cols(parts["iq"], IDX_HEADS, IDX_DIM),
                           parts["ik"], _rot_half_cols(parts["ik"], 1, IDX_DIM), parts["iw"]], axis=1)
    idx = jnp.pad(idx, ((0, 0), (0, 384 - idx.shape[1])))
    ih, il = _split_bf16(idx)
    return main.astype(BF16), ih, il


def _inproj(x2, mod_l, g, w_in, tabs, seq):
    n, d = x2.shape
    tm = min(512, seq)
    tpb = seq // tm
    wm, wih, wil = _inproj_weights(w_in)
    w = W_BRANCH
    row = lambda width: pl.BlockSpec((tm, width), lambda i: (i, 0))
    full = lambda a: pl.BlockSpec(a.shape, lambda i: (0,) * a.ndim)
    shp = lambda width, dt: jax.ShapeDtypeStruct((n, width), dt)
    tab = row(V7X_LANES)
    return pl.pallas_call(
        _inproj_kernel,
        grid=(n // tm,),
        in_specs=[row(d), pl.BlockSpec((1, 6, d), lambda i: (i // tpb, 0, 0)), full(g),
                  full(wm), full(wih), full(wil), tab, tab, tab, tab],
        out_specs=[row(w), row(w), row(w), row(w), row(128), row(128), row(IDX_DIM), row(IDX_DIM),
                   row(IDX_HEADS), row(w), row(w), row(RW_COLS)],
        out_shape=[shp(w, F32), shp(w, BF16), shp(w, BF16), shp(w, BF16), shp(128, BF16), shp(128, BF16),
                   shp(IDX_DIM, BF16), shp(IDX_DIM, BF16), shp(IDX_HEADS, F32),
                   shp(w, F32), shp(w, F32), shp(RW_COLS, F32)],
        compiler_params=_cparams("parallel"),
        name="inproj",
    )(x2, mod_l, g, wm, wih, wil, *tabs)


def _s5_prep_kernel(lre_r, lim_r, st_r, lre_c, lim_c, st_c, bre_ref, bim_ref,
                    pre_ref, pim_ref, bbre_ref, bbim_ref):
    kk = (lax.broadcasted_iota(I32, (V7X_SUBLANES, 1), 0) + 1).astype(F32)
    step = jnp.exp(st_r[...])
    mag = jnp.exp(kk * (lre_r[...] * step))
    ang = kk * (lim_r[...] * step)
    pre_ref[...] = mag * jnp.cos(ang)
    pim_ref[...] = mag * jnp.sin(ang)
    a = lre_c[...]
    b = lim_c[...]
    stc = jnp.exp(st_c[...])
    m1 = jnp.exp(a * stc)
    nr = m1 * jnp.cos(b * stc) - 1.0
    ni = m1 * jnp.sin(b * stc)
    den = a * a + b * b
    cr = (nr * a + ni * b) / den
    ci = (ni * a - nr * b) / den
    bre = bre_ref[...]
    bim = bim_ref[...]
    bbre_ref[...] = cr * bre - ci * bim
    bbim_ref[...] = cr * bim + ci * bre


def _block_diag(blocks):
    g, r, c = blocks.shape
    eye = jnp.eye(g, dtype=blocks.dtype)
    return (blocks[:, :, None, :] * eye[:, None, :, None]).reshape(g * r, g * c)


def _s5_params(lam_re, lam_im, log_step, b_re, b_im, c_re, c_im):
    g, p, c = b_re.shape
    ns = g * p
    st = jnp.repeat(log_step, p)
    rows = [a.reshape(1, ns) for a in (lam_re, lam_im, st)]
    cols = [a.reshape(ns, 1) for a in (lam_re, lam_im, st)]
    full = lambda a: pl.BlockSpec(a.shape, lambda: (0,) * a.ndim)
    args = rows + cols + [b_re.reshape(ns, c), b_im.reshape(ns, c)]
    pre, pim, bbre, bbim = pl.pallas_call(
        _s5_prep_kernel,
        in_specs=[full(a) for a in args],
        out_specs=[pl.BlockSpec((V7X_SUBLANES, ns), lambda: (0, 0))] * 2
        + [pl.BlockSpec((ns, c), lambda: (0, 0))] * 2,
        out_shape=[jax.ShapeDtypeStruct((V7X_SUBLANES, ns), F32)] * 2
        + [jax.ShapeDtypeStruct((ns, c), F32)] * 2,
        name="s5_prep",
    )(*args)
    bmat = jnp.concatenate([_block_diag(bbre.reshape(g, p, c).transpose(0, 2, 1)),
                            _block_diag(bbim.reshape(g, p, c).transpose(0, 2, 1))], axis=1)
    cmat = jnp.concatenate([_block_diag(c_re.transpose(0, 2, 1)),
                            -_block_diag(c_im.transpose(0, 2, 1))], axis=0)
    return pre, pim, bmat.astype(BF16), cmat.astype(BF16)


def _s5_kernel(u_ref, bmat_ref, cmat_ref, pre_ref, pim_ref, d_ref, gw_ref, gb_ref,
               o_ref, h_ref, carry_ref, *, chunk, ns):
    @pl.when(pl.program_id(1) == 0)
    def _():
        carry_ref[...] = jnp.zeros_like(carry_ref)

    u = u_ref[...]
    bu = _dot(u.astype(BF16), bmat_ref[...])
    nt = chunk // V7X_SUBLANES
    xre = bu[:, :ns].reshape(nt, V7X_SUBLANES, ns)
    xim = bu[:, ns:].reshape(nt, V7X_SUBLANES, ns)
    pre = pre_ref[...]
    pim = pim_ref[...]
    rowi = lax.broadcasted_iota(I32, (V7X_SUBLANES, 1), 0)
    for s in (1, 2, 4):
        cre = jnp.where(rowi >= s, pre[s - 1:s], 0.0)[None]
        cim = jnp.where(rowi >= s, pim[s - 1:s], 0.0)[None]
        sre = pltpu.roll(xre, s, 1)
        sim = pltpu.roll(xim, s, 1)
        xre, xim = xre + (cre * sre - cim * sim), xim + (cre * sim + cim * sre)
    cr = carry_ref[0:1, :]
    ci = carry_ref[1:2, :]
    for t in range(nt):
        hre = xre[t] + (pre * cr - pim * ci)
        him = xim[t] + (pre * ci + pim * cr)
        h_ref[t * V7X_SUBLANES:(t + 1) * V7X_SUBLANES, 0:ns] = hre
        h_ref[t * V7X_SUBLANES:(t + 1) * V7X_SUBLANES, ns:2 * ns] = him
        cr = hre[V7X_SUBLANES - 1:V7X_SUBLANES]
        ci = him[V7X_SUBLANES - 1:V7X_SUBLANES]
    carry_ref[0:1, :] = cr
    carry_ref[1:2, :] = ci
    y = _dot(h_ref[...].astype(BF16), cmat_ref[...]) + d_ref[...] * u
    y = _gelu(y)
    z = _dot(y.astype(BF16), gw_ref[...]) + gb_ref[...]
    o_ref[...] = y * _sigmoid(z)


def _s5_mixer(u, params, d_skip, glu_w, glu_b, bsz, seq):
    pre, pim, bmat, cmat = params
    n, w = u.shape
    ns = pre.shape[1]
    chunk = min(256, seq)
    cpb = seq // chunk
    full = lambda a: pl.BlockSpec(a.shape, lambda b, c: (0,) * a.ndim)
    row = pl.BlockSpec((chunk, w), lambda b, c: (b * cpb + c, 0))
    args = (u, bmat, cmat, pre, pim, d_skip.reshape(1, w), glu_w.astype(BF16), glu_b.reshape(1, w))
    return pl.pallas_call(
        functools.partial(_s5_kernel, chunk=chunk, ns=ns),
        grid=(bsz, cpb),
        in_specs=[row] + [full(a) for a in args[1:]],
        out_specs=row,
        out_shape=jax.ShapeDtypeStruct((n, w), F32),
        scratch_shapes=[pltpu.VMEM((chunk, 2 * ns), F32), pltpu.VMEM((V7X_SUBLANES, ns), F32)],
        compiler_params=_cparams("parallel", "arbitrary"),
        name="s5_mixer",
    )(*args)


def _rglru_kernel(x_ref, gate_ref, cw_ref, cb_ref, wr_ref, br_ref, wi_ref, bi_ref, lam_ref,
                  o_ref, tail_ref, carry_ref, *, chunk):
    w = W_BRANCH

    @pl.when(pl.program_id(1) == 0)
    def _():
        tail_ref[...] = jnp.zeros_like(tail_ref)
        carry_ref[...] = jnp.zeros_like(carry_ref)

    nt = chunk // V7X_SUBLANES
    x = x_ref[...]
    tail = tail_ref[...]
    rowc = lax.broadcasted_iota(I32, (chunk, 1), 0)
    cw = cw_ref[...]
    xc = cw[CONV_WIDTH - 1:CONV_WIDTH] * x + cb_ref[...]
    for sh in range(1, CONV_WIDTH):
        prev = jnp.broadcast_to(pltpu.roll(tail, sh, 0)[None], (nt, V7X_SUBLANES, w)).reshape(chunk, w)
        xs = jnp.where(rowc < sh, prev, pltpu.roll(x, sh, 0))
        xc = xc + cw[CONV_WIDTH - 1 - sh:CONV_WIDTH - sh] * xs
    tail_ref[...] = x[chunk - V7X_SUBLANES:chunk]
    xb = xc.astype(BF16)
    r = _sigmoid(_dot(xb, wr_ref[...]) + br_ref[...])
    i = _sigmoid(_dot(xb, wi_ref[...]) + bi_ref[...])
    log_a = (-RG_C) * r * _softplus(-lam_ref[...])
    a = jnp.exp(log_a)
    mult = jnp.sqrt(1.0 - jnp.exp(2.0 * log_a))
    b = mult * i * xc
    a3 = a.reshape(nt, V7X_SUBLANES, w)
    b3 = b.reshape(nt, V7X_SUBLANES, w)
    rowi = lax.broadcasted_iota(I32, (1, V7X_SUBLANES, 1), 1)
    for s in (1, 2, 4):
        keep = rowi >= s
        ash = jnp.where(keep, pltpu.roll(a3, s, 1), 1.0)
        bsh = jnp.where(keep, pltpu.roll(b3, s, 1), 0.0)
        b3 = b3 + a3 * bsh
        a3 = a3 * ash
    h = carry_ref[0:1, :]
    gate = _gelu(gate_ref[...])
    for t in range(nt):
        ht = b3[t] + a3[t] * h
        o_ref[t * V7X_SUBLANES:(t + 1) * V7X_SUBLANES, :] = ht * gate[t * V7X_SUBLANES:(t + 1) * V7X_SUBLANES]
        h = ht[V7X_SUBLANES - 1:V7X_SUBLANES]
    carry_ref[0:1, :] = h


def _rglru_mixer(xr, gate, conv_w, conv_b, w_r, b_r, w_i, b_i, lam, bsz, seq):
    n, w = xr.shape
    chunk = min(256, seq)
    cpb = seq // chunk
    full = lambda a: pl.BlockSpec(a.shape, lambda b, c: (0,) * a.ndim)
    row = pl.BlockSpec((chunk, w), lambda b, c: (b * cpb + c, 0))
    args = (xr, gate, conv_w, conv_b.reshape(1, w), _block_diag(w_r).astype(BF16), b_r.reshape(1, w),
            _block_diag(w_i).astype(BF16), b_i.reshape(1, w), lam.reshape(1, w))
    return pl.pallas_call(
        functools.partial(_rglru_kernel, chunk=chunk),
        grid=(bsz, cpb),
        in_specs=[row, row] + [full(a) for a in args[2:]],
        out_specs=row,
        out_shape=jax.ShapeDtypeStruct((n, w), F32),
        scratch_shapes=[pltpu.VMEM((V7X_SUBLANES, w), F32), pltpu.VMEM((V7X_SUBLANES, w), F32)],
        compiler_params=_cparams("parallel", "arbitrary"),
        name="rglru_mixer",
    )(*args)


RW_CHUNK = 64


def _segsum(x, ones_bf16):
    hi, lo = _split_bf16(x)
    return _dot(hi, ones_bf16) + _dot(lo, ones_bf16)


def _stack_heads(x, head_of_lane):
    return jnp.concatenate([jnp.where(head_of_lane == h, x, 0.0) for h in range(RW_HEADS)], axis=0)


def _rwkv_kernel(f_ref, mu_ref, w0_ref, wup_ref, a0_ref, aup_ref, gup_ref, kk_ref, ka_ref, rk_ref,
                 gnw_ref, gnb_ref, ones_ref, o_ref, tail_ref, state_ref, *, bsz, chunk):
    w = W_BRANCH
    sl = V7X_SUBLANES
    n4 = RW_HEADS * chunk

    @pl.when(pl.program_id(0) == 0)
    def _():
        tail_ref[...] = jnp.zeros_like(tail_ref)
        state_ref[...] = jnp.zeros_like(state_ref)

    ones = ones_ref[...]
    rowc = lax.broadcasted_iota(I32, (chunk, 1), 0)
    head_of_lane = lax.broadcasted_iota(I32, (1, w), 1) // RW_DIM
    ri = lax.broadcasted_iota(I32, (n4, n4), 0)
    ci = lax.broadcasted_iota(I32, (n4, n4), 1)
    strict = ci < ri
    incl = ci <= ri
    same16 = (ri // 16) == (ci // 16)
    same32 = (ri // 32) == (ci // 32)
    eye = (ri == ci).astype(F32)
    tri_c = (lax.broadcasted_iota(I32, (chunk, chunk), 1)
             <= lax.broadcasted_iota(I32, (chunk, chunk), 0)).astype(BF16)

    for b in range(bsz):
        f = f_ref[b]
        tail = tail_ref[b]
        prev = jnp.where(rowc < 1,
                         jnp.broadcast_to(pltpu.roll(tail, 1, 0)[None], (chunk // sl, sl, RW_COLS)).reshape(chunk, RW_COLS),
                         pltpu.roll(f, 1, 0))
        tail_ref[b] = f[chunk - sl:chunk]
        f = f + (prev - f) * mu_ref[...]
        r = f[:, 0:w]
        k = f[:, w:2 * w]
        v = f[:, 2 * w:3 * w]
        wd = f[:, 3 * w:3 * w + RW_DECAY_LORA]
        ad = f[:, 3 * w + RW_DECAY_LORA:3 * w + RW_DECAY_LORA + RW_A_LORA]
        gd = f[:, 3 * w + RW_DECAY_LORA + RW_A_LORA:RW_COLS]
        wlog = -_softplus(-(w0_ref[...] + _dot(jnp.tanh(wd).astype(BF16), wup_ref[...]))) - 0.5
        logd = -jnp.exp(wlog)
        a = _sigmoid(a0_ref[...] + _dot(ad.astype(BF16), aup_ref[...]))
        g = _dot(_sigmoid(gd).astype(BF16), gup_ref[...])
        kk = k * kk_ref[...]
        kk = kk / jnp.maximum(jnp.sqrt(_segsum(kk * kk, ones)), 1e-12)
        k2 = k * (1.0 + (a - 1.0) * ka_ref[...])
        beta = kk * a
        l1 = logd.astype(BF16)
        rem = logd - l1.astype(F32)
        l2 = rem.astype(BF16)
        l3 = (rem - l2.astype(F32)).astype(BF16)
        cum = _dot(tri_c, l1) + _dot(tri_c, l2) + _dot(tri_c, l3)
        p = jnp.exp(cum)
        pinv = jnp.exp(-cum)
        kt = _stack_heads(kk * jnp.exp(cum - logd), head_of_lane).astype(BF16)
        rt = _stack_heads(r * p, head_of_lane).astype(BF16)
        ki = _stack_heads(k2 * pinv, head_of_lane)
        bi = _stack_heads(beta * pinv, head_of_lane)
        vs = _stack_heads(v, head_of_lane)
        pl_row = p[chunk - 1:chunk]
        kib = ki.astype(BF16)
        bib = bi.astype(BF16)
        vsb = vs.astype(BF16)
        a_bk = jnp.where(strict, _dot_nt(kt, bib), 0.0)
        a_vk = jnp.where(strict, _dot_nt(kt, kib), 0.0)
        b_bk = jnp.where(incl, _dot_nt(rt, bib), 0.0)
        b_vk = jnp.where(incl, _dot_nt(rt, kib), 0.0)
        a0 = jnp.where(same16, a_bk, 0.0)
        a0b = a0.astype(BF16)
        tinv = eye - a0
        pw = _dot(a0b, a0b)
        for _ in range(3):
            pwb = pw.astype(BF16)
            tinv = tinv + _dot(tinv.astype(BF16), pwb)
            pw = _dot(pwb, pwb)
        a1 = jnp.where(same32 & jnp.logical_not(same16), a_bk, 0.0).astype(BF16)
        tb = tinv.astype(BF16)
        tinv = tinv - _dot(_dot(tb, a1).astype(BF16), tb)
        a2 = jnp.where(same32, 0.0, a_bk).astype(BF16)
        tb = tinv.astype(BF16)
        tinv = tinv - _dot(_dot(tb, a2).astype(BF16), tb)
        st = state_ref[b]
        stb = st.astype(BF16)
        wmat = _dot_nt(kt, stb) + _dot(a_vk.astype(BF16), vsb)
        u = _dot(tinv.astype(BF16), wmat.astype(BF16))
        ub = u.astype(BF16)
        ys = _dot_nt(rt, stb) + _dot(b_vk.astype(BF16), vsb) - _dot(b_bk.astype(BF16), ub)
        y = ys[0:chunk]
        for h in range(1, RW_HEADS):
            y = y + ys[h * chunk:(h + 1) * chunk]
        kip = (ki * pl_row).astype(BF16)
        bip = (bi * pl_row).astype(BF16)
        state_ref[b] = st * pl_row + _dot(vs.T.astype(BF16), kip) - _dot(u.T.astype(BF16), bip)
        mean = _segsum(y, ones) * (1.0 / RW_DIM)
        yc = y - mean
        var = _segsum(yc * yc, ones) * (1.0 / RW_DIM)
        yn = yc * lax.rsqrt(var + RW_GN_EPS) * gnw_ref[...] + gnb_ref[...]
        bonus = _segsum(r * k2 * rk_ref[...], ones) * v
        o_ref[b] = (yn + bonus) * g


def _rwkv_mixer(feat, mu, w0, w_up, a0, a_up, g_up, k_k, k_a, r_k, gn_w, gn_b, bsz, seq):
    w = W_BRANCH
    chunk = min(RW_CHUNK, seq)
    f3 = feat.reshape(bsz, seq, RW_COLS)
    row1 = lambda a: a.reshape(1, -1)
    ones = _block_diag(jnp.ones((RW_HEADS, RW_DIM, RW_DIM), F32)).astype(BF16)
    args = (f3, row1(mu), row1(w0), w_up.astype(BF16), row1(a0), a_up.astype(BF16), g_up.astype(BF16),
            row1(k_k), row1(k_a), row1(r_k), row1(gn_w), row1(gn_b), ones)
    full = lambda a: pl.BlockSpec(a.shape, lambda c: (0,) * a.ndim)
    n4 = RW_HEADS * chunk
    out = pl.pallas_call(
        functools.partial(_rwkv_kernel, bsz=bsz, chunk=chunk),
        grid=(seq // chunk,),
        in_specs=[pl.BlockSpec((bsz, chunk, RW_COLS), lambda c: (0, c, 0))] + [full(a) for a in args[1:]],
        out_specs=pl.BlockSpec((bsz, chunk, w), lambda c: (0, c, 0)),
        out_shape=jax.ShapeDtypeStruct((bsz, seq, w), F32),
        scratch_shapes=[pltpu.VMEM((bsz, V7X_SUBLANES, RW_COLS), F32), pltpu.VMEM((bsz, n4, w), F32)],
        compiler_params=_cparams("arbitrary"),
        name="rwkv_mixer",
    )(*args)
    return out.reshape(bsz * seq, w)


DSA_TQ = 128
DSA_KC = 512


DSA_SUB = 128
DSA_VROWS = ATT_DIM + 16
KEY_NEG_INF = INT_MIN + 0x7FFFFF


def _ordered_to_f32(key):
    return lax.bitcast_convert_type(jnp.where(key >= 0, key, key ^ 0x7FFFFFFF), F32)


def _dsa_kernel(qT_ref, k4_ref, vx_ref, iq3_ref, ik3_ref, iwT_ref, o_ref, sc_ref, lga_ref, lgb_ref,
                *, seq, tq, kc, topk):
    i = pl.program_id(1)
    nchunks = lax.div((i + 1) * tq + (kc - 1), kc)
    qpos = i * tq + lax.broadcasted_iota(I32, (1, tq), 1)
    rows = lax.broadcasted_iota(I32, (kc, 1), 0)
    wT = iwT_ref[0]
    iq3 = iq3_ref[0, 0]

    def score_body(c, carry):
        s0 = pl.multiple_of(c * kc, kc)
        d = _dot(ik3_ref[pl.ds(s0, kc), :], iq3)
        sc = wT[0:1] * jnp.maximum(d[:, 0:tq], 0.0)
        for h in range(1, IDX_HEADS):
            sc = sc + wT[h:h + 1] * jnp.maximum(d[:, h * tq:(h + 1) * tq], 0.0)
        sc_ref[pl.ds(s0, kc), :] = jnp.where(s0 + rows <= qpos, sc, -jnp.inf)
        return carry

    lax.fori_loop(0, nchunks, score_body, 0)

    def count(ind_fn):
        def body(c, acc):
            s0 = pl.multiple_of(c * kc, kc)
            ind = ind_fn(sc_ref[pl.ds(s0, kc), :], s0 + rows)
            return acc + ind.reshape(kc // V7X_SUBLANES, V7X_SUBLANES, tq).sum(axis=0)
        acc = lax.fori_loop(0, nchunks, body, jnp.zeros((V7X_SUBLANES, tq), I32))
        return jnp.sum(acc, axis=0, keepdims=True)

    c0 = count(lambda s, idx: jnp.where(s >= 0.0, 1, 0))
    ans = jnp.where(c0 >= topk, 0, INT_MIN).astype(I32)

    def bit_body(j, ans):
        cand = ans | lax.shift_left(jnp.int32(1), 30 - j)
        cf = _ordered_to_f32(cand)
        cnt = count(lambda s, idx: jnp.where(s >= cf, 1, 0))
        return jnp.where(cand <= KEY_NEG_INF, cand, jnp.where(cnt >= topk, cand, ans))

    thr = _ordered_to_f32(lax.fori_loop(0, 31, bit_body, ans))
    cnt_gt = count(lambda s, idx: jnp.where(s > thr, 1, 0))
    cnt_ge = count(lambda s, idx: jnp.where(s >= thr, 1, 0))
    need = topk - cnt_gt
    tie = jnp.where(cnt_ge > topk, jnp.where(thr > -jnp.inf, 1, 0), 0)
    nbits = max(1, (seq - 1).bit_length())

    def tie_search():
        def jb(j, jans):
            cand = jans | lax.shift_left(jnp.int32(1), nbits - 1 - j)
            f = count(lambda s, idx: jnp.where(s == thr, jnp.where(idx < cand, 1, 0), 0))
            return jnp.where(f < need, cand, jans)
        return lax.fori_loop(0, nbits, jb, jnp.zeros((1, tq), I32))

    jtie = lax.cond(jnp.max(tie) > 0, tie_search, lambda: jnp.zeros((1, tq), I32))
    jsel = jnp.where(thr == -jnp.inf, qpos, jnp.where(tie > 0, jtie, seq))

    q_all = qT_ref[0, 0]
    hd = ATT_DIM

    def logits_into(buf_ref, c):
        s0 = pl.multiple_of(c * kc, kc)
        for h in range(ATT_HEADS):
            buf_ref[h] = _dot(k4_ref[0, h, pl.ds(s0, kc), :], q_all[h * hd:(h + 1) * hd])

    def softmax_step(buf_ref, c, live, state):
        s0 = pl.multiple_of(c * kc, kc)
        sch = sc_ref[pl.ds(s0, kc), :]
        dead = jnp.where(live, 0.0, NEG_BIG)
        bias = jnp.where(sch > thr, dead,
                         jnp.where(sch == thr, jnp.where(s0 + rows <= jsel, dead, NEG_BIG), NEG_BIG))
        for h in range(ATT_HEADS):
            m, acc = state[2 * h], state[2 * h + 1]
            lg = buf_ref[h] + bias
            mn = jnp.maximum(m, jnp.max(lg, axis=0, keepdims=True))
            p = jnp.exp2(lg - mn).astype(BF16)
            state[2 * h] = mn
            state[2 * h + 1] = jnp.exp2(m - mn) * acc + _dot(vx_ref[0, c, h], p)

    last = nchunks - 1
    logits_into(lga_ref, 0)

    def att_body(t, carry):
        state = list(carry)
        c0 = 2 * t
        c1 = jnp.minimum(c0 + 1, last)
        logits_into(lgb_ref, c1)
        softmax_step(lga_ref, c0, True, state)
        logits_into(lga_ref, jnp.minimum(c0 + 2, last))
        softmax_step(lgb_ref, c1, c0 + 1 <= last, state)
        return tuple(state)

    init = (jnp.full((1, tq), NEG_BIG, F32), jnp.zeros((DSA_VROWS, tq), F32)) * ATT_HEADS
    res = lax.fori_loop(0, lax.div(nchunks + 1, 2), att_body, init)
    for h in range(ATT_HEADS):
        acc = res[2 * h + 1]
        o_ref[0, 0, h * hd:(h + 1) * hd, :] = acc[0:hd] / acc[hd:hd + 1]


def _dsa_mixer(q, k, v, iqh, iql, ikh, ikl, iw, bsz, seq):
    w = W_BRANCH
    tq = min(DSA_TQ, seq)
    kc = min(DSA_KC, seq)
    nq = seq // tq
    nc = seq // kc
    topk = min(TOPK_MAX, seq // 4)
    assert kc >= topk and kc % DSA_SUB == 0
    qT = q.reshape(bsz, nq, tq, w).transpose(0, 1, 3, 2)
    k4 = k.reshape(bsz, seq, ATT_HEADS, ATT_DIM).transpose(0, 2, 1, 3)
    vT = v.reshape(bsz, nc, kc, ATT_HEADS, ATT_DIM).transpose(0, 1, 3, 4, 2)
    vx = jnp.concatenate([vT, jnp.ones((bsz, nc, ATT_HEADS, DSA_VROWS - ATT_DIM, kc), BF16)], axis=3)
    ik3 = jnp.concatenate([ikh, ikh, ikl], axis=1)
    iq3 = jnp.stack([iqh, iql, iqh], axis=1).reshape(bsz, nq, tq, 3, IDX_HEADS, IDX_DIM)
    iq3 = iq3.transpose(0, 1, 3, 5, 4, 2).reshape(bsz, nq, 3 * IDX_DIM, IDX_HEADS * tq)
    iwT = iw.reshape(bsz, seq, IDX_HEADS).transpose(0, 2, 1)
    qblk = lambda rows, cols: pl.BlockSpec((1, 1, rows, cols), lambda b, i: (b, i, 0, 0))
    out = pl.pallas_call(
        functools.partial(_dsa_kernel, seq=seq, tq=tq, kc=kc, topk=topk),
        grid=(bsz, nq),
        in_specs=[qblk(w, tq),
                  pl.BlockSpec((1, ATT_HEADS, seq, ATT_DIM), lambda b, i: (b, 0, 0, 0)),
                  pl.BlockSpec((1, nc, ATT_HEADS, DSA_VROWS, kc), lambda b, i: (b, 0, 0, 0, 0)),
                  qblk(3 * IDX_DIM, IDX_HEADS * tq),
                  pl.BlockSpec((seq, 3 * IDX_DIM), lambda b, i: (b, 0)),
                  pl.BlockSpec((1, IDX_HEADS, tq), lambda b, i: (b, 0, i))],
        out_specs=qblk(w, tq),
        out_shape=jax.ShapeDtypeStruct((bsz, nq, w, tq), F32),
        scratch_shapes=[pltpu.VMEM((seq, tq), F32), pltpu.VMEM((ATT_HEADS, kc, tq), F32),
                        pltpu.VMEM((ATT_HEADS, kc, tq), F32)],
        compiler_params=_cparams("parallel", "arbitrary"),
        name="dsa_mixer",
    )(qT, k4, vx, iq3, ik3, iwT)
    return out.transpose(0, 1, 3, 2).reshape(bsz * seq, w)


def _merge_kernel(x_ref, mod_ref, g_ref, o0_ref, o1_ref, o2_ref, o3_ref, gw_ref, gb_ref, bw_ref, ow_ref, x1_ref):
    d = x_ref.shape[1]
    mod = mod_ref[0]
    x = x_ref[...]
    hb = _modulated_norm(x, g_ref[...], mod[1:2], mod[0:1]).astype(BF16)
    mixed = jnp.zeros(x.shape, F32)
    for n, o_ref in enumerate((o0_ref, o1_ref, o2_ref, o3_ref)):
        gate = _sigmoid(_dot(hb, gw_ref[:, n * d:(n + 1) * d]) + gb_ref[:, n * d:(n + 1) * d])
        mixed = mixed + gate * _dot(o_ref[...].astype(BF16), bw_ref[n])
    x1_ref[...] = x + mod[2:3] * _dot(mixed.astype(BF16), ow_ref[...])


def _merge(x2, mod_l, g, branches, gate_w, gate_b, branch_w, out_w, seq):
    n, d = x2.shape
    w = W_BRANCH
    tm = min(512, seq)
    tpb = seq // tm
    row = lambda width: pl.BlockSpec((tm, width), lambda i: (i, 0))
    full = lambda a: pl.BlockSpec(a.shape, lambda i: (0,) * a.ndim)
    args = (gate_w.astype(BF16), gate_b.reshape(1, -1), branch_w.astype(BF16), out_w.astype(BF16))
    return pl.pallas_call(
        _merge_kernel,
        grid=(n // tm,),
        in_specs=[row(d), pl.BlockSpec((1, 6, d), lambda i: (i // tpb, 0, 0)), full(g)]
        + [row(w)] * N_BRANCH + [full(a) for a in args],
        out_specs=row(d),
        out_shape=jax.ShapeDtypeStruct((n, d), F32),
        compiler_params=_cparams("parallel"),
        name="merge",
    )(x2, mod_l, g, *branches, *args)


def _route_combine(scores, rb):
    biased = scores + rb
    col = lambda a, e: a[:, e:e + 1]
    npg = EXP_PER_GROUP
    gs = []
    for g in range(N_GROUPS):
        best = None
        for j1 in range(npg):
            for j2 in range(j1 + 1, npg):
                s = col(biased, npg * g + j1) + col(biased, npg * g + j2)
                best = s if best is None else jnp.maximum(best, s)
        gs.append(best)
    bg = jnp.zeros(gs[0].shape, I32)
    bv = gs[0]
    for g in range(1, N_GROUPS):
        upd = gs[g] > bv
        bv = jnp.where(upd, gs[g], bv)
        bg = jnp.where(upd, g, bg)
    bsel, ssel = [], []
    for j in range(npg):
        bj, sj = col(biased, j), col(scores, j)
        for g in range(1, N_GROUPS):
            bj = jnp.where(bg == g, col(biased, npg * g + j), bj)
            sj = jnp.where(bg == g, col(scores, npg * g + j), sj)
        bsel.append(bj)
        ssel.append(sj)
    i1, v1, s1 = jnp.zeros(bg.shape, I32), bsel[0], ssel[0]
    for j in range(1, npg):
        upd = bsel[j] > v1
        v1 = jnp.where(upd, bsel[j], v1)
        s1 = jnp.where(upd, ssel[j], s1)
        i1 = jnp.where(upd, j, i1)
    i2, v2, s2 = jnp.zeros(bg.shape, I32), jnp.where(i1 == 0, -jnp.inf, bsel[0]), ssel[0]
    for j in range(1, npg):
        cand = jnp.where(i1 == j, -jnp.inf, bsel[j])
        upd = cand > v2
        v2 = jnp.where(upd, cand, v2)
        s2 = jnp.where(upd, ssel[j], s2)
        i2 = jnp.where(upd, j, i2)
    tot = s1 + s2
    lane = lax.broadcasted_iota(I32, scores.shape, 1)
    return (jnp.where(lane == npg * bg + i1, s1 / tot, 0.0)
            + jnp.where(lane == npg * bg + i2, s2 / tot, 0.0))


def _route_kernel(x1_ref, mod_ref, g_ref, rwh_ref, rwl_ref, rb_ref, h2_ref, comb_ref):
    mod = mod_ref[0]
    h2 = _modulated_norm(x1_ref[...], g_ref[...], mod[4:5], mod[3:4])
    h2_ref[...] = h2.astype(BF16)
    hh, hl = _split_bf16(h2)
    logits = _dot(hh, rwh_ref[...]) + _dot(hh, rwl_ref[...]) + _dot(hl, rwh_ref[...])
    comb_ref[...] = _route_combine(_sigmoid(logits[:, 0:N_EXPERTS]), rb_ref[...])


def _route(x1, mod_l, g, router_w, router_b, seq):
    n, d = x1.shape
    tm = min(512, seq)
    tpb = seq // tm
    rwh, rwl = _split_bf16(jnp.pad(router_w, ((0, 0), (0, V7X_LANES - N_EXPERTS))))
    row = lambda width: pl.BlockSpec((tm, width), lambda i: (i, 0))
    full = lambda a: pl.BlockSpec(a.shape, lambda i: (0,) * a.ndim)
    rb = router_b.reshape(1, N_EXPERTS)
    return pl.pallas_call(
        _route_kernel,
        grid=(n // tm,),
        in_specs=[row(d), pl.BlockSpec((1, 6, d), lambda i: (i // tpb, 0, 0)), full(g), full(rwh), full(rwl), full(rb)],
        out_specs=[row(d), row(N_EXPERTS)],
        out_shape=[jax.ShapeDtypeStruct((n, d), BF16), jax.ShapeDtypeStruct((n, N_EXPERTS), F32)],
        compiler_params=_cparams("parallel"),
        name="route",
    )(x1, mod_l, g, rwh, rwl, rb)


def _moe_kernel(h2_ref, c_ref, w1_ref, w3_ref, w2_ref, x1_ref, mod_ref, fg_ref, o_ref, acc_ref, *, final_norm):
    e = pl.program_id(1)

    @pl.when(e == 0)
    def _():
        acc_ref[...] = jnp.zeros_like(acc_ref)

    h2 = h2_ref[...]
    a = _dot(h2, w1_ref[0, 0].astype(BF16))
    b = _dot(h2, w3_ref[0, 0].astype(BF16))
    y = _dot((a * _sigmoid(a) * b).astype(BF16), w2_ref[0, 0].astype(BF16))
    acc_ref[...] += c_ref[0] * y

    @pl.when(e == pl.num_programs(1) - 1)
    def _():
        out = x1_ref[...] + mod_ref[0][5:6] * acc_ref[...]
        if final_norm:
            ms = jnp.mean(out * out, axis=-1, keepdims=True)
            out = out * lax.rsqrt(ms + NORM_EPS) * fg_ref[...]
        o_ref[...] = out


def _moe_dense(h2, comb, exp_w1, exp_w3, exp_w2, layer, x1, mod_l, final_g, seq, final_norm):
    n, d = x1.shape
    ne, f = exp_w1.shape[1], exp_w1.shape[3]
    tm = min(1024, seq)
    tpb = seq // tm
    c3 = comb.T.reshape(ne, n, 1)
    row = lambda width: pl.BlockSpec((tm, width), lambda i, e: (i, 0))
    return pl.pallas_call(
        functools.partial(_moe_kernel, final_norm=final_norm),
        grid=(n // tm, ne),
        in_specs=[row(d), pl.BlockSpec((1, tm, 1), lambda i, e: (e, i, 0)),
                  pl.BlockSpec((1, 1, d, f), lambda i, e: (layer, e, 0, 0)),
                  pl.BlockSpec((1, 1, d, f), lambda i, e: (layer, e, 0, 0)),
                  pl.BlockSpec((1, 1, f, d), lambda i, e: (layer, e, 0, 0)),
                  row(d), pl.BlockSpec((1, 6, d), lambda i, e: (i // tpb, 0, 0)),
                  pl.BlockSpec((1, d), lambda i, e: (0, 0))],
        out_specs=row(d),
        out_shape=jax.ShapeDtypeStruct((n, d), F32),
        scratch_shapes=[pltpu.VMEM((tm, d), F32)],
        compiler_params=_cparams("parallel", "arbitrary"),
        name="moe_dense",
    )(h2, c3, exp_w1, exp_w3, exp_w2, x1, mod_l, final_g)


def kernel(x, c, positions, ada_w, ada_b, mix_norm_g, w_in, gate_w, gate_b, branch_w, out_w, s5_lam_re, s5_lam_im, s5_log_step, s5_b_re, s5_b_im, s5_c_re, s5_c_im, s5_d, s5_glu_w, s5_glu_b, rg_conv_w, rg_conv_b, rg_wr, rg_br, rg_wi, rg_bi, rg_lam, rw_mu, rw_w0, rw_w_up, rw_a0, rw_a_up, rw_g_up, rw_k_k, rw_k_a, rw_r_k, rw_gn_w, rw_gn_b, ffn_norm_g, router_w, router_b, exp_w1, exp_w3, exp_w2, final_norm_g):
    bsz, seq, d = x.shape
    depth = ada_w.shape[0]
    n = bsz * seq
    x2 = x.reshape(n, d)
    mod = _ada_mod(c, ada_w, ada_b)
    tabs = _rope_tables(positions)
    fg = final_norm_g.reshape(1, d)
    for l in range(depth):
        mod_l = mod[l]
        g1 = mix_norm_g[l].reshape(1, d)
        (u, q, k, v, iqh, iql, ikh, ikl, iw, rgx, rgg, rwf) = _inproj(x2, mod_l, g1, w_in[l], tabs, seq)
        s5p = _s5_params(s5_lam_re[l], s5_lam_im[l], s5_log_step[l], s5_b_re[l], s5_b_im[l], s5_c_re[l], s5_c_im[l])
        o_s5 = _s5_mixer(u, s5p, s5_d[l], s5_glu_w[l], s5_glu_b[l], bsz, seq)
        o_dsa = _dsa_mixer(q, k, v, iqh, iql, ikh, ikl, iw, bsz, seq)
        o_rg = _rglru_mixer(rgx, rgg, rg_conv_w[l], rg_conv_b[l], rg_wr[l], rg_br[l], rg_wi[l], rg_bi[l],
                            rg_lam[l], bsz, seq)
        o_rw = _rwkv_mixer(rwf, rw_mu[l], rw_w0[l], rw_w_up[l], rw_a0[l], rw_a_up[l], rw_g_up[l], rw_k_k[l],
                           rw_k_a[l], rw_r_k[l], rw_gn_w[l], rw_gn_b[l], bsz, seq)
        x1 = _merge(x2, mod_l, g1, (o_s5, o_dsa, o_rg, o_rw), gate_w[l], gate_b[l], branch_w[l], out_w[l], seq)
        h2, comb = _route(x1, mod_l, ffn_norm_g[l].reshape(1, d), router_w, router_b, seq)
        x2 = _moe_dense(h2, comb, exp_w1, exp_w3, exp_w2, l, x1, mod_l, fg, seq, l == depth - 1)
    return x2.reshape(bsz, seq, d)
```

```python
import functools
import math

import jax
import jax.numpy as jnp
from jax import lax
from jax.experimental import pallas as pl
from jax.experimental.pallas import tpu as pltpu

F32 = jnp.float32
BF16 = jnp.bfloat16
I32 = jnp.int32

W_BRANCH = 256
N_BRANCH = 4
NORM_EPS = 1e-6
S5_GROUP = 16
S5_GROUPS = W_BRANCH // S5_GROUP
S5_STATE = 64
ATT_HEADS = 4
ATT_DIM = W_BRANCH // ATT_HEADS
IDX_HEADS = 4
IDX_DIM = 32
TOPK_MAX = 256
ROPE_THETA = 10000.0
RG_BLOCKS = 4
RG_C = 8.0
CONV_WIDTH = 4
RW_HEADS = 4
RW_DIM = W_BRANCH // RW_HEADS
RW_DECAY_LORA = 32
RW_A_LORA = 32
RW_GATE_LORA = 64
RW_GN_EPS = 64e-5
RW_COLS = 3 * W_BRANCH + RW_DECAY_LORA + RW_A_LORA + RW_GATE_LORA
N_EXPERTS = 16
N_GROUPS = 4
EXP_PER_GROUP = N_EXPERTS // N_GROUPS
D_EXPERT = 512

V7X_SUBLANES = 8
V7X_LANES = 128
V7X_VMEM_LIMIT_BYTES = 56 * 1024 * 1024

INT_MIN = -(2 ** 31)
NEG_BIG = -1e30


def _cparams(*sem, flags=None):
    return pltpu.CompilerParams(dimension_semantics=sem, vmem_limit_bytes=V7X_VMEM_LIMIT_BYTES, flags=flags)


def _split_bf16(x):
    hi = x.astype(BF16)
    lo = (x - hi.astype(F32)).astype(BF16)
    return hi, lo


def _dot(a, b):
    return jnp.dot(a, b, preferred_element_type=F32)


def _dot_nt(a, b):
    return lax.dot_general(a, b, (((1,), (1,)), ((), ())), preferred_element_type=F32)


def _dot_tn(a, b):
    return lax.dot_general(a, b, (((0,), (0,)), ((), ())), preferred_element_type=F32)


def _dot3(a, b):
    ah, al = _split_bf16(a)
    bh, bl = _split_bf16(b)
    return _dot(ah, bh) + _dot(ah, bl) + _dot(al, bh)


def _sigmoid(x):
    return 1.0 / (1.0 + jnp.exp(-x))


def _softplus(x):
    return jnp.maximum(x, 0.0) + jnp.log(1.0 + jnp.exp(-jnp.abs(x)))


def _gelu(x):
    c = math.sqrt(2.0 / math.pi)
    return 0.5 * x * (1.0 + jnp.tanh(c * (x + 0.044715 * (x * x * x))))


def _rot_half_cols(w, n_heads, dim):
    lead = w.shape[0]
    w4 = w.reshape(lead, n_heads, 2, dim // 2)
    return jnp.concatenate([-w4[:, :, 1:2], w4[:, :, 0:1]], axis=2).reshape(lead, n_heads * dim)


def _ada_kernel(c_ref, w_ref, b_ref, o_ref):
    c = c_ref[...]
    ca = c * _sigmoid(c)
    o_ref[0] = _dot3(ca, w_ref[0]) + b_ref[0]


def _ada_mod(c, ada_w, ada_b):
    depth, d, d6 = ada_w.shape
    bsz = c.shape[0]
    rows = -(-bsz // V7X_SUBLANES) * V7X_SUBLANES
    cp = jnp.zeros((rows, d), F32).at[:bsz].set(c)
    out = pl.pallas_call(
        _ada_kernel,
        grid=(depth, d6 // d),
        in_specs=[pl.BlockSpec((rows, d), lambda l, j: (0, 0)),
                  pl.BlockSpec((1, d, d), lambda l, j: (l, 0, j)),
                  pl.BlockSpec((1, 1, d), lambda l, j: (l, 0, j))],
        out_specs=pl.BlockSpec((1, rows, d), lambda l, j: (l, 0, j)),
        out_shape=jax.ShapeDtypeStruct((depth, rows, d6), F32),
        compiler_params=_cparams("parallel", "parallel"),
        name="ada_mod",
    )(cp, ada_w, ada_b.reshape(depth, 1, d6))
    return out[:, :bsz].reshape(depth, bsz, 6, d)


def _rope_kernel(pos_ref, fa_ref, fi_ref, ca_ref, sa_ref, ci_ref, si_ref):
    pos = pos_ref[...]
    ang_a = pos * fa_ref[...]
    ang_i = pos * fi_ref[...]
    ca_ref[...] = jnp.cos(ang_a)
    sa_ref[...] = jnp.sin(ang_a)
    ci_ref[...] = jnp.cos(ang_i)
    si_ref[...] = jnp.sin(ang_i)


def _rope_tables(positions):
    n = positions.size
    pos = positions.reshape(n, 1).astype(F32)

    def freq_row(dim):
        half = dim // 2
        inv = ROPE_THETA ** (-jnp.arange(half, dtype=F32) / half)
        return jnp.tile(inv, V7X_LANES // half).reshape(1, V7X_LANES)

    tm = min(n, 2048)
    row = pl.BlockSpec((tm, V7X_LANES), lambda i: (i, 0))
    frq = pl.BlockSpec((1, V7X_LANES), lambda i: (0, 0))
    shp = jax.ShapeDtypeStruct((n, V7X_LANES), F32)
    return pl.pallas_call(
        _rope_kernel,
        grid=(n // tm,),
        in_specs=[pl.BlockSpec((tm, 1), lambda i: (i, 0)), frq, frq],
        out_specs=[row, row, row, row],
        out_shape=[shp, shp, shp, shp],
        compiler_params=_cparams("parallel"),
        name="rope_tables",
    )(pos, freq_row(ATT_DIM), freq_row(IDX_DIM))


def _modulated_norm(x, g, scale, shift):
    ms = jnp.mean(x * x, axis=-1, keepdims=True)
    return (x * lax.rsqrt(ms + NORM_EPS)) * g * (1.0 + scale) + shift


def _inproj_kernel(x_ref, mod_ref, g_ref, wm_ref, wih_ref, wil_ref,
                   ca_ref, sa_ref, ci_ref, si_ref,
                   u_ref, q_ref, k_ref, v_ref, iqh_ref, iql_ref, ikh_ref, ikl_ref, iw_ref,
                   rgx_ref, rgg_ref, rw_ref):
    w = W_BRANCH
    mod = mod_ref[0]
    h = _modulated_norm(x_ref[...], g_ref[...], mod[1:2], mod[0:1])
    hh, hl = _split_bf16(h)
    main = _dot(hh, wm_ref[...])
    idx = _dot(hh, wih_ref[...]) + _dot(hh, wil_ref[...]) + _dot(hl, wih_ref[...])
    ca = jnp.concatenate([ca_ref[...], ca_ref[...]], axis=1)
    sa = jnp.concatenate([sa_ref[...], sa_ref[...]], axis=1)
    u_ref[...] = main[:, 0:w]
    q_ref[...] = (main[:, w:2 * w] * ca + main[:, 2 * w:3 * w] * sa).astype(BF16)
    k_ref[...] = (main[:, 3 * w:4 * w] * ca + main[:, 4 * w:5 * w] * sa).astype(BF16)
    v_ref[...] = main[:, 5 * w:6 * w].astype(BF16)
    rgx_ref[...] = main[:, 6 * w:7 * w]
    rgg_ref[...] = main[:, 7 * w:8 * w]
    rw_ref[...] = main[:, 8 * w:8 * w + RW_COLS]
    ci = ci_ref[...]
    si = si_ref[...]
    iq = idx[:, 0:128] * ci + idx[:, 128:256] * si
    ik = idx[:, 256:288] * ci[:, 0:IDX_DIM] + idx[:, 288:320] * si[:, 0:IDX_DIM]
    iqh, iql = _split_bf16(iq)
    ikh, ikl = _split_bf16(ik)
    iqh_ref[...] = iqh
    iql_ref[...] = iql
    ikh_ref[...] = ikh
    ikl_ref[...] = ikl
    iw_ref[...] = idx[:, 320:320 + IDX_HEADS]


def _inproj_weights(w_in):
    w = W_BRANCH
    o = 0
    parts = {}
    for name, width in (("u", w), ("q", w), ("k", w), ("v", w), ("iq", IDX_HEADS * IDX_DIM),
                        ("ik", IDX_DIM), ("iw", IDX_HEADS), ("rgx", w), ("rgg", w), ("rw", RW_COLS)):
        parts[name] = w_in[:, o:o + width]
        o += width
    wq = parts["q"] * (ATT_DIM ** -0.5 * math.log2(math.e))
    main = jnp.concatenate([parts["u"], wq, _rot_half_cols(wq, ATT_HEADS, ATT_DIM),
                            parts["k"], _rot_half_cols(parts["k"], ATT_HEADS, ATT_DIM),
                            parts["v"], parts["rgx"], parts["rgg"], parts["rw"]], axis=1)
    pad = (-main.shape[1]) % V7X_LANES
    main = jnp.pad(main, ((0, 0), (0, pad)))
    idx = jnp.concatenate([parts["iq"], _rot_half_cols(parts["iq"], IDX_HEADS, IDX_DIM),
                           parts["ik"], _rot_half_cols(parts["ik"], 1, IDX_DIM), parts["iw"]], axis=1)
    idx = jnp.pad(idx, ((0, 0), (0, 384 - idx.shape[1])))
    ih, il = _split_bf16(idx)
    return main.astype(BF16), ih, il


def _inproj(x2, mod_l, g, w_in, tabs, seq):
    n, d = x2.shape
    tm = min(512, seq)
    tpb = seq // tm
    wm, wih, wil = _inproj_weights(w_in)
    w = W_BRANCH
    row = lambda width: pl.BlockSpec((tm, width), lambda i: (i, 0))
    full = lambda a: pl.BlockSpec(a.shape, lambda i: (0,) * a.ndim)
    shp = lambda width, dt: jax.ShapeDtypeStruct((n, width), dt)
    tab = row(V7X_LANES)
    return pl.pallas_call(
        _inproj_kernel,
        grid=(n // tm,),
        in_specs=[row(d), pl.BlockSpec((1, 6, d), lambda i: (i // tpb, 0, 0)), full(g),
                  full(wm), full(wih), full(wil), tab, tab, tab, tab],
        out_specs=[row(w), row(w), row(w), row(w), row(128), row(128), row(IDX_DIM), row(IDX_DIM),
                   row(IDX_HEADS), row(w), row(w), row(RW_COLS)],
        out_shape=[shp(w, F32), shp(w, BF16), shp(w, BF16), shp(w, BF16), shp(128, BF16), shp(128, BF16),
                   shp(IDX_DIM, BF16), shp(IDX_DIM, BF16), shp(IDX_HEADS, F32),
                   shp(w, F32), shp(w, F32), shp(RW_COLS, F32)],
        compiler_params=_cparams("parallel"),
        name="inproj",
    )(x2, mod_l, g, wm, wih, wil, *tabs)


def _s5_prep_kernel(lre_r, lim_r, st_r, lre_c, lim_c, st_c, bre_ref, bim_ref,
                    pre_ref, pim_ref, bbre_ref, bbim_ref):
    kk = (lax.broadcasted_iota(I32, (V7X_SUBLANES, 1), 0) + 1).astype(F32)
    step = jnp.exp(st_r[...])
    mag = jnp.exp(kk * (lre_r[...] * step))
    ang = kk * (lim_r[...] * step)
    pre_ref[...] = mag * jnp.cos(ang)
    pim_ref[...] = mag * jnp.sin(ang)
    a = lre_c[...]
    b = lim_c[...]
    stc = jnp.exp(st_c[...])
    m1 = jnp.exp(a * stc)
    nr = m1 * jnp.cos(b * stc) - 1.0
    ni = m1 * jnp.sin(b * stc)
    den = a * a + b * b
    cr = (nr * a + ni * b) / den
    ci = (ni * a - nr * b) / den
    bre = bre_ref[...]
    bim = bim_ref[...]
    bbre_ref[...] = cr * bre - ci * bim
    bbim_ref[...] = cr * bim + ci * bre


def _block_diag(blocks):
    g, r, c = blocks.shape
    eye = jnp.eye(g, dtype=blocks.dtype)
    return (blocks[:, :, None, :] * eye[:, None, :, None]).reshape(g * r, g * c)


def _s5_params(lam_re, lam_im, log_step, b_re, b_im, c_re, c_im):
    g, p, c = b_re.shape
    ns = g * p
    st = jnp.repeat(log_step, p)
    rows = [a.reshape(1, ns) for a in (lam_re, lam_im, st)]
    cols = [a.reshape(ns, 1) for a in (lam_re, lam_im, st)]
    full = lambda a: pl.BlockSpec(a.shape, lambda: (0,) * a.ndim)
    args = rows + cols + [b_re.reshape(ns, c), b_im.reshape(ns, c)]
    pre, pim, bbre, bbim = pl.pallas_call(
        _s5_prep_kernel,
        in_specs=[full(a) for a in args],
        out_specs=[pl.BlockSpec((V7X_SUBLANES, ns), lambda: (0, 0))] * 2
        + [pl.BlockSpec((ns, c), lambda: (0, 0))] * 2,
        out_shape=[jax.ShapeDtypeStruct((V7X_SUBLANES, ns), F32)] * 2
        + [jax.ShapeDtypeStruct((ns, c), F32)] * 2,
        name="s5_prep",
    )(*args)
    bmat = jnp.concatenate([_block_diag(bbre.reshape(g, p, c).transpose(0, 2, 1)),
                            _block_diag(bbim.reshape(g, p, c).transpose(0, 2, 1))], axis=1)
    cmat = jnp.concatenate([_block_diag(c_re.transpose(0, 2, 1)),
                            -_block_diag(c_im.transpose(0, 2, 1))], axis=0)
    return pre, pim, bmat.astype(BF16), cmat.astype(BF16)


def _s5_kernel(u_ref, bmat_ref, cmat_ref, pre_ref, pim_ref, d_ref, gw_ref, gb_ref,
               o_ref, h_ref, carry_ref, *, chunk, ns):
    @pl.when(pl.program_id(1) == 0)
    def _():
        carry_ref[...] = jnp.zeros_like(carry_ref)

    u = u_ref[...]
    bu = _dot(u.astype(BF16), bmat_ref[...])
    nt = chunk // V7X_SUBLANES
    xre = bu[:, :ns].reshape(nt, V7X_SUBLANES, ns)
    xim = bu[:, ns:].reshape(nt, V7X_SUBLANES, ns)
    pre = pre_ref[...]
    pim = pim_ref[...]
    rowi = lax.broadcasted_iota(I32, (V7X_SUBLANES, 1), 0)
    for s in (1, 2, 4):
        cre = jnp.where(rowi >= s, pre[s - 1:s], 0.0)[None]
        cim = jnp.where(rowi >= s, pim[s - 1:s], 0.0)[None]
        sre = pltpu.roll(xre, s, 1)
        sim = pltpu.roll(xim, s, 1)
        xre, xim = xre + (cre * sre - cim * sim), xim + (cre * sim + cim * sre)
    cr = carry_ref[0:1, :]
    ci = carry_ref[1:2, :]
    for t in range(nt):
        hre = xre[t] + (pre * cr - pim * ci)
        him = xim[t] + (pre * ci + pim * cr)
        h_ref[t * V7X_SUBLANES:(t + 1) * V7X_SUBLANES, 0:ns] = hre
        h_ref[t * V7X_SUBLANES:(t + 1) * V7X_SUBLANES, ns:2 * ns] = him
        cr = hre[V7X_SUBLANES - 1:V7X_SUBLANES]
        ci = him[V7X_SUBLANES - 1:V7X_SUBLANES]
    carry_ref[0:1, :] = cr
    carry_ref[1:2, :] = ci
    y = _dot(h_ref[...].astype(BF16), cmat_ref[...]) + d_ref[...] * u
    y = _gelu(y)
    z = _dot(y.astype(BF16), gw_ref[...]) + gb_ref[...]
    o_ref[...] = y * _sigmoid(z)


def _s5_mixer(u, params, d_skip, glu_w, glu_b, bsz, seq):
    pre, pim, bmat, cmat = params
    n, w = u.shape
    ns = pre.shape[1]
    chunk = min(256, seq)
    cpb = seq // chunk
    full = lambda a: pl.BlockSpec(a.shape, lambda b, c: (0,) * a.ndim)
    row = pl.BlockSpec((chunk, w), lambda b, c: (b * cpb + c, 0))
    args = (u, bmat, cmat, pre, pim, d_skip.reshape(1, w), glu_w.astype(BF16), glu_b.reshape(1, w))
    return pl.pallas_call(
        functools.partial(_s5_kernel, chunk=chunk, ns=ns),
        grid=(bsz, cpb),
        in_specs=[row] + [full(a) for a in args[1:]],
        out_specs=row,
        out_shape=jax.ShapeDtypeStruct((n, w), F32),
        scratch_shapes=[pltpu.VMEM((chunk, 2 * ns), F32), pltpu.VMEM((V7X_SUBLANES, ns), F32)],
        compiler_params=_cparams("parallel", "arbitrary"),
        name="s5_mixer",
    )(*args)


def _rglru_kernel(x_ref, gate_ref, cw_ref, cb_ref, wr_ref, br_ref, wi_ref, bi_ref, lam_ref,
                  o_ref, tail_ref, carry_ref, *, chunk):
    w = W_BRANCH

    @pl.when(pl.program_id(1) == 0)
    def _():
        tail_ref[...] = jnp.zeros_like(tail_ref)
        carry_ref[...] = jnp.zeros_like(carry_ref)

    nt = chunk // V7X_SUBLANES
    x = x_ref[...]
    tail = tail_ref[...]
    rowc = lax.broadcasted_iota(I32, (chunk, 1), 0)
    cw = cw_ref[...]
    xc = cw[CONV_WIDTH - 1:CONV_WIDTH] * x + cb_ref[...]
    for sh in range(1, CONV_WIDTH):
        prev = jnp.broadcast_to(pltpu.roll(tail, sh, 0)[None], (nt, V7X_SUBLANES, w)).reshape(chunk, w)
        xs = jnp.where(rowc < sh, prev, pltpu.roll(x, sh, 0))
        xc = xc + cw[CONV_WIDTH - 1 - sh:CONV_WIDTH - sh] * xs
    tail_ref[...] = x[chunk - V7X_SUBLANES:chunk]
    xb = xc.astype(BF16)
    r = _sigmoid(_dot(xb, wr_ref[...]) + br_ref[...])
    i = _sigmoid(_dot(xb, wi_ref[...]) + bi_ref[...])
    log_a = (-RG_C) * r * _softplus(-lam_ref[...])
    a = jnp.exp(log_a)
    mult = jnp.sqrt(1.0 - jnp.exp(2.0 * log_a))
    b = mult * i * xc
    a3 = a.reshape(nt, V7X_SUBLANES, w)
    b3 = b.reshape(nt, V7X_SUBLANES, w)
    rowi = lax.broadcasted_iota(I32, (1, V7X_SUBLANES, 1), 1)
    for s in (1, 2, 4):
        keep = rowi >= s
        ash = jnp.where(keep, pltpu.roll(a3, s, 1), 1.0)
        bsh = jnp.where(keep, pltpu.roll(b3, s, 1), 0.0)
        b3 = b3 + a3 * bsh
        a3 = a3 * ash
    h = carry_ref[0:1, :]
    gate = _gelu(gate_ref[...])
    for t in range(nt):
        ht = b3[t] + a3[t] * h
        o_ref[t * V7X_SUBLANES:(t + 1) * V7X_SUBLANES, :] = ht * gate[t * V7X_SUBLANES:(t + 1) * V7X_SUBLANES]
        h = ht[V7X_SUBLANES - 1:V7X_SUBLANES]
    carry_ref[0:1, :] = h


def _rglru_mixer(xr, gate, conv_w, conv_b, w_r, b_r, w_i, b_i, lam, bsz, seq):
    n, w = xr.shape
    chunk = min(256, seq)
    cpb = seq // chunk
    full = lambda a: pl.BlockSpec(a.shape, lambda b, c: (0,) * a.ndim)
    row = pl.BlockSpec((chunk, w), lambda b, c: (b * cpb + c, 0))
    args = (xr, gate, conv_w, conv_b.reshape(1, w), _block_diag(w_r).astype(BF16), b_r.reshape(1, w),
            _block_diag(w_i).astype(BF16), b_i.reshape(1, w), lam.reshape(1, w))
    return pl.pallas_call(
        functools.partial(_rglru_kernel, chunk=chunk),
        grid=(bsz, cpb),
        in_specs=[row, row] + [full(a) for a in args[2:]],
        out_specs=row,
        out_shape=jax.ShapeDtypeStruct((n, w), F32),
        scratch_shapes=[pltpu.VMEM((V7X_SUBLANES, w), F32), pltpu.VMEM((V7X_SUBLANES, w), F32)],
        compiler_params=_cparams("parallel", "arbitrary"),
        name="rglru_mixer",
    )(*args)


RW_CHUNK = 64


def _segsum(x, ones_bf16):
    hi, lo = _split_bf16(x)
    return _dot(hi, ones_bf16) + _dot(lo, ones_bf16)


def _stack_heads(x, head_of_lane):
    return jnp.concatenate([jnp.where(head_of_lane == h, x, 0.0) for h in range(RW_HEADS)], axis=0)


def _rwkv_chunk_kernel(f_ref, fprev_ref, mu_ref, w0_ref, wup_ref, a0_ref, aup_ref, gup_ref, kk_ref, ka_ref,
                       rk_ref, ones_ref, rt2_ref, y1_ref, m_ref, s1_ref, pl_ref, g_ref, bonus_ref,
                       kt_s, rt_s, kib_s, bib_s, vsb_s, vst_s, kip_s, bip_s, a0_s, a1_s, a2_s, avk_s, bbk_s, bvk_s,
                       pw_s, kt2_s, kt2t_s, u0_s, u0t_s, tinv_s, *, bsz, chunk):
    w = W_BRANCH
    sl = V7X_SUBLANES
    n4 = RW_HEADS * chunk
    first = pl.program_id(0) == 0
    ones = ones_ref[...]
    rowc = lax.broadcasted_iota(I32, (chunk, 1), 0)
    head_of_lane = lax.broadcasted_iota(I32, (1, w), 1) // RW_DIM
    ri = lax.broadcasted_iota(I32, (n4, n4), 0)
    ci = lax.broadcasted_iota(I32, (n4, n4), 1)
    strict = ci < ri
    incl = ci <= ri
    same16 = (ri // 16) == (ci // 16)
    same32 = (ri // 32) == (ci // 32)
    eye = (ri == ci).astype(F32)
    tri_c = (lax.broadcasted_iota(I32, (chunk, chunk), 1)
             <= lax.broadcasted_iota(I32, (chunk, chunk), 0)).astype(BF16)

    for b in range(bsz):
        f = f_ref[b]
        tail = jnp.where(first, 0.0, fprev_ref[b])
        prev = jnp.where(rowc < 1,
                         jnp.broadcast_to(pltpu.roll(tail, 1, 0)[None], (chunk // sl, sl, RW_COLS)).reshape(chunk, RW_COLS),
                         pltpu.roll(f, 1, 0))
        f = f + (prev - f) * mu_ref[...]
        r = f[:, 0:w]
        k = f[:, w:2 * w]
        v = f[:, 2 * w:3 * w]
        wd = f[:, 3 * w:3 * w + RW_DECAY_LORA]
        ad = f[:, 3 * w + RW_DECAY_LORA:3 * w + RW_DECAY_LORA + RW_A_LORA]
        gd = f[:, 3 * w + RW_DECAY_LORA + RW_A_LORA:RW_COLS]
        wlog = -_softplus(-(w0_ref[...] + _dot(jnp.tanh(wd).astype(BF16), wup_ref[...]))) - 0.5
        logd = -jnp.exp(wlog)
        a = _sigmoid(a0_ref[...] + _dot(ad.astype(BF16), aup_ref[...]))
        g = _dot(_sigmoid(gd).astype(BF16), gup_ref[...])
        kk = k * kk_ref[...]
        kk = kk / jnp.maximum(jnp.sqrt(_segsum(kk * kk, ones)), 1e-12)
        k2 = k * (1.0 + (a - 1.0) * ka_ref[...])
        beta = kk * a
        l1 = logd.astype(BF16)
        rem = logd - l1.astype(F32)
        l2 = rem.astype(BF16)
        l3 = (rem - l2.astype(F32)).astype(BF16)
        cum = _dot(tri_c, l1) + _dot(tri_c, l2) + _dot(tri_c, l3)
        p = jnp.exp(cum)
        pinv = jnp.exp(-cum)
        kt = _stack_heads(kk * jnp.exp(cum - logd), head_of_lane).astype(BF16)
        rt = _stack_heads(r * p, head_of_lane).astype(BF16)
        ki = _stack_heads(k2 * pinv, head_of_lane)
        bi = _stack_heads(beta * pinv, head_of_lane)
        vs = _stack_heads(v, head_of_lane)
        pl_row = p[chunk - 1:chunk]
        kt_s[b] = kt
        rt_s[b] = rt
        kib_s[b] = ki.astype(BF16)
        bib_s[b] = bi.astype(BF16)
        vsb_s[b] = vs.astype(BF16)
        vst_s[b] = vs.T.astype(BF16)
        kip_s[b] = (ki * pl_row).astype(BF16)
        bip_s[b] = (bi * pl_row).astype(BF16)
        pl_ref[b, 0] = jnp.broadcast_to(pl_row, (sl, w))
        g_ref[b] = g
        bonus_ref[b] = _segsum(r * k2 * rk_ref[...], ones) * v

    for b in range(bsz):
        kt, rt, kib, bib = kt_s[b], rt_s[b], kib_s[b], bib_s[b]
        a_bk = jnp.where(strict, _dot_nt(kt, bib), 0.0)
        a0 = jnp.where(same16, a_bk, 0.0)
        a0_s[b] = a0.astype(BF16)
        a1_s[b] = jnp.where(same32 & jnp.logical_not(same16), a_bk, 0.0).astype(BF16)
        a2_s[b] = jnp.where(same32, 0.0, a_bk).astype(BF16)
        tinv_s[b] = eye - a0
        avk_s[b] = jnp.where(strict, _dot_nt(kt, kib), 0.0).astype(BF16)
        bbk_s[b] = jnp.where(incl, _dot_nt(rt, bib), 0.0).astype(BF16)
        bvk_s[b] = jnp.where(incl, _dot_nt(rt, kib), 0.0).astype(BF16)
    for b in range(bsz):
        pw_s[b] = _dot(a0_s[b], a0_s[b]).astype(BF16)
    for it in range(3):
        for b in range(bsz):
            tinv_s[b] = tinv_s[b] + _dot(tinv_s[b].astype(BF16), pw_s[b])
        if it < 2:
            for b in range(bsz):
                pw_s[b] = _dot(pw_s[b], pw_s[b]).astype(BF16)
    for off_s in (a1_s, a2_s):
        for b in range(bsz):
            pw_s[b] = _dot(tinv_s[b].astype(BF16), off_s[b]).astype(BF16)
        for b in range(bsz):
            tinv_s[b] = tinv_s[b] - _dot(pw_s[b], tinv_s[b].astype(BF16))
    for b in range(bsz):
        kt2 = _dot(tinv_s[b].astype(BF16), kt_s[b])
        kt2_s[b] = kt2.astype(BF16)
        kt2t_s[b] = kt2.T.astype(BF16)
        pw_s[b] = _dot(avk_s[b], vsb_s[b]).astype(BF16)
    for b in range(bsz):
        u0 = _dot(tinv_s[b].astype(BF16), pw_s[b])
        u0_s[b] = u0.astype(BF16)
        u0t_s[b] = u0.T.astype(BF16)
    for b in range(bsz):
        bbk = bbk_s[b]
        y1s = _dot(bvk_s[b], vsb_s[b]) - _dot(bbk, u0_s[b])
        y1 = y1s[0:chunk]
        for h in range(1, RW_HEADS):
            y1 = y1 + y1s[h * chunk:(h + 1) * chunk]
        y1_ref[b] = y1
        rt2_ref[b, 0] = (rt_s[b].astype(F32) - _dot(bbk, kt2_s[b])).astype(BF16)
        m_ref[b, 0] = _dot(kt2t_s[b], bip_s[b]).astype(BF16)
        s1_ref[b, 0] = _dot(vst_s[b], kip_s[b]) - _dot(u0t_s[b], bip_s[b])


def _rwkv_state_kernel(rt2_ref, y1_ref, m_ref, s1_ref, pl_ref, g_ref, bonus_ref, gnw_ref, gnb_ref, ones_ref,
                       o_ref, state_ref, *, bsz, chunk):
    @pl.when(pl.program_id(0) == 0)
    def _():
        state_ref[...] = jnp.zeros_like(state_ref)

    ones = ones_ref[...]
    ys_all, yc_all = [], []
    for b in range(bsz):
        st = state_ref[b]
        stb = st.astype(BF16)
        ys = _dot_nt(rt2_ref[b, 0], stb)
        y = y1_ref[b] + ys[0:chunk]
        for h in range(1, RW_HEADS):
            y = y + ys[h * chunk:(h + 1) * chunk]
        ys_all.append(y)
        state_ref[b] = st * pl_ref[b, 0][0:1] + s1_ref[b, 0] - _dot(stb, m_ref[b, 0])
    for b in range(bsz):
        yc_all.append(ys_all[b] - _segsum(ys_all[b], ones) * (1.0 / RW_DIM))
    for b in range(bsz):
        yc = yc_all[b]
        var = _segsum(yc * yc, ones) * (1.0 / RW_DIM)
        yn = yc * lax.rsqrt(var + RW_GN_EPS) * gnw_ref[...] + gnb_ref[...]
        o_ref[b] = (yn + bonus_ref[b]) * g_ref[b]


def _rwkv_mixer(feat, mu, w0, w_up, a0, a_up, g_up, k_k, k_a, r_k, gn_w, gn_b, bsz, seq):
    w = W_BRANCH
    sl = V7X_SUBLANES
    chunk = min(RW_CHUNK, seq)
    nch = seq // chunk
    n4 = RW_HEADS * chunk
    f3 = feat.reshape(bsz, seq, RW_COLS)
    row1 = lambda a: a.reshape(1, -1)
    ones = _block_diag(jnp.ones((RW_HEADS, RW_DIM, RW_DIM), F32)).astype(BF16)
    full = lambda a: pl.BlockSpec(a.shape, lambda c: (0,) * a.ndim)
    tok = lambda width: pl.BlockSpec((bsz, chunk, width), lambda c: (0, c, 0))
    per_chunk = lambda rows, cols: pl.BlockSpec((bsz, 1, rows, cols), lambda c: (0, c, 0, 0))
    a_args = (row1(mu), row1(w0), w_up.astype(BF16), row1(a0), a_up.astype(BF16), g_up.astype(BF16),
              row1(k_k), row1(k_a), row1(r_k), ones)
    mids = pl.pallas_call(
        functools.partial(_rwkv_chunk_kernel, bsz=bsz, chunk=chunk),
        grid=(nch,),
        in_specs=[tok(RW_COLS),
                  pl.BlockSpec((bsz, sl, RW_COLS), lambda c: (0, jnp.maximum(c * (chunk // sl) - 1, 0), 0))]
        + [full(a) for a in a_args],
        out_specs=[per_chunk(n4, w), tok(w), per_chunk(w, w), per_chunk(w, w), per_chunk(sl, w), tok(w), tok(w)],
        out_shape=[jax.ShapeDtypeStruct((bsz, nch, n4, w), BF16), jax.ShapeDtypeStruct((bsz, seq, w), F32),
                   jax.ShapeDtypeStruct((bsz, nch, w, w), BF16), jax.ShapeDtypeStruct((bsz, nch, w, w), F32),
                   jax.ShapeDtypeStruct((bsz, nch, sl, w), F32), jax.ShapeDtypeStruct((bsz, seq, w), F32),
                   jax.ShapeDtypeStruct((bsz, seq, w), F32)],
        scratch_shapes=[pltpu.VMEM((bsz, n4, w), BF16)] * 19 + [pltpu.VMEM((bsz, n4, w), F32)],
        compiler_params=_cparams("parallel"),
        name="rwkv_chunks",
    )(f3, f3, *a_args)
    b_args = (row1(gn_w), row1(gn_b), ones)
    out = pl.pallas_call(
        functools.partial(_rwkv_state_kernel, bsz=bsz, chunk=chunk),
        grid=(nch,),
        in_specs=[per_chunk(n4, w), tok(w), per_chunk(w, w), per_chunk(w, w), per_chunk(sl, w), tok(w), tok(w)]
        + [full(a) for a in b_args],
        out_specs=tok(w),
        out_shape=jax.ShapeDtypeStruct((bsz, seq, w), F32),
        scratch_shapes=[pltpu.VMEM((bsz, w, w), F32)],
        compiler_params=_cparams("arbitrary"),
        name="rwkv_state",
    )(*mids, *b_args)
    return out.reshape(bsz * seq, w)


DSA_TQ = 256
DSA_KC = 512


DSA_SUB = 128
DSA_VROWS = ATT_DIM + 16
KEY_NEG_INF = INT_MIN + 0x7FFFFF


def _ordered_to_f32(key):
    return lax.bitcast_convert_type(jnp.where(key >= 0, key, key ^ 0x7FFFFFFF), F32)


def _dsa_kernel(*refs, seq, tq, kc, topk):
    def block(i, carry):
        _dsa_block(i, *refs, seq=seq, tq=tq, kc=kc, topk=topk)
        return carry

    lax.fori_loop(0, seq // tq, block, 0)


def _dsa_block(i, qT_ref, k4_ref, vx_ref, iq3_ref, ik3_ref, iwT_ref, o_ref, sc_ref, lga_ref, lgb_ref, bias_ref, p_ref, acc_ref,
               *, seq, tq, kc, topk):
    nchunks = lax.div((i + 1) * tq + (kc - 1), kc)
    qpos = i * tq + lax.broadcasted_iota(I32, (1, tq), 1)
    rows = lax.broadcasted_iota(I32, (kc, 1), 0)
    wT = iwT_ref[0, i]
    iq3 = iq3_ref[0, i]

    def score_body(c, carry):
        s0 = pl.multiple_of(c * kc, kc)
        d = _dot(ik3_ref[pl.ds(s0, kc), :], iq3)
        sc = wT[0:1] * jnp.maximum(d[:, 0:tq], 0.0)
        for h in range(1, IDX_HEADS):
            sc = sc + wT[h:h + 1] * jnp.maximum(d[:, h * tq:(h + 1) * tq], 0.0)
        sc_ref[pl.ds(s0, kc), :] = jnp.where(s0 + rows <= qpos, sc, -jnp.inf)
        return carry

    lax.fori_loop(0, nchunks, score_body, 0)

    def count(ind_fn):
        def body(c, acc):
            s0 = pl.multiple_of(c * kc, kc)
            ind = ind_fn(sc_ref[pl.ds(s0, kc), :], s0 + rows)
            return acc + ind.reshape(kc // V7X_SUBLANES, V7X_SUBLANES, tq).sum(axis=0)
        acc = lax.fori_loop(0, nchunks, body, jnp.zeros((V7X_SUBLANES, tq), I32))
        return jnp.sum(acc, axis=0, keepdims=True)

    c0 = count(lambda s, idx: jnp.where(s >= 0.0, 1, 0))
    ans = jnp.where(c0 >= topk, 0, INT_MIN).astype(I32)
    cnt_ge = jnp.where(c0 >= topk, c0, nchunks * kc)

    def bit_body(j, carry):
        ans, cnt_ge = carry
        cand = ans | lax.shift_left(jnp.int32(1), 30 - j)
        cf = _ordered_to_f32(cand)
        cnt = count(lambda s, idx: jnp.where(s >= cf, 1, 0))
        take = jnp.where(cand <= KEY_NEG_INF, 1, jnp.where(cnt >= topk, 1, 0))
        keep_cnt = jnp.where(cand <= KEY_NEG_INF, nchunks * kc, cnt)
        return jnp.where(take > 0, cand, ans), jnp.where(take > 0, keep_cnt, cnt_ge)

    ans, cnt_ge = lax.fori_loop(0, 31, bit_body, (ans, cnt_ge))
    thr = _ordered_to_f32(ans)
    cnt_gt = count(lambda s, idx: jnp.where(s > thr, 1, 0))
    need = topk - cnt_gt
    tie = jnp.where(cnt_ge > topk, jnp.where(thr > -jnp.inf, 1, 0), 0)
    nbits = max(1, (seq - 1).bit_length())

    def tie_search():
        def jb(j, jans):
            cand = jans | lax.shift_left(jnp.int32(1), nbits - 1 - j)
            f = count(lambda s, idx: jnp.where(s == thr, jnp.where(idx < cand, 1, 0), 0))
            return jnp.where(f < need, cand, jans)
        return lax.fori_loop(0, nbits, jb, jnp.zeros((1, tq), I32))

    jtie = lax.cond(jnp.max(tie) > 0, tie_search, lambda: jnp.zeros((1, tq), I32))
    jsel = jnp.where(thr == -jnp.inf, qpos, jnp.where(tie > 0, jtie, seq))

    q_all = qT_ref[0, i]
    hd = ATT_DIM

    def logits_into(buf_ref, c, live):
        s0 = pl.multiple_of(c * kc, kc)
        sch = sc_ref[pl.ds(s0, kc), :]
        dead = jnp.where(live, 0.0, NEG_BIG)
        bias_ref[...] = jnp.where(sch > thr, dead,
                                  jnp.where(sch == thr, jnp.where(s0 + rows <= jsel, dead, NEG_BIG), NEG_BIG))
        for h in range(ATT_HEADS):
            buf_ref[h] = _dot(k4_ref[0, h, pl.ds(s0, kc), :], q_all[h * hd:(h + 1) * hd]) + bias_ref[...]

    def softmax_step(buf_ref, c, state):
        for h in range(ATT_HEADS):
            m = state[h]
            part = buf_ref[h].reshape(kc // V7X_SUBLANES, V7X_SUBLANES, tq).max(axis=0)
            mn = jnp.maximum(m, jnp.max(part, axis=0, keepdims=True))
            state[h] = mn
            p_ref[h] = jnp.exp2(buf_ref[h] - mn).astype(BF16)
            acc_ref[h] = jnp.exp2(m - mn) * acc_ref[h] + _dot(vx_ref[0, c, h], p_ref[h])

    last = nchunks - 1
    acc_ref[...] = jnp.zeros_like(acc_ref)
    logits_into(lga_ref, 0, True)

    def att_body(t, carry):
        state = list(carry)
        c0 = 2 * t
        c1 = jnp.minimum(c0 + 1, last)
        logits_into(lgb_ref, c1, c0 + 1 <= last)
        softmax_step(lga_ref, c0, state)
        logits_into(lga_ref, jnp.minimum(c0 + 2, last), True)
        softmax_step(lgb_ref, c1, state)
        return tuple(state)

    init = (jnp.full((1, tq), NEG_BIG, F32),) * ATT_HEADS
    lax.fori_loop(0, lax.div(nchunks + 1, 2), att_body, init)
    for h in range(ATT_HEADS):
        acc = acc_ref[h]
        o_ref[0, i, h * hd:(h + 1) * hd, :] = acc[0:hd] / acc[hd:hd + 1]


def _dsa_mixer(q, k, v, iqh, iql, ikh, ikl, iw, bsz, seq):
    w = W_BRANCH
    tq = min(DSA_TQ, seq)
    kc = min(DSA_KC, seq)
    nq = seq // tq
    nc = seq // kc
    topk = min(TOPK_MAX, seq // 4)
    assert kc >= topk and kc % DSA_SUB == 0
    qT = q.reshape(bsz, nq, tq, w).transpose(0, 1, 3, 2)
    k4 = k.reshape(bsz, seq, ATT_HEADS, ATT_DIM).transpose(0, 2, 1, 3)
    vT = v.reshape(bsz, nc, kc, ATT_HEADS, ATT_DIM).transpose(0, 1, 3, 4, 2)
    vx = jnp.concatenate([vT, jnp.ones((bsz, nc, ATT_HEADS, DSA_VROWS - ATT_DIM, kc), BF16)], axis=3)
    ik3 = jnp.concatenate([ikh, ikh, ikl], axis=1)
    iq3 = jnp.stack([iqh, iql, iqh], axis=1).reshape(bsz, nq, tq, 3, IDX_HEADS, IDX_DIM)
    iq3 = iq3.transpose(0, 1, 3, 5, 4, 2).reshape(bsz, nq, 3 * IDX_DIM, IDX_HEADS * tq)
    iwT = iw.reshape(bsz, nq, tq, IDX_HEADS).transpose(0, 1, 3, 2)
    seq_blk = lambda a: pl.BlockSpec((1,) + a.shape[1:], lambda b: (b,) + (0,) * (a.ndim - 1))
    out = pl.pallas_call(
        functools.partial(_dsa_kernel, seq=seq, tq=tq, kc=kc, topk=topk),
        grid=(bsz,),
        in_specs=[seq_blk(qT), seq_blk(k4), seq_blk(vx), seq_blk(iq3),
                  pl.BlockSpec((seq, 3 * IDX_DIM), lambda b: (b, 0)), seq_blk(iwT)],
        out_specs=pl.BlockSpec((1, nq, w, tq), lambda b: (b, 0, 0, 0)),
        out_shape=jax.ShapeDtypeStruct((bsz, nq, w, tq), F32),
        scratch_shapes=[pltpu.VMEM((seq, tq), F32), pltpu.VMEM((ATT_HEADS, kc, tq), F32),
                        pltpu.VMEM((ATT_HEADS, kc, tq), F32), pltpu.VMEM((kc, tq), F32),
                        pltpu.VMEM((ATT_HEADS, kc, tq), BF16), pltpu.VMEM((ATT_HEADS, DSA_VROWS, tq), F32)],
        compiler_params=_cparams("parallel"),
        name="dsa_mixer",
    )(qT, k4, vx, iq3, ik3, iwT)
    return out.transpose(0, 1, 3, 2).reshape(bsz * seq, w)


def _merge_kernel(x_ref, mod_ref, g_ref, o0_ref, o1_ref, o2_ref, o3_ref, gw_ref, gb_ref, bw_ref, ow_ref, x1_ref):
    d = x_ref.shape[1]
    mod = mod_ref[0]
    x = x_ref[...]
    hb = _modulated_norm(x, g_ref[...], mod[1:2], mod[0:1]).astype(BF16)
    mixed = jnp.zeros(x.shape, F32)
    for n, o_ref in enumerate((o0_ref, o1_ref, o2_ref, o3_ref)):
        gate = _sigmoid(_dot(hb, gw_ref[:, n * d:(n + 1) * d]) + gb_ref[:, n * d:(n + 1) * d])
        mixed = mixed + gate * _dot(o_ref[...].astype(BF16), bw_ref[n])
    x1_ref[...] = x + mod[2:3] * _dot(mixed.astype(BF16), ow_ref[...])


def _merge(x2, mod_l, g, branches, gate_w, gate_b, branch_w, out_w, seq):
    n, d = x2.shape
    w = W_BRANCH
    tm = min(512, seq)
    tpb = seq // tm
    row = lambda width: pl.BlockSpec((tm, width), lambda i: (i, 0))
    full = lambda a: pl.BlockSpec(a.shape, lambda i: (0,) * a.ndim)
    args = (gate_w.astype(BF16), gate_b.reshape(1, -1), branch_w.astype(BF16), out_w.astype(BF16))
    return pl.pallas_call(
        _merge_kernel,
        grid=(n // tm,),
        in_specs=[row(d), pl.BlockSpec((1, 6, d), lambda i: (i // tpb, 0, 0)), full(g)]
        + [row(w)] * N_BRANCH + [full(a) for a in args],
        out_specs=row(d),
        out_shape=jax.ShapeDtypeStruct((n, d), F32),
        compiler_params=_cparams("parallel"),
        name="merge",
    )(x2, mod_l, g, *branches, *args)


def _route_combine(scores, rb):
    biased = scores + rb
    col = lambda a, e: a[:, e:e + 1]
    npg = EXP_PER_GROUP
    gs = []
    for g in range(N_GROUPS):
        best = None
        for j1 in range(npg):
            for j2 in range(j1 + 1, npg):
                s = col(biased, npg * g + j1) + col(biased, npg * g + j2)
                best = s if best is None else jnp.maximum(best, s)
        gs.append(best)
    bg = jnp.zeros(gs[0].shape, I32)
    bv = gs[0]
    for g in range(1, N_GROUPS):
        upd = gs[g] > bv
        bv = jnp.where(upd, gs[g], bv)
        bg = jnp.where(upd, g, bg)
    bsel, ssel = [], []
    for j in range(npg):
        bj, sj = col(biased, j), col(scores, j)
        for g in range(1, N_GROUPS):
            bj = jnp.where(bg == g, col(biased, npg * g + j), bj)
            sj = jnp.where(bg == g, col(scores, npg * g + j), sj)
        bsel.append(bj)
        ssel.append(sj)
    i1, v1, s1 = jnp.zeros(bg.shape, I32), bsel[0], ssel[0]
    for j in range(1, npg):
        upd = bsel[j] > v1
        v1 = jnp.where(upd, bsel[j], v1)
        s1 = jnp.where(upd, ssel[j], s1)
        i1 = jnp.where(upd, j, i1)
    i2, v2, s2 = jnp.zeros(bg.shape, I32), jnp.where(i1 == 0, -jnp.inf, bsel[0]), ssel[0]
    for j in range(1, npg):
        cand = jnp.where(i1 == j, -jnp.inf, bsel[j])
        upd = cand > v2
        v2 = jnp.where(upd, cand, v2)
        s2 = jnp.where(upd, ssel[j], s2)
        i2 = jnp.where(upd, j, i2)
    tot = s1 + s2
    lane = lax.broadcasted_iota(I32, scores.shape, 1)
    return (jnp.where(lane == npg * bg + i1, s1 / tot, 0.0)
            + jnp.where(lane == npg * bg + i2, s2 / tot, 0.0))


def _route_kernel(x1_ref, mod_ref, g_ref, rwh_ref, rwl_ref, rb_ref, h2_ref, comb_ref):
    mod = mod_ref[0]
    h2 = _modulated_norm(x1_ref[...], g_ref[...], mod[4:5], mod[3:4])
    h2_ref[...] = h2.astype(BF16)
    hh, hl = _split_bf16(h2)
    logits = _dot(hh, rwh_ref[...]) + _dot(hh, rwl_ref[...]) + _dot(hl, rwh_ref[...])
    comb_ref[...] = _route_combine(_sigmoid(logits[:, 0:N_EXPERTS]), rb_ref[...])


def _route(x1, mod_l, g, router_w, router_b, seq):
    n, d = x1.shape
    tm = min(512, seq)
    tpb = seq // tm
    rwh, rwl = _split_bf16(jnp.pad(router_w, ((0, 0), (0, V7X_LANES - N_EXPERTS))))
    row = lambda width: pl.BlockSpec((tm, width), lambda i: (i, 0))
    full = lambda a: pl.BlockSpec(a.shape, lambda i: (0,) * a.ndim)
    rb = router_b.reshape(1, N_EXPERTS)
    return pl.pallas_call(
        _route_kernel,
        grid=(n // tm,),
        in_specs=[row(d), pl.BlockSpec((1, 6, d), lambda i: (i // tpb, 0, 0)), full(g), full(rwh), full(rwl), full(rb)],
        out_specs=[row(d), row(N_EXPERTS)],
        out_shape=[jax.ShapeDtypeStruct((n, d), BF16), jax.ShapeDtypeStruct((n, N_EXPERTS), F32)],
        compiler_params=_cparams("parallel"),
        name="route",
    )(x1, mod_l, g, rwh, rwl, rb)


def _moe_kernel(h2_ref, c_ref, w1_ref, w3_ref, w2_ref, x1_ref, mod_ref, fg_ref, o_ref, acc_ref, *, final_norm):
    e = pl.program_id(1)

    @pl.when(e == 0)
    def _():
        acc_ref[...] = jnp.zeros_like(acc_ref)

    h2 = h2_ref[...]
    a = _dot(h2, w1_ref[0, 0].astype(BF16))
    b = _dot(h2, w3_ref[0, 0].astype(BF16))
    y = _dot((a * _sigmoid(a) * b).astype(BF16), w2_ref[0, 0].astype(BF16))
    acc_ref[...] += c_ref[0] * y

    @pl.when(e == pl.num_programs(1) - 1)
    def _():
        out = x1_ref[...] + mod_ref[0][5:6] * acc_ref[...]
        if final_norm:
            ms = jnp.mean(out * out, axis=-1, keepdims=True)
            out = out * lax.rsqrt(ms + NORM_EPS) * fg_ref[...]
        o_ref[...] = out


def _moe_dense(h2, comb, exp_w1, exp_w3, exp_w2, layer, x1, mod_l, final_g, seq, final_norm):
    n, d = x1.shape
    ne, f = exp_w1.shape[1], exp_w1.shape[3]
    tm = min(1024, seq)
    tpb = seq // tm
    c3 = comb.T.reshape(ne, n, 1)
    row = lambda width: pl.BlockSpec((tm, width), lambda i, e: (i, 0))
    return pl.pallas_call(
        functools.partial(_moe_kernel, final_norm=final_norm),
        grid=(n // tm, ne),
        in_specs=[row(d), pl.BlockSpec((1, tm, 1), lambda i, e: (e, i, 0)),
                  pl.BlockSpec((1, 1, d, f), lambda i, e: (layer, e, 0, 0)),
                  pl.BlockSpec((1, 1, d, f), lambda i, e: (layer, e, 0, 0)),
                  pl.BlockSpec((1, 1, f, d), lambda i, e: (layer, e, 0, 0)),
                  row(d), pl.BlockSpec((1, 6, d), lambda i, e: (i // tpb, 0, 0)),
                  pl.BlockSpec((1, d), lambda i, e: (0, 0))],
        out_specs=row(d),
        out_shape=jax.ShapeDtypeStruct((n, d), F32),
        scratch_shapes=[pltpu.VMEM((tm, d), F32)],
        compiler_params=_cparams("parallel", "arbitrary"),
        name="moe_dense",
    )(h2, c3, exp_w1, exp_w3, exp_w2, x1, mod_l, final_g)


def kernel(x, c, positions, ada_w, ada_b, mix_norm_g, w_in, gate_w, gate_b, branch_w, out_w, s5_lam_re, s5_lam_im, s5_log_step, s5_b_re, s5_b_im, s5_c_re, s5_c_im, s5_d, s5_glu_w, s5_glu_b, rg_conv_w, rg_conv_b, rg_wr, rg_br, rg_wi, rg_bi, rg_lam, rw_mu, rw_w0, rw_w_up, rw_a0, rw_a_up, rw_g_up, rw_k_k, rw_k_a, rw_r_k, rw_gn_w, rw_gn_b, ffn_norm_g, router_w, router_b, exp_w1, exp_w3, exp_w2, final_norm_g):
    bsz, seq, d = x.shape
    depth = ada_w.shape[0]
    n = bsz * seq
    x2 = x.reshape(n, d)
    mod = _ada_mod(c, ada_w, ada_b)
    tabs = _rope_tables(positions)
    fg = final_norm_g.reshape(1, d)
    for l in range(depth):
        mod_l = mod[l]
        g1 = mix_norm_g[l].reshape(1, d)
        (u, q, k, v, iqh, iql, ikh, ikl, iw, rgx, rgg, rwf) = _inproj(x2, mod_l, g1, w_in[l], tabs, seq)
        s5p = _s5_params(s5_lam_re[l], s5_lam_im[l], s5_log_step[l], s5_b_re[l], s5_b_im[l], s5_c_re[l], s5_c_im[l])
        o_s5 = _s5_mixer(u, s5p, s5_d[l], s5_glu_w[l], s5_glu_b[l], bsz, seq)
        o_dsa = _dsa_mixer(q, k, v, iqh, iql, ikh, ikl, iw, bsz, seq)
        o_rg = _rglru_mixer(rgx, rgg, rg_conv_w[l], rg_conv_b[l], rg_wr[l], rg_br[l], rg_wi[l], rg_bi[l],
                            rg_lam[l], bsz, seq)
        o_rw = _rwkv_mixer(rwf, rw_mu[l], rw_w0[l], rw_w_up[l], rw_a0[l], rw_a_up[l], rw_g_up[l], rw_k_k[l],
                           rw_k_a[l], rw_r_k[l], rw_gn_w[l], rw_gn_b[l], bsz, seq)
        x1 = _merge(x2, mod_l, g1, (o_s5, o_dsa, o_rg, o_rw), gate_w[l], gate_b[l], branch_w[l], out_w[l], seq)
        h2, comb = _route(x1, mod_l, ffn_norm_g[l].reshape(1, d), router_w, router_b, seq)
        x2 = _moe_dense(h2, comb, exp_w1, exp_w3, exp_w2, l, x1, mod_l, fg, seq, l == depth - 1)
    return x2.reshape(bsz, seq, d)
```

```python
import functools
import math

import jax
import jax.numpy as jnp
from jax import lax
from jax.experimental import pallas as pl
from jax.experimental.pallas import tpu as pltpu

F32 = jnp.float32
BF16 = jnp.bfloat16
I32 = jnp.int32

W_BRANCH = 256
N_BRANCH = 4
NORM_EPS = 1e-6
S5_GROUP = 16
S5_GROUPS = W_BRANCH // S5_GROUP
S5_STATE = 64
ATT_HEADS = 4
ATT_DIM = W_BRANCH // ATT_HEADS
IDX_HEADS = 4
IDX_DIM = 32
TOPK_MAX = 256
ROPE_THETA = 10000.0
RG_BLOCKS = 4
RG_C = 8.0
CONV_WIDTH = 4
RW_HEADS = 4
RW_DIM = W_BRANCH // RW_HEADS
RW_DECAY_LORA = 32
RW_A_LORA = 32
RW_GATE_LORA = 64
RW_GN_EPS = 64e-5
RW_COLS = 3 * W_BRANCH + RW_DECAY_LORA + RW_A_LORA + RW_GATE_LORA
N_EXPERTS = 16
N_GROUPS = 4
EXP_PER_GROUP = N_EXPERTS // N_GROUPS
D_EXPERT = 512

V7X_SUBLANES = 8
V7X_LANES = 128
V7X_VMEM_LIMIT_BYTES = 56 * 1024 * 1024

INT_MIN = -(2 ** 31)
NEG_BIG = -1e30


def _cparams(*sem, flags=None):
    return pltpu.CompilerParams(dimension_semantics=sem, vmem_limit_bytes=V7X_VMEM_LIMIT_BYTES, flags=flags)


def _split_bf16(x):
    hi = x.astype(BF16)
    lo = (x - hi.astype(F32)).astype(BF16)
    return hi, lo


def _dot(a, b):
    return jnp.dot(a, b, preferred_element_type=F32)


def _dot_nt(a, b):
    return lax.dot_general(a, b, (((1,), (1,)), ((), ())), preferred_element_type=F32)


def _dot_tn(a, b):
    return lax.dot_general(a, b, (((0,), (0,)), ((), ())), preferred_element_type=F32)


def _dot3(a, b):
    ah, al = _split_bf16(a)
    bh, bl = _split_bf16(b)
    return _dot(ah, bh) + _dot(ah, bl) + _dot(al, bh)


def _sigmoid(x):
    return 1.0 / (1.0 + jnp.exp(-x))


def _softplus(x):
    return jnp.maximum(x, 0.0) + jnp.log(1.0 + jnp.exp(-jnp.abs(x)))


def _gelu(x):
    c = math.sqrt(2.0 / math.pi)
    return 0.5 * x * (1.0 + jnp.tanh(c * (x + 0.044715 * (x * x * x))))


def _rot_half_cols(w, n_heads, dim):
    lead = w.shape[0]
    w4 = w.reshape(lead, n_heads, 2, dim // 2)
    return jnp.concatenate([-w4[:, :, 1:2], w4[:, :, 0:1]], axis=2).reshape(lead, n_heads * dim)


def _ada_kernel(c_ref, w_ref, b_ref, o_ref):
    c = c_ref[...]
    ca = c * _sigmoid(c)
    o_ref[0] = _dot3(ca, w_ref[0]) + b_ref[0]


def _ada_mod(c, ada_w, ada_b):
    depth, d, d6 = ada_w.shape
    bsz = c.shape[0]
    rows = -(-bsz // V7X_SUBLANES) * V7X_SUBLANES
    cp = jnp.zeros((rows, d), F32).at[:bsz].set(c)
    out = pl.pallas_call(
        _ada_kernel,
        grid=(depth, d6 // d),
        in_specs=[pl.BlockSpec((rows, d), lambda l, j: (0, 0)),
                  pl.BlockSpec((1, d, d), lambda l, j: (l, 0, j)),
                  pl.BlockSpec((1, 1, d), lambda l, j: (l, 0, j))],
        out_specs=pl.BlockSpec((1, rows, d), lambda l, j: (l, 0, j)),
        out_shape=jax.ShapeDtypeStruct((depth, rows, d6), F32),
        compiler_params=_cparams("parallel", "parallel"),
        name="ada_mod",
    )(cp, ada_w, ada_b.reshape(depth, 1, d6))
    return out[:, :bsz].reshape(depth, bsz, 6, d)


def _rope_kernel(pos_ref, fa_ref, fi_ref, ca_ref, sa_ref, ci_ref, si_ref):
    pos = pos_ref[...]
    ang_a = pos * fa_ref[...]
    ang_i = pos * fi_ref[...]
    ca_ref[...] = jnp.cos(ang_a)
    sa_ref[...] = jnp.sin(ang_a)
    ci_ref[...] = jnp.cos(ang_i)
    si_ref[...] = jnp.sin(ang_i)


def _rope_tables(positions):
    n = positions.size
    pos = positions.reshape(n, 1).astype(F32)

    def freq_row(dim):
        half = dim // 2
        inv = ROPE_THETA ** (-jnp.arange(half, dtype=F32) / half)
        return jnp.tile(inv, V7X_LANES // half).reshape(1, V7X_LANES)

    tm = min(n, 2048)
    row = pl.BlockSpec((tm, V7X_LANES), lambda i: (i, 0))
    frq = pl.BlockSpec((1, V7X_LANES), lambda i: (0, 0))
    shp = jax.ShapeDtypeStruct((n, V7X_LANES), F32)
    return pl.pallas_call(
        _rope_kernel,
        grid=(n // tm,),
        in_specs=[pl.BlockSpec((tm, 1), lambda i: (i, 0)), frq, frq],
        out_specs=[row, row, row, row],
        out_shape=[shp, shp, shp, shp],
        compiler_params=_cparams("parallel"),
        name="rope_tables",
    )(pos, freq_row(ATT_DIM), freq_row(IDX_DIM))


def _modulated_norm(x, g, scale, shift):
    ms = jnp.mean(x * x, axis=-1, keepdims=True)
    return (x * lax.rsqrt(ms + NORM_EPS)) * g * (1.0 + scale) + shift


def _inproj_kernel(x_ref, mod_ref, g_ref, wm_ref, wih_ref, wil_ref,
                   ca_ref, sa_ref, ci_ref, si_ref,
                   u_ref, q_ref, k_ref, v_ref, iqh_ref, iql_ref, ikh_ref, ikl_ref, iw_ref,
                   rgx_ref, rgg_ref, rw_ref):
    w = W_BRANCH
    mod = mod_ref[0]
    h = _modulated_norm(x_ref[...], g_ref[...], mod[1:2], mod[0:1])
    hh, hl = _split_bf16(h)
    main = _dot(hh, wm_ref[...])
    idx = _dot(hh, wih_ref[...]) + _dot(hh, wil_ref[...]) + _dot(hl, wih_ref[...])
    ca = jnp.concatenate([ca_ref[...], ca_ref[...]], axis=1)
    sa = jnp.concatenate([sa_ref[...], sa_ref[...]], axis=1)
    u_ref[...] = main[:, 0:w]
    q_ref[...] = (main[:, w:2 * w] * ca + main[:, 2 * w:3 * w] * sa).astype(BF16)
    k_ref[...] = (main[:, 3 * w:4 * w] * ca + main[:, 4 * w:5 * w] * sa).astype(BF16)
    v_ref[...] = main[:, 5 * w:6 * w].astype(BF16)
    rgx_ref[...] = main[:, 6 * w:7 * w]
    rgg_ref[...] = main[:, 7 * w:8 * w]
    rw_ref[...] = main[:, 8 * w:8 * w + RW_COLS]
    ci = ci_ref[...]
    si = si_ref[...]
    iq = idx[:, 0:128] * ci + idx[:, 128:256] * si
    ik = idx[:, 256:288] * ci[:, 0:IDX_DIM] + idx[:, 288:320] * si[:, 0:IDX_DIM]
    iqh, iql = _split_bf16(iq)
    ikh, ikl = _split_bf16(ik)
    iqh_ref[...] = iqh
    iql_ref[...] = iql
    ikh_ref[...] = ikh
    ikl_ref[...] = ikl
    iw_ref[...] = idx[:, 320:320 + IDX_HEADS]


def _inproj_weights(w_in):
    w = W_BRANCH
    o = 0
    parts = {}
    for name, width in (("u", w), ("q", w), ("k", w), ("v", w), ("iq", IDX_HEADS * IDX_DIM),
                        ("ik", IDX_DIM), ("iw", IDX_HEADS), ("rgx", w), ("rgg", w), ("rw", RW_COLS)):
        parts[name] = w_in[:, o:o + width]
        o += width
    wq = parts["q"] * (ATT_DIM ** -0.5 * math.log2(math.e))
    main = jnp.concatenate([parts["u"], wq, _rot_half_cols(wq, ATT_HEADS, ATT_DIM),
                            parts["k"], _rot_half_cols(parts["k"], ATT_HEADS, ATT_DIM),
                            parts["v"], parts["rgx"], parts["rgg"], parts["rw"]], axis=1)
    pad = (-main.shape[1]) % V7X_LANES
    main = jnp.pad(main, ((0, 0), (0, pad)))
    idx = jnp.concatenate([parts["iq"], _rot_half_cols(parts["iq"], IDX_HEADS, IDX_DIM),
                           parts["ik"], _rot_half_cols(parts["ik"], 1, IDX_DIM), parts["iw"]], axis=1)
    idx = jnp.pad(idx, ((0, 0), (0, 384 - idx.shape[1])))
    ih, il = _split_bf16(idx)
    return main.astype(BF16), ih, il


def _inproj(x2, mod_l, g, w_in, tabs, seq):
    n, d = x2.shape
    tm = min(512, seq)
    tpb = seq // tm
    wm, wih, wil = _inproj_weights(w_in)
    w = W_BRANCH
    row = lambda width: pl.BlockSpec((tm, width), lambda i: (i, 0))
    full = lambda a: pl.BlockSpec(a.shape, lambda i: (0,) * a.ndim)
    shp = lambda width, dt: jax.ShapeDtypeStruct((n, width), dt)
    tab = row(V7X_LANES)
    return pl.pallas_call(
        _inproj_kernel,
        grid=(n // tm,),
        in_specs=[row(d), pl.BlockSpec((1, 6, d), lambda i: (i // tpb, 0, 0)), full(g),
                  full(wm), full(wih), full(wil), tab, tab, tab, tab],
        out_specs=[row(w), row(w), row(w), row(w), row(128), row(128), row(IDX_DIM), row(IDX_DIM),
                   row(IDX_HEADS), row(w), row(w), row(RW_COLS)],
        out_shape=[shp(w, F32), shp(w, BF16), shp(w, BF16), shp(w, BF16), shp(128, BF16), shp(128, BF16),
                   shp(IDX_DIM, BF16), shp(IDX_DIM, BF16), shp(IDX_HEADS, F32),
                   shp(w, F32), shp(w, F32), shp(RW_COLS, F32)],
        compiler_params=_cparams("parallel"),
        name="inproj",
    )(x2, mod_l, g, wm, wih, wil, *tabs)


def _s5_prep_kernel(lre_r, lim_r, st_r, lre_c, lim_c, st_c, bre_ref, bim_ref,
                    pre_ref, pim_ref, bbre_ref, bbim_ref):
    kk = (lax.broadcasted_iota(I32, (V7X_SUBLANES, 1), 0) + 1).astype(F32)
    step = jnp.exp(st_r[...])
    mag = jnp.exp(kk * (lre_r[...] * step))
    ang = kk * (lim_r[...] * step)
    pre_ref[...] = mag * jnp.cos(ang)
    pim_ref[...] = mag * jnp.sin(ang)
    a = lre_c[...]
    b = lim_c[...]
    stc = jnp.exp(st_c[...])
    m1 = jnp.exp(a * stc)
    nr = m1 * jnp.cos(b * stc) - 1.0
    ni = m1 * jnp.sin(b * stc)
    den = a * a + b * b
    cr = (nr * a + ni * b) / den
    ci = (ni * a - nr * b) / den
    bre = bre_ref[...]
    bim = bim_ref[...]
    bbre_ref[...] = cr * bre - ci * bim
    bbim_ref[...] = cr * bim + ci * bre


def _block_diag(blocks):
    g, r, c = blocks.shape
    eye = jnp.eye(g, dtype=blocks.dtype)
    return (blocks[:, :, None, :] * eye[:, None, :, None]).reshape(g * r, g * c)


def _s5_params(lam_re, lam_im, log_step, b_re, b_im, c_re, c_im):
    g, p, c = b_re.shape
    ns = g * p
    st = jnp.repeat(log_step, p)
    rows = [a.reshape(1, ns) for a in (lam_re, lam_im, st)]
    cols = [a.reshape(ns, 1) for a in (lam_re, lam_im, st)]
    full = lambda a: pl.BlockSpec(a.shape, lambda: (0,) * a.ndim)
    args = rows + cols + [b_re.reshape(ns, c), b_im.reshape(ns, c)]
    pre, pim, bbre, bbim = pl.pallas_call(
        _s5_prep_kernel,
        in_specs=[full(a) for a in args],
        out_specs=[pl.BlockSpec((V7X_SUBLANES, ns), lambda: (0, 0))] * 2
        + [pl.BlockSpec((ns, c), lambda: (0, 0))] * 2,
        out_shape=[jax.ShapeDtypeStruct((V7X_SUBLANES, ns), F32)] * 2
        + [jax.ShapeDtypeStruct((ns, c), F32)] * 2,
        name="s5_prep",
    )(*args)
    bmat = jnp.concatenate([_block_diag(bbre.reshape(g, p, c).transpose(0, 2, 1)),
                            _block_diag(bbim.reshape(g, p, c).transpose(0, 2, 1))], axis=1)
    cmat = jnp.concatenate([_block_diag(c_re.transpose(0, 2, 1)),
                            -_block_diag(c_im.transpose(0, 2, 1))], axis=0)
    return pre, pim, bmat.astype(BF16), cmat.astype(BF16)


def _s5_kernel(u_ref, bmat_ref, cmat_ref, pre_ref, pim_ref, d_ref, gw_ref, gb_ref,
               o_ref, h_ref, carry_ref, *, chunk, ns):
    @pl.when(pl.program_id(1) == 0)
    def _():
        carry_ref[...] = jnp.zeros_like(carry_ref)

    u = u_ref[...]
    bu = _dot(u.astype(BF16), bmat_ref[...])
    nt = chunk // V7X_SUBLANES
    xre = bu[:, :ns].reshape(nt, V7X_SUBLANES, ns)
    xim = bu[:, ns:].reshape(nt, V7X_SUBLANES, ns)
    pre = pre_ref[...]
    pim = pim_ref[...]
    rowi = lax.broadcasted_iota(I32, (V7X_SUBLANES, 1), 0)
    for s in (1, 2, 4):
        cre = jnp.where(rowi >= s, pre[s - 1:s], 0.0)[None]
        cim = jnp.where(rowi >= s, pim[s - 1:s], 0.0)[None]
        sre = pltpu.roll(xre, s, 1)
        sim = pltpu.roll(xim, s, 1)
        xre, xim = xre + (cre * sre - cim * sim), xim + (cre * sim + cim * sre)
    cr = carry_ref[0:1, :]
    ci = carry_ref[1:2, :]
    for t in range(nt):
        hre = xre[t] + (pre * cr - pim * ci)
        him = xim[t] + (pre * ci + pim * cr)
        h_ref[t * V7X_SUBLANES:(t + 1) * V7X_SUBLANES, 0:ns] = hre
        h_ref[t * V7X_SUBLANES:(t + 1) * V7X_SUBLANES, ns:2 * ns] = him
        cr = hre[V7X_SUBLANES - 1:V7X_SUBLANES]
        ci = him[V7X_SUBLANES - 1:V7X_SUBLANES]
    carry_ref[0:1, :] = cr
    carry_ref[1:2, :] = ci
    y = _dot(h_ref[...].astype(BF16), cmat_ref[...]) + d_ref[...] * u
    y = _gelu(y)
    z = _dot(y.astype(BF16), gw_ref[...]) + gb_ref[...]
    o_ref[...] = y * _sigmoid(z)


def _s5_mixer(u, params, d_skip, glu_w, glu_b, bsz, seq):
    pre, pim, bmat, cmat = params
    n, w = u.shape
    ns = pre.shape[1]
    chunk = min(256, seq)
    cpb = seq // chunk
    full = lambda a: pl.BlockSpec(a.shape, lambda b, c: (0,) * a.ndim)
    row = pl.BlockSpec((chunk, w), lambda b, c: (b * cpb + c, 0))
    args = (u, bmat, cmat, pre, pim, d_skip.reshape(1, w), glu_w.astype(BF16), glu_b.reshape(1, w))
    return pl.pallas_call(
        functools.partial(_s5_kernel, chunk=chunk, ns=ns),
        grid=(bsz, cpb),
        in_specs=[row] + [full(a) for a in args[1:]],
        out_specs=row,
        out_shape=jax.ShapeDtypeStruct((n, w), F32),
        scratch_shapes=[pltpu.VMEM((chunk, 2 * ns), F32), pltpu.VMEM((V7X_SUBLANES, ns), F32)],
        compiler_params=_cparams("parallel", "arbitrary"),
        name="s5_mixer",
    )(*args)


def _rglru_kernel(x_ref, gate_ref, cw_ref, cb_ref, wr_ref, br_ref, wi_ref, bi_ref, lam_ref,
                  o_ref, tail_ref, carry_ref, *, chunk):
    w = W_BRANCH

    @pl.when(pl.program_id(1) == 0)
    def _():
        tail_ref[...] = jnp.zeros_like(tail_ref)
        carry_ref[...] = jnp.zeros_like(carry_ref)

    nt = chunk // V7X_SUBLANES
    x = x_ref[...]
    tail = tail_ref[...]
    rowc = lax.broadcasted_iota(I32, (chunk, 1), 0)
    cw = cw_ref[...]
    xc = cw[CONV_WIDTH - 1:CONV_WIDTH] * x + cb_ref[...]
    for sh in range(1, CONV_WIDTH):
        prev = jnp.broadcast_to(pltpu.roll(tail, sh, 0)[None], (nt, V7X_SUBLANES, w)).reshape(chunk, w)
        xs = jnp.where(rowc < sh, prev, pltpu.roll(x, sh, 0))
        xc = xc + cw[CONV_WIDTH - 1 - sh:CONV_WIDTH - sh] * xs
    tail_ref[...] = x[chunk - V7X_SUBLANES:chunk]
    xb = xc.astype(BF16)
    r = _sigmoid(_dot(xb, wr_ref[...]) + br_ref[...])
    i = _sigmoid(_dot(xb, wi_ref[...]) + bi_ref[...])
    log_a = (-RG_C) * r * _softplus(-lam_ref[...])
    a = jnp.exp(log_a)
    mult = jnp.sqrt(1.0 - jnp.exp(2.0 * log_a))
    b = mult * i * xc
    a3 = a.reshape(nt, V7X_SUBLANES, w)
    b3 = b.reshape(nt, V7X_SUBLANES, w)
    rowi = lax.broadcasted_iota(I32, (1, V7X_SUBLANES, 1), 1)
    for s in (1, 2, 4):
        keep = rowi >= s
        ash = jnp.where(keep, pltpu.roll(a3, s, 1), 1.0)
        bsh = jnp.where(keep, pltpu.roll(b3, s, 1), 0.0)
        b3 = b3 + a3 * bsh
        a3 = a3 * ash
    h = carry_ref[0:1, :]
    gate = _gelu(gate_ref[...])
    for t in range(nt):
        ht = b3[t] + a3[t] * h
        o_ref[t * V7X_SUBLANES:(t + 1) * V7X_SUBLANES, :] = ht * gate[t * V7X_SUBLANES:(t + 1) * V7X_SUBLANES]
        h = ht[V7X_SUBLANES - 1:V7X_SUBLANES]
    carry_ref[0:1, :] = h


def _rglru_mixer(xr, gate, conv_w, conv_b, w_r, b_r, w_i, b_i, lam, bsz, seq):
    n, w = xr.shape
    chunk = min(256, seq)
    cpb = seq // chunk
    full = lambda a: pl.BlockSpec(a.shape, lambda b, c: (0,) * a.ndim)
    row = pl.BlockSpec((chunk, w), lambda b, c: (b * cpb + c, 0))
    args = (xr, gate, conv_w, conv_b.reshape(1, w), _block_diag(w_r).astype(BF16), b_r.reshape(1, w),
            _block_diag(w_i).astype(BF16), b_i.reshape(1, w), lam.reshape(1, w))
    return pl.pallas_call(
        functools.partial(_rglru_kernel, chunk=chunk),
        grid=(bsz, cpb),
        in_specs=[row, row] + [full(a) for a in args[2:]],
        out_specs=row,
        out_shape=jax.ShapeDtypeStruct((n, w), F32),
        scratch_shapes=[pltpu.VMEM((V7X_SUBLANES, w), F32), pltpu.VMEM((V7X_SUBLANES, w), F32)],
        compiler_params=_cparams("parallel", "arbitrary"),
        name="rglru_mixer",
    )(*args)


RW_CHUNK = 64


def _segsum(x, ones_bf16):
    hi, lo = _split_bf16(x)
    return _dot(hi, ones_bf16) + _dot(lo, ones_bf16)


def _stack_heads(x, head_of_lane):
    return jnp.concatenate([jnp.where(head_of_lane == h, x, 0.0) for h in range(RW_HEADS)], axis=0)


def _rwkv_chunk_kernel(f_ref, fprev_ref, mu_ref, w0_ref, wup_ref, a0_ref, aup_ref, gup_ref, kk_ref, ka_ref,
                       rk_ref, ones_ref, rt2_ref, y1_ref, m_ref, s1_ref, pl_ref, g_ref, bonus_ref,
                       kt_s, rt_s, kib_s, bib_s, vsb_s, vst_s, kip_s, bip_s, a0_s, a1_s, a2_s, avk_s, bbk_s, bvk_s,
                       pw_s, kt2_s, kt2t_s, u0_s, u0t_s, tinv_s, *, bsz, chunk):
    w = W_BRANCH
    sl = V7X_SUBLANES
    n4 = RW_HEADS * chunk
    first = pl.program_id(0) == 0
    ones = ones_ref[...]
    rowc = lax.broadcasted_iota(I32, (chunk, 1), 0)
    head_of_lane = lax.broadcasted_iota(I32, (1, w), 1) // RW_DIM
    ri = lax.broadcasted_iota(I32, (n4, n4), 0)
    ci = lax.broadcasted_iota(I32, (n4, n4), 1)
    strict = ci < ri
    incl = ci <= ri
    same16 = (ri // 16) == (ci // 16)
    same32 = (ri // 32) == (ci // 32)
    eye = (ri == ci).astype(F32)
    tri_c = (lax.broadcasted_iota(I32, (chunk, chunk), 1)
             <= lax.broadcasted_iota(I32, (chunk, chunk), 0)).astype(BF16)

    for b in range(bsz):
        f = f_ref[b]
        tail = jnp.where(first, 0.0, fprev_ref[b])
        prev = jnp.where(rowc < 1,
                         jnp.broadcast_to(pltpu.roll(tail, 1, 0)[None], (chunk // sl, sl, RW_COLS)).reshape(chunk, RW_COLS),
                         pltpu.roll(f, 1, 0))
        f = f + (prev - f) * mu_ref[...]
        r = f[:, 0:w]
        k = f[:, w:2 * w]
        v = f[:, 2 * w:3 * w]
        wd = f[:, 3 * w:3 * w + RW_DECAY_LORA]
        ad = f[:, 3 * w + RW_DECAY_LORA:3 * w + RW_DECAY_LORA + RW_A_LORA]
        gd = f[:, 3 * w + RW_DECAY_LORA + RW_A_LORA:RW_COLS]
        wlog = -_softplus(-(w0_ref[...] + _dot(jnp.tanh(wd).astype(BF16), wup_ref[...]))) - 0.5
        logd = -jnp.exp(wlog)
        a = _sigmoid(a0_ref[...] + _dot(ad.astype(BF16), aup_ref[...]))
        g = _dot(_sigmoid(gd).astype(BF16), gup_ref[...])
        kk = k * kk_ref[...]
        kk = kk / jnp.maximum(jnp.sqrt(_segsum(kk * kk, ones)), 1e-12)
        k2 = k * (1.0 + (a - 1.0) * ka_ref[...])
        beta = kk * a
        l1 = logd.astype(BF16)
        rem = logd - l1.astype(F32)
        l2 = rem.astype(BF16)
        l3 = (rem - l2.astype(F32)).astype(BF16)
        cum = _dot(tri_c, l1) + _dot(tri_c, l2) + _dot(tri_c, l3)
        p = jnp.exp(cum)
        pinv = jnp.exp(-cum)
        kt = _stack_heads(kk * jnp.exp(cum - logd), head_of_lane).astype(BF16)
        rt = _stack_heads(r * p, head_of_lane).astype(BF16)
        ki = _stack_heads(k2 * pinv, head_of_lane)
        bi = _stack_heads(beta * pinv, head_of_lane)
        vs = _stack_heads(v, head_of_lane)
        pl_row = p[chunk - 1:chunk]
        kt_s[b] = kt
        rt_s[b] = rt
        kib_s[b] = ki.astype(BF16)
        bib_s[b] = bi.astype(BF16)
        vsb_s[b] = vs.astype(BF16)
        vst_s[b] = vs.T.astype(BF16)
        kip_s[b] = (ki * pl_row).astype(BF16)
        bip_s[b] = (bi * pl_row).astype(BF16)
        pl_ref[b, 0] = jnp.broadcast_to(pl_row, (sl, w))
        g_ref[b] = g
        bonus_ref[b] = _segsum(r * k2 * rk_ref[...], ones) * v

    for b in range(bsz):
        kt, rt, kib, bib = kt_s[b], rt_s[b], kib_s[b], bib_s[b]
        a_bk = jnp.where(strict, _dot_nt(kt, bib), 0.0)
        a0 = jnp.where(same16, a_bk, 0.0)
        a0_s[b] = a0.astype(BF16)
        a1_s[b] = jnp.where(same32 & jnp.logical_not(same16), a_bk, 0.0).astype(BF16)
        a2_s[b] = jnp.where(same32, 0.0, a_bk).astype(BF16)
        tinv_s[b] = eye - a0
        avk_s[b] = jnp.where(strict, _dot_nt(kt, kib), 0.0).astype(BF16)
        bbk_s[b] = jnp.where(incl, _dot_nt(rt, bib), 0.0).astype(BF16)
        bvk_s[b] = jnp.where(incl, _dot_nt(rt, kib), 0.0).astype(BF16)
    for b in range(bsz):
        pw_s[b] = _dot(a0_s[b], a0_s[b]).astype(BF16)
    for it in range(3):
        for b in range(bsz):
            tinv_s[b] = tinv_s[b] + _dot(tinv_s[b].astype(BF16), pw_s[b])
        if it < 2:
            for b in range(bsz):
                pw_s[b] = _dot(pw_s[b], pw_s[b]).astype(BF16)
    for off_s in (a1_s, a2_s):
        for b in range(bsz):
            pw_s[b] = _dot(tinv_s[b].astype(BF16), off_s[b]).astype(BF16)
        for b in range(bsz):
            tinv_s[b] = tinv_s[b] - _dot(pw_s[b], tinv_s[b].astype(BF16))
    for b in range(bsz):
        kt2 = _dot(tinv_s[b].astype(BF16), kt_s[b])
        kt2_s[b] = kt2.astype(BF16)
        kt2t_s[b] = kt2.T.astype(BF16)
        pw_s[b] = _dot(avk_s[b], vsb_s[b]).astype(BF16)
    for b in range(bsz):
        u0 = _dot(tinv_s[b].astype(BF16), pw_s[b])
        u0_s[b] = u0.astype(BF16)
        u0t_s[b] = u0.T.astype(BF16)
    for b in range(bsz):
        bbk = bbk_s[b]
        y1s = _dot(bvk_s[b], vsb_s[b]) - _dot(bbk, u0_s[b])
        y1 = y1s[0:chunk]
        for h in range(1, RW_HEADS):
            y1 = y1 + y1s[h * chunk:(h + 1) * chunk]
        y1_ref[b] = y1
        rt2_ref[b, 0] = (rt_s[b].astype(F32) - _dot(bbk, kt2_s[b])).astype(BF16)
        m_ref[b, 0] = _dot(kt2t_s[b], bip_s[b]).astype(BF16)
        s1_ref[b, 0] = _dot(vst_s[b], kip_s[b]) - _dot(u0t_s[b], bip_s[b])


def _rwkv_state_kernel(rt2_ref, y1_ref, m_ref, s1_ref, pl_ref, g_ref, bonus_ref, gnw_ref, gnb_ref, ones_ref,
                       o_ref, state_ref, *, bsz, chunk):
    @pl.when(pl.program_id(0) == 0)
    def _():
        state_ref[...] = jnp.zeros_like(state_ref)

    ones = ones_ref[...]
    ys_all, yc_all = [], []
    for b in range(bsz):
        st = state_ref[b]
        stb = st.astype(BF16)
        ys = _dot_nt(rt2_ref[b, 0], stb)
        y = y1_ref[b] + ys[0:chunk]
        for h in range(1, RW_HEADS):
            y = y + ys[h * chunk:(h + 1) * chunk]
        ys_all.append(y)
        state_ref[b] = st * pl_ref[b, 0][0:1] + s1_ref[b, 0] - _dot(stb, m_ref[b, 0])
    for b in range(bsz):
        yc_all.append(ys_all[b] - _segsum(ys_all[b], ones) * (1.0 / RW_DIM))
    for b in range(bsz):
        yc = yc_all[b]
        var = _segsum(yc * yc, ones) * (1.0 / RW_DIM)
        yn = yc * lax.rsqrt(var + RW_GN_EPS) * gnw_ref[...] + gnb_ref[...]
        o_ref[b] = (yn + bonus_ref[b]) * g_ref[b]


def _rwkv_mixer(feat, mu, w0, w_up, a0, a_up, g_up, k_k, k_a, r_k, gn_w, gn_b, bsz, seq):
    w = W_BRANCH
    sl = V7X_SUBLANES
    chunk = min(RW_CHUNK, seq)
    nch = seq // chunk
    n4 = RW_HEADS * chunk
    f3 = feat.reshape(bsz, seq, RW_COLS)
    row1 = lambda a: a.reshape(1, -1)
    ones = _block_diag(jnp.ones((RW_HEADS, RW_DIM, RW_DIM), F32)).astype(BF16)
    full = lambda a: pl.BlockSpec(a.shape, lambda c: (0,) * a.ndim)
    tok = lambda width: pl.BlockSpec((bsz, chunk, width), lambda c: (0, c, 0))
    per_chunk = lambda rows, cols: pl.BlockSpec((bsz, 1, rows, cols), lambda c: (0, c, 0, 0))
    a_args = (row1(mu), row1(w0), w_up.astype(BF16), row1(a0), a_up.astype(BF16), g_up.astype(BF16),
              row1(k_k), row1(k_a), row1(r_k), ones)
    mids = pl.pallas_call(
        functools.partial(_rwkv_chunk_kernel, bsz=bsz, chunk=chunk),
        grid=(nch,),
        in_specs=[tok(RW_COLS),
                  pl.BlockSpec((bsz, sl, RW_COLS), lambda c: (0, jnp.maximum(c * (chunk // sl) - 1, 0), 0))]
        + [full(a) for a in a_args],
        out_specs=[per_chunk(n4, w), tok(w), per_chunk(w, w), per_chunk(w, w), per_chunk(sl, w), tok(w), tok(w)],
        out_shape=[jax.ShapeDtypeStruct((bsz, nch, n4, w), BF16), jax.ShapeDtypeStruct((bsz, seq, w), F32),
                   jax.ShapeDtypeStruct((bsz, nch, w, w), BF16), jax.ShapeDtypeStruct((bsz, nch, w, w), F32),
                   jax.ShapeDtypeStruct((bsz, nch, sl, w), F32), jax.ShapeDtypeStruct((bsz, seq, w), F32),
                   jax.ShapeDtypeStruct((bsz, seq, w), F32)],
        scratch_shapes=[pltpu.VMEM((bsz, n4, w), BF16)] * 19 + [pltpu.VMEM((bsz, n4, w), F32)],
        compiler_params=_cparams("parallel"),
        name="rwkv_chunks",
    )(f3, f3, *a_args)
    b_args = (row1(gn_w), row1(gn_b), ones)
    out = pl.pallas_call(
        functools.partial(_rwkv_state_kernel, bsz=bsz, chunk=chunk),
        grid=(nch,),
        in_specs=[per_chunk(n4, w), tok(w), per_chunk(w, w), per_chunk(w, w), per_chunk(sl, w), tok(w), tok(w)]
        + [full(a) for a in b_args],
        out_specs=tok(w),
        out_shape=jax.ShapeDtypeStruct((bsz, seq, w), F32),
        scratch_shapes=[pltpu.VMEM((bsz, w, w), F32)],
        compiler_params=_cparams("arbitrary"),
        name="rwkv_state",
    )(*mids, *b_args)
    return out.reshape(bsz * seq, w)


DSA_TQ = 256
DSA_KC = 512


DSA_SUB = 128
DSA_VROWS = ATT_DIM + 16
KEY_NEG_INF = INT_MIN + 0x7FFFFF


def _ordered_to_f32(key):
    return lax.bitcast_convert_type(jnp.where(key >= 0, key, key ^ 0x7FFFFFFF), F32)


def _dsa_kernel(*refs, seq, tq, kc, topk):
    def block(i, carry):
        _dsa_block(i, *refs, seq=seq, tq=tq, kc=kc, topk=topk)
        return carry

    lax.fori_loop(0, seq // tq, block, 0)


def _dsa_block(i, qT_ref, k4_ref, vx_ref, iq3_ref, ik3_ref, iwT_ref, o_ref, sc_ref, lga_ref, lgb_ref, bias_ref, p_ref, acc_ref,
               *, seq, tq, kc, topk):
    nchunks = lax.div((i + 1) * tq + (kc - 1), kc)
    qpos = i * tq + lax.broadcasted_iota(I32, (1, tq), 1)
    rows = lax.broadcasted_iota(I32, (kc, 1), 0)
    wT = iwT_ref[0, i]
    iq3 = iq3_ref[0, i]

    def score_body(c, carry):
        s0 = pl.multiple_of(c * kc, kc)
        d = _dot(ik3_ref[pl.ds(s0, kc), :], iq3)
        sc = wT[0:1] * jnp.maximum(d[:, 0:tq], 0.0)
        for h in range(1, IDX_HEADS):
            sc = sc + wT[h:h + 1] * jnp.maximum(d[:, h * tq:(h + 1) * tq], 0.0)
        sc_ref[pl.ds(s0, kc), :] = jnp.where(s0 + rows <= qpos, sc, -jnp.inf)
        return carry

    lax.fori_loop(0, nchunks, score_body, 0)

    def count(ind_fn):
        def body(c, acc):
            s0 = pl.multiple_of(c * kc, kc)
            ind = ind_fn(sc_ref[pl.ds(s0, kc), :], s0 + rows)
            return acc + ind.reshape(kc // V7X_SUBLANES, V7X_SUBLANES, tq).sum(axis=0)
        acc = lax.fori_loop(0, nchunks, body, jnp.zeros((V7X_SUBLANES, tq), I32))
        return jnp.sum(acc, axis=0, keepdims=True)

    c0 = count(lambda s, idx: jnp.where(s >= 0.0, 1, 0))
    ans = jnp.where(c0 >= topk, 0, INT_MIN).astype(I32)

    def bit_body(j, ans):
        cand = ans | lax.shift_left(jnp.int32(1), 30 - j)
        cf = _ordered_to_f32(cand)
        cnt = count(lambda s, idx: jnp.where(s >= cf, 1, 0))
        return jnp.where(cand <= KEY_NEG_INF, cand, jnp.where(cnt >= topk, cand, ans))

    thr = _ordered_to_f32(lax.fori_loop(0, 31, bit_body, ans))
    cnt_gt = count(lambda s, idx: jnp.where(s > thr, 1, 0))
    need = (topk - cnt_gt).astype(F32)

    tri = (lax.broadcasted_iota(I32, (kc, kc), 1) <= lax.broadcasted_iota(I32, (kc, kc), 0)).astype(BF16)

    def tie_body(c, seen):
        s0 = pl.multiple_of(c * kc, kc)
        sch = sc_ref[pl.ds(s0, kc), :]
        eq = sch == thr
        rank = _dot(tri, jnp.where(eq, 1.0, 0.0).astype(BF16)) + seen
        sc_ref[pl.ds(s0, kc), :] = jnp.where(eq, jnp.where(rank > need, -jnp.inf, sch), sch)
        return rank[kc - 1:kc]

    lax.fori_loop(0, nchunks, tie_body, jnp.zeros((1, tq), F32))
    thr = jnp.maximum(thr, jnp.finfo(F32).min)

    q_all = qT_ref[0, i]
    hd = ATT_DIM

    def logits_into(buf_ref, c, live):
        s0 = pl.multiple_of(c * kc, kc)
        sch = sc_ref[pl.ds(s0, kc), :]
        dead = jnp.where(live, 0.0, NEG_BIG)
        bias_ref[...] = jnp.where(sch >= thr, dead, NEG_BIG)
        for h in range(ATT_HEADS):
            buf_ref[h] = _dot(k4_ref[0, h, pl.ds(s0, kc), :], q_all[h * hd:(h + 1) * hd]) + bias_ref[...]

    def softmax_step(buf_ref, c, state):
        for h in range(ATT_HEADS):
            m = state[h]
            part = buf_ref[h].reshape(kc // V7X_SUBLANES, V7X_SUBLANES, tq).max(axis=0)
            mn = jnp.maximum(m, jnp.max(part, axis=0, keepdims=True))
            state[h] = mn
            p_ref[h] = jnp.exp2(buf_ref[h] - mn).astype(BF16)
            acc_ref[h] = jnp.exp2(m - mn) * acc_ref[h] + _dot(vx_ref[0, c, h], p_ref[h])

    last = nchunks - 1
    acc_ref[...] = jnp.zeros_like(acc_ref)
    logits_into(lga_ref, 0, True)

    def att_body(t, carry):
        state = list(carry)
        c0 = 2 * t
        c1 = jnp.minimum(c0 + 1, last)
        logits_into(lgb_ref, c1, c0 + 1 <= last)
        softmax_step(lga_ref, c0, state)
        logits_into(lga_ref, jnp.minimum(c0 + 2, last), True)
        softmax_step(lgb_ref, c1, state)
        return tuple(state)

    init = (jnp.full((1, tq), NEG_BIG, F32),) * ATT_HEADS
    lax.fori_loop(0, lax.div(nchunks + 1, 2), att_body, init)
    for h in range(ATT_HEADS):
        acc = acc_ref[h]
        o_ref[0, i, h * hd:(h + 1) * hd, :] = acc[0:hd] / acc[hd:hd + 1]


def _dsa_mixer(q, k, v, iqh, iql, ikh, ikl, iw, bsz, seq):
    w = W_BRANCH
    tq = min(DSA_TQ, seq)
    kc = min(DSA_KC, seq)
    nq = seq // tq
    nc = seq // kc
    topk = min(TOPK_MAX, seq // 4)
    assert kc >= topk and kc % DSA_SUB == 0
    qT = q.reshape(bsz, nq, tq, w).transpose(0, 1, 3, 2)
    k4 = k.reshape(bsz, seq, ATT_HEADS, ATT_DIM).transpose(0, 2, 1, 3)
    vT = v.reshape(bsz, nc, kc, ATT_HEADS, ATT_DIM).transpose(0, 1, 3, 4, 2)
    vx = jnp.concatenate([vT, jnp.ones((bsz, nc, ATT_HEADS, DSA_VROWS - ATT_DIM, kc), BF16)], axis=3)
    ik3 = jnp.concatenate([ikh, ikh, ikl], axis=1)
    iq3 = jnp.stack([iqh, iql, iqh], axis=1).reshape(bsz, nq, tq, 3, IDX_HEADS, IDX_DIM)
    iq3 = iq3.transpose(0, 1, 3, 5, 4, 2).reshape(bsz, nq, 3 * IDX_DIM, IDX_HEADS * tq)
    iwT = iw.reshape(bsz, nq, tq, IDX_HEADS).transpose(0, 1, 3, 2)
    seq_blk = lambda a: pl.BlockSpec((1,) + a.shape[1:], lambda b: (b,) + (0,) * (a.ndim - 1))
    out = pl.pallas_call(
        functools.partial(_dsa_kernel, seq=seq, tq=tq, kc=kc, topk=topk),
        grid=(bsz,),
        in_specs=[seq_blk(qT), seq_blk(k4), seq_blk(vx), seq_blk(iq3),
                  pl.BlockSpec((seq, 3 * IDX_DIM), lambda b: (b, 0)), seq_blk(iwT)],
        out_specs=pl.BlockSpec((1, nq, w, tq), lambda b: (b, 0, 0, 0)),
        out_shape=jax.ShapeDtypeStruct((bsz, nq, w, tq), F32),
        scratch_shapes=[pltpu.VMEM((seq, tq), F32), pltpu.VMEM((ATT_HEADS, kc, tq), F32),
                        pltpu.VMEM((ATT_HEADS, kc, tq), F32), pltpu.VMEM((kc, tq), F32),
                        pltpu.VMEM((ATT_HEADS, kc, tq), BF16), pltpu.VMEM((ATT_HEADS, DSA_VROWS, tq), F32)],
        compiler_params=_cparams("parallel"),
        name="dsa_mixer",
    )(qT, k4, vx, iq3, ik3, iwT)
    return out.transpose(0, 1, 3, 2).reshape(bsz * seq, w)


def _merge_kernel(x_ref, mod_ref, g_ref, o0_ref, o1_ref, o2_ref, o3_ref, gw_ref, gb_ref, bw_ref, ow_ref, x1_ref):
    d = x_ref.shape[1]
    mod = mod_ref[0]
    x = x_ref[...]
    hb = _modulated_norm(x, g_ref[...], mod[1:2], mod[0:1]).astype(BF16)
    mixed = jnp.zeros(x.shape, F32)
    for n, o_ref in enumerate((o0_ref, o1_ref, o2_ref, o3_ref)):
        gate = _sigmoid(_dot(hb, gw_ref[:, n * d:(n + 1) * d]) + gb_ref[:, n * d:(n + 1) * d])
        mixed = mixed + gate * _dot(o_ref[...].astype(BF16), bw_ref[n])
    x1_ref[...] = x + mod[2:3] * _dot(mixed.astype(BF16), ow_ref[...])


def _merge(x2, mod_l, g, branches, gate_w, gate_b, branch_w, out_w, seq):
    n, d = x2.shape
    w = W_BRANCH
    tm = min(512, seq)
    tpb = seq // tm
    row = lambda width: pl.BlockSpec((tm, width), lambda i: (i, 0))
    full = lambda a: pl.BlockSpec(a.shape, lambda i: (0,) * a.ndim)
    args = (gate_w.astype(BF16), gate_b.reshape(1, -1), branch_w.astype(BF16), out_w.astype(BF16))
    return pl.pallas_call(
        _merge_kernel,
        grid=(n // tm,),
        in_specs=[row(d), pl.BlockSpec((1, 6, d), lambda i: (i // tpb, 0, 0)), full(g)]
        + [row(w)] * N_BRANCH + [full(a) for a in args],
        out_specs=row(d),
        out_shape=jax.ShapeDtypeStruct((n, d), F32),
        compiler_params=_cparams("parallel"),
        name="merge",
    )(x2, mod_l, g, *branches, *args)


def _route_combine(scores, rb):
    biased = scores + rb
    col = lambda a, e: a[:, e:e + 1]
    npg = EXP_PER_GROUP
    gs = []
    for g in range(N_GROUPS):
        best = None
        for j1 in range(npg):
            for j2 in range(j1 + 1, npg):
                s = col(biased, npg * g + j1) + col(biased, npg * g + j2)
                best = s if best is None else jnp.maximum(best, s)
        gs.append(best)
    bg = jnp.zeros(gs[0].shape, I32)
    bv = gs[0]
    for g in range(1, N_GROUPS):
        upd = gs[g] > bv
        bv = jnp.where(upd, gs[g], bv)
        bg = jnp.where(upd, g, bg)
    bsel, ssel = [], []
    for j in range(npg):
        bj, sj = col(biased, j), col(scores, j)
        for g in range(1, N_GROUPS):
            bj = jnp.where(bg == g, col(biased, npg * g + j), bj)
            sj = jnp.where(bg == g, col(scores, npg * g + j), sj)
        bsel.append(bj)
        ssel.append(sj)
    i1, v1, s1 = jnp.zeros(bg.shape, I32), bsel[0], ssel[0]
    for j in range(1, npg):
        upd = bsel[j] > v1
        v1 = jnp.where(upd, bsel[j], v1)
        s1 = jnp.where(upd, ssel[j], s1)
        i1 = jnp.where(upd, j, i1)
    i2, v2, s2 = jnp.zeros(bg.shape, I32), jnp.where(i1 == 0, -jnp.inf, bsel[0]), ssel[0]
    for j in range(1, npg):
        cand = jnp.where(i1 == j, -jnp.inf, bsel[j])
        upd = cand > v2
        v2 = jnp.where(upd, cand, v2)
        s2 = jnp.where(upd, ssel[j], s2)
        i2 = jnp.where(upd, j, i2)
    tot = s1 + s2
    lane = lax.broadcasted_iota(I32, scores.shape, 1)
    return (jnp.where(lane == npg * bg + i1, s1 / tot, 0.0)
            + jnp.where(lane == npg * bg + i2, s2 / tot, 0.0))


def _route_kernel(x1_ref, mod_ref, g_ref, rwh_ref, rwl_ref, rb_ref, h2_ref, comb_ref):
    mod = mod_ref[0]
    h2 = _modulated_norm(x1_ref[...], g_ref[...], mod[4:5], mod[3:4])
    h2_ref[...] = h2.astype(BF16)
    hh, hl = _split_bf16(h2)
    logits = _dot(hh, rwh_ref[...]) + _dot(hh, rwl_ref[...]) + _dot(hl, rwh_ref[...])
    comb_ref[...] = _route_combine(_sigmoid(logits[:, 0:N_EXPERTS]), rb_ref[...])


def _route(x1, mod_l, g, router_w, router_b, seq):
    n, d = x1.shape
    tm = min(512, seq)
    tpb = seq // tm
    rwh, rwl = _split_bf16(jnp.pad(router_w, ((0, 0), (0, V7X_LANES - N_EXPERTS))))
    row = lambda width: pl.BlockSpec((tm, width), lambda i: (i, 0))
    full = lambda a: pl.BlockSpec(a.shape, lambda i: (0,) * a.ndim)
    rb = router_b.reshape(1, N_EXPERTS)
    return pl.pallas_call(
        _route_kernel,
        grid=(n // tm,),
        in_specs=[row(d), pl.BlockSpec((1, 6, d), lambda i: (i // tpb, 0, 0)), full(g), full(rwh), full(rwl), full(rb)],
        out_specs=[row(d), row(N_EXPERTS)],
        out_shape=[jax.ShapeDtypeStruct((n, d), BF16), jax.ShapeDtypeStruct((n, N_EXPERTS), F32)],
        compiler_params=_cparams("parallel"),
        name="route",
    )(x1, mod_l, g, rwh, rwl, rb)


def _moe_kernel(h2_ref, c_ref, w1_ref, w3_ref, w2_ref, x1_ref, mod_ref, fg_ref, o_ref, acc_ref, *, final_norm):
    e = pl.program_id(1)

    @pl.when(e == 0)
    def _():
        acc_ref[...] = jnp.zeros_like(acc_ref)

    h2 = h2_ref[...]
    a = _dot(h2, w1_ref[0, 0].astype(BF16))
    b = _dot(h2, w3_ref[0, 0].astype(BF16))
    y = _dot((a * _sigmoid(a) * b).astype(BF16), w2_ref[0, 0].astype(BF16))
    acc_ref[...] += c_ref[0] * y

    @pl.when(e == pl.num_programs(1) - 1)
    def _():
        out = x1_ref[...] + mod_ref[0][5:6] * acc_ref[...]
        if final_norm:
            ms = jnp.mean(out * out, axis=-1, keepdims=True)
            out = out * lax.rsqrt(ms + NORM_EPS) * fg_ref[...]
        o_ref[...] = out


def _moe_dense(h2, comb, exp_w1, exp_w3, exp_w2, layer, x1, mod_l, final_g, seq, final_norm):
    n, d = x1.shape
    ne, f = exp_w1.shape[1], exp_w1.shape[3]
    tm = min(1024, seq)
    tpb = seq // tm
    c3 = comb.T.reshape(ne, n, 1)
    row = lambda width: pl.BlockSpec((tm, width), lambda i, e: (i, 0))
    return pl.pallas_call(
        functools.partial(_moe_kernel, final_norm=final_norm),
        grid=(n // tm, ne),
        in_specs=[row(d), pl.BlockSpec((1, tm, 1), lambda i, e: (e, i, 0)),
                  pl.BlockSpec((1, 1, d, f), lambda i, e: (layer, e, 0, 0)),
                  pl.BlockSpec((1, 1, d, f), lambda i, e: (layer, e, 0, 0)),
                  pl.BlockSpec((1, 1, f, d), lambda i, e: (layer, e, 0, 0)),
                  row(d), pl.BlockSpec((1, 6, d), lambda i, e: (i // tpb, 0, 0)),
                  pl.BlockSpec((1, d), lambda i, e: (0, 0))],
        out_specs=row(d),
        out_shape=jax.ShapeDtypeStruct((n, d), F32),
        scratch_shapes=[pltpu.VMEM((tm, d), F32)],
        compiler_params=_cparams("parallel", "arbitrary"),
        name="moe_dense",
    )(h2, c3, exp_w1, exp_w3, exp_w2, x1, mod_l, final_g)


def kernel(x, c, positions, ada_w, ada_b, mix_norm_g, w_in, gate_w, gate_b, branch_w, out_w, s5_lam_re, s5_lam_im, s5_log_step, s5_b_re, s5_b_im, s5_c_re, s5_c_im, s5_d, s5_glu_w, s5_glu_b, rg_conv_w, rg_conv_b, rg_wr, rg_br, rg_wi, rg_bi, rg_lam, rw_mu, rw_w0, rw_w_up, rw_a0, rw_a_up, rw_g_up, rw_k_k, rw_k_a, rw_r_k, rw_gn_w, rw_gn_b, ffn_norm_g, router_w, router_b, exp_w1, exp_w3, exp_w2, final_norm_g):
    bsz, seq, d = x.shape
    depth = ada_w.shape[0]
    n = bsz * seq
    x2 = x.reshape(n, d)
    mod = _ada_mod(c, ada_w, ada_b)
    tabs = _rope_tables(positions)
    fg = final_norm_g.reshape(1, d)
    for l in range(depth):
        mod_l = mod[l]
        g1 = mix_norm_g[l].reshape(1, d)
        (u, q, k, v, iqh, iql, ikh, ikl, iw, rgx, rgg, rwf) = _inproj(x2, mod_l, g1, w_in[l], tabs, seq)
        s5p = _s5_params(s5_lam_re[l], s5_lam_im[l], s5_log_step[l], s5_b_re[l], s5_b_im[l], s5_c_re[l], s5_c_im[l])
        o_s5 = _s5_mixer(u, s5p, s5_d[l], s5_glu_w[l], s5_glu_b[l], bsz, seq)
        o_dsa = _dsa_mixer(q, k, v, iqh, iql, ikh, ikl, iw, bsz, seq)
        o_rg = _rglru_mixer(rgx, rgg, rg_conv_w[l], rg_conv_b[l], rg_wr[l], rg_br[l], rg_wi[l], rg_bi[l],
                            rg_lam[l], bsz, seq)
        o_rw = _rwkv_mixer(rwf, rw_mu[l], rw_w0[l], rw_w_up[l], rw_a0[l], rw_a_up[l], rw_g_up[l], rw_k_k[l],
                           rw_k_a[l], rw_r_k[l], rw_gn_w[l], rw_gn_b[l], bsz, seq)
        x1 = _merge(x2, mod_l, g1, (o_s5, o_dsa, o_rg, o_rw), gate_w[l], gate_b[l], branch_w[l], out_w[l], seq)
        h2, comb = _route(x1, mod_l, ffn_norm_g[l].reshape(1, d), router_w, router_b, seq)
        x2 = _moe_dense(h2, comb, exp_w1, exp_w3, exp_w2, l, x1, mod_l, fg, seq, l == depth - 1)
    return x2.reshape(bsz, seq, d)
```

```python
import functools
import math

import jax
import jax.numpy as jnp
from jax import lax
from jax.experimental import pallas as pl
from jax.experimental.pallas import tpu as pltpu

F32 = jnp.float32
BF16 = jnp.bfloat16
I32 = jnp.int32

W_BRANCH = 256
N_BRANCH = 4
NORM_EPS = 1e-6
S5_GROUP = 16
S5_GROUPS = W_BRANCH // S5_GROUP
S5_STATE = 64
ATT_HEADS = 4
ATT_DIM = W_BRANCH // ATT_HEADS
IDX_HEADS = 4
IDX_DIM = 32
TOPK_MAX = 256
ROPE_THETA = 10000.0
RG_BLOCKS = 4
RG_C = 8.0
CONV_WIDTH = 4
RW_HEADS = 4
RW_DIM = W_BRANCH // RW_HEADS
RW_DECAY_LORA = 32
RW_A_LORA = 32
RW_GATE_LORA = 64
RW_GN_EPS = 64e-5
RW_COLS = 3 * W_BRANCH + RW_DECAY_LORA + RW_A_LORA + RW_GATE_LORA
N_EXPERTS = 16
N_GROUPS = 4
EXP_PER_GROUP = N_EXPERTS // N_GROUPS
D_EXPERT = 512

V7X_SUBLANES = 8
V7X_LANES = 128
V7X_VMEM_LIMIT_BYTES = 56 * 1024 * 1024

INT_MIN = -(2 ** 31)
NEG_BIG = -1e30


def _cparams(*sem, flags=None):
    return pltpu.CompilerParams(dimension_semantics=sem, vmem_limit_bytes=V7X_VMEM_LIMIT_BYTES, flags=flags)


def _split_bf16(x):
    hi = x.astype(BF16)
    lo = (x - hi.astype(F32)).astype(BF16)
    return hi, lo


def _dot(a, b):
    return jnp.dot(a, b, preferred_element_type=F32)


def _dot_nt(a, b):
    return lax.dot_general(a, b, (((1,), (1,)), ((), ())), preferred_element_type=F32)


def _dot_tn(a, b):
    return lax.dot_general(a, b, (((0,), (0,)), ((), ())), preferred_element_type=F32)


def _dot3(a, b):
    ah, al = _split_bf16(a)
    bh, bl = _split_bf16(b)
    return _dot(ah, bh) + _dot(ah, bl) + _dot(al, bh)


def _sigmoid(x):
    return 1.0 / (1.0 + jnp.exp(-x))


def _softplus(x):
    return jnp.maximum(x, 0.0) + jnp.log(1.0 + jnp.exp(-jnp.abs(x)))


def _gelu(x):
    c = math.sqrt(2.0 / math.pi)
    return 0.5 * x * (1.0 + jnp.tanh(c * (x + 0.044715 * (x * x * x))))


def _rot_half_cols(w, n_heads, dim):
    lead = w.shape[0]
    w4 = w.reshape(lead, n_heads, 2, dim // 2)
    return jnp.concatenate([-w4[:, :, 1:2], w4[:, :, 0:1]], axis=2).reshape(lead, n_heads * dim)


def _ada_kernel(c_ref, w_ref, b_ref, o_ref):
    c = c_ref[...]
    ca = c * _sigmoid(c)
    o_ref[0] = _dot3(ca, w_ref[0]) + b_ref[0]


def _ada_mod(c, ada_w, ada_b):
    depth, d, d6 = ada_w.shape
    bsz = c.shape[0]
    rows = -(-bsz // V7X_SUBLANES) * V7X_SUBLANES
    cp = jnp.zeros((rows, d), F32).at[:bsz].set(c)
    out = pl.pallas_call(
        _ada_kernel,
        grid=(depth, d6 // d),
        in_specs=[pl.BlockSpec((rows, d), lambda l, j: (0, 0)),
                  pl.BlockSpec((1, d, d), lambda l, j: (l, 0, j)),
                  pl.BlockSpec((1, 1, d), lambda l, j: (l, 0, j))],
        out_specs=pl.BlockSpec((1, rows, d), lambda l, j: (l, 0, j)),
        out_shape=jax.ShapeDtypeStruct((depth, rows, d6), F32),
        compiler_params=_cparams("parallel", "parallel"),
        name="ada_mod",
    )(cp, ada_w, ada_b.reshape(depth, 1, d6))
    return out[:, :bsz].reshape(depth, bsz, 6, d)


def _rope_kernel(pos_ref, fa_ref, fi_ref, ca_ref, sa_ref, ci_ref, si_ref):
    pos = pos_ref[...]
    ang_a = pos * fa_ref[...]
    ang_i = pos * fi_ref[...]
    ca_ref[...] = jnp.cos(ang_a)
    sa_ref[...] = jnp.sin(ang_a)
    ci_ref[...] = jnp.cos(ang_i)
    si_ref[...] = jnp.sin(ang_i)


def _rope_tables(positions):
    n = positions.size
    pos = positions.reshape(n, 1).astype(F32)

    def freq_row(dim):
        half = dim // 2
        inv = ROPE_THETA ** (-jnp.arange(half, dtype=F32) / half)
        return jnp.tile(inv, V7X_LANES // half).reshape(1, V7X_LANES)

    tm = min(n, 2048)
    row = pl.BlockSpec((tm, V7X_LANES), lambda i: (i, 0))
    frq = pl.BlockSpec((1, V7X_LANES), lambda i: (0, 0))
    shp = jax.ShapeDtypeStruct((n, V7X_LANES), F32)
    return pl.pallas_call(
        _rope_kernel,
        grid=(n // tm,),
        in_specs=[pl.BlockSpec((tm, 1), lambda i: (i, 0)), frq, frq],
        out_specs=[row, row, row, row],
        out_shape=[shp, shp, shp, shp],
        compiler_params=_cparams("parallel"),
        name="rope_tables",
    )(pos, freq_row(ATT_DIM), freq_row(IDX_DIM))


def _modulated_norm(x, g, scale, shift):
    ms = jnp.mean(x * x, axis=-1, keepdims=True)
    return (x * lax.rsqrt(ms + NORM_EPS)) * g * (1.0 + scale) + shift


def _inproj_kernel(x_ref, mod_ref, g_ref, wm_ref, wih_ref, wil_ref,
                   ca_ref, sa_ref, ci_ref, si_ref,
                   u_ref, q_ref, k_ref, v_ref, iqh_ref, iql_ref, ikh_ref, ikl_ref, iw_ref,
                   rgx_ref, rgg_ref, rw_ref):
    w = W_BRANCH
    mod = mod_ref[0]
    h = _modulated_norm(x_ref[...], g_ref[...], mod[1:2], mod[0:1])
    hh, hl = _split_bf16(h)
    main = _dot(hh, wm_ref[...])
    idx = _dot(hh, wih_ref[...]) + _dot(hh, wil_ref[...]) + _dot(hl, wih_ref[...])
    ca = jnp.concatenate([ca_ref[...], ca_ref[...]], axis=1)
    sa = jnp.concatenate([sa_ref[...], sa_ref[...]], axis=1)
    u_ref[...] = main[:, 0:w]
    q_ref[...] = (main[:, w:2 * w] * ca + main[:, 2 * w:3 * w] * sa).astype(BF16)
    k_ref[...] = (main[:, 3 * w:4 * w] * ca + main[:, 4 * w:5 * w] * sa).astype(BF16)
    v_ref[...] = main[:, 5 * w:6 * w].astype(BF16)
    rgx_ref[...] = main[:, 6 * w:7 * w]
    rgg_ref[...] = main[:, 7 * w:8 * w]
    rw_ref[...] = main[:, 8 * w:8 * w + RW_COLS]
    ci = ci_ref[...]
    si = si_ref[...]
    iq = idx[:, 0:128] * ci + idx[:, 128:256] * si
    ik = idx[:, 256:288] * ci[:, 0:IDX_DIM] + idx[:, 288:320] * si[:, 0:IDX_DIM]
    iqh, iql = _split_bf16(iq)
    ikh, ikl = _split_bf16(ik)
    iqh_ref[...] = iqh
    iql_ref[...] = iql
    ikh_ref[...] = ikh
    ikl_ref[...] = ikl
    iw_ref[...] = idx[:, 320:320 + IDX_HEADS]


def _inproj_weights(w_in):
    w = W_BRANCH
    o = 0
    parts = {}
    for name, width in (("u", w), ("q", w), ("k", w), ("v", w), ("iq", IDX_HEADS * IDX_DIM),
                        ("ik", IDX_DIM), ("iw", IDX_HEADS), ("rgx", w), ("rgg", w), ("rw", RW_COLS)):
        parts[name] = w_in[:, o:o + width]
        o += width
    wq = parts["q"] * (ATT_DIM ** -0.5 * math.log2(math.e))
    main = jnp.concatenate([parts["u"], wq, _rot_half_cols(wq, ATT_HEADS, ATT_DIM),
                            parts["k"], _rot_half_cols(parts["k"], ATT_HEADS, ATT_DIM),
                            parts["v"], parts["rgx"], parts["rgg"], parts["rw"]], axis=1)
    pad = (-main.shape[1]) % V7X_LANES
    main = jnp.pad(main, ((0, 0), (0, pad)))
    idx = jnp.concatenate([parts["iq"], _rot_half_cols(parts["iq"], IDX_HEADS, IDX_DIM),
                           parts["ik"], _rot_half_cols(parts["ik"], 1, IDX_DIM), parts["iw"]], axis=1)
    idx = jnp.pad(idx, ((0, 0), (0, 384 - idx.shape[1])))
    ih, il = _split_bf16(idx)
    return main.astype(BF16), ih, il


def _inproj(x2, mod_l, g, w_in, tabs, seq):
    n, d = x2.shape
    tm = min(512, seq)
    tpb = seq // tm
    wm, wih, wil = _inproj_weights(w_in)
    w = W_BRANCH
    row = lambda width: pl.BlockSpec((tm, width), lambda i: (i, 0))
    full = lambda a: pl.BlockSpec(a.shape, lambda i: (0,) * a.ndim)
    shp = lambda width, dt: jax.ShapeDtypeStruct((n, width), dt)
    tab = row(V7X_LANES)
    return pl.pallas_call(
        _inproj_kernel,
        grid=(n // tm,),
        in_specs=[row(d), pl.BlockSpec((1, 6, d), lambda i: (i // tpb, 0, 0)), full(g),
                  full(wm), full(wih), full(wil), tab, tab, tab, tab],
        out_specs=[row(w), row(w), row(w), row(w), row(128), row(128), row(IDX_DIM), row(IDX_DIM),
                   row(IDX_HEADS), row(w), row(w), row(RW_COLS)],
        out_shape=[shp(w, F32), shp(w, BF16), shp(w, BF16), shp(w, BF16), shp(128, BF16), shp(128, BF16),
                   shp(IDX_DIM, BF16), shp(IDX_DIM, BF16), shp(IDX_HEADS, F32),
                   shp(w, F32), shp(w, F32), shp(RW_COLS, F32)],
        compiler_params=_cparams("parallel"),
        name="inproj",
    )(x2, mod_l, g, wm, wih, wil, *tabs)


def _s5_prep_kernel(lre_r, lim_r, st_r, lre_c, lim_c, st_c, bre_ref, bim_ref,
                    pre_ref, pim_ref, bbre_ref, bbim_ref):
    kk = (lax.broadcasted_iota(I32, (V7X_SUBLANES, 1), 0) + 1).astype(F32)
    step = jnp.exp(st_r[...])
    mag = jnp.exp(kk * (lre_r[...] * step))
    ang = kk * (lim_r[...] * step)
    pre_ref[...] = mag * jnp.cos(ang)
    pim_ref[...] = mag * jnp.sin(ang)
    a = lre_c[...]
    b = lim_c[...]
    stc = jnp.exp(st_c[...])
    m1 = jnp.exp(a * stc)
    nr = m1 * jnp.cos(b * stc) - 1.0
    ni = m1 * jnp.sin(b * stc)
    den = a * a + b * b
    cr = (nr * a + ni * b) / den
    ci = (ni * a - nr * b) / den
    bre = bre_ref[...]
    bim = bim_ref[...]
    bbre_ref[...] = cr * bre - ci * bim
    bbim_ref[...] = cr * bim + ci * bre


def _block_diag(blocks):
    g, r, c = blocks.shape
    eye = jnp.eye(g, dtype=blocks.dtype)
    return (blocks[:, :, None, :] * eye[:, None, :, None]).reshape(g * r, g * c)


def _s5_params(lam_re, lam_im, log_step, b_re, b_im, c_re, c_im):
    g, p, c = b_re.shape
    ns = g * p
    st = jnp.repeat(log_step, p)
    rows = [a.reshape(1, ns) for a in (lam_re, lam_im, st)]
    cols = [a.reshape(ns, 1) for a in (lam_re, lam_im, st)]
    full = lambda a: pl.BlockSpec(a.shape, lambda: (0,) * a.ndim)
    args = rows + cols + [b_re.reshape(ns, c), b_im.reshape(ns, c)]
    pre, pim, bbre, bbim = pl.pallas_call(
        _s5_prep_kernel,
        in_specs=[full(a) for a in args],
        out_specs=[pl.BlockSpec((V7X_SUBLANES, ns), lambda: (0, 0))] * 2
        + [pl.BlockSpec((ns, c), lambda: (0, 0))] * 2,
        out_shape=[jax.ShapeDtypeStruct((V7X_SUBLANES, ns), F32)] * 2
        + [jax.ShapeDtypeStruct((ns, c), F32)] * 2,
        name="s5_prep",
    )(*args)
    bmat = jnp.concatenate([_block_diag(bbre.reshape(g, p, c).transpose(0, 2, 1)),
                            _block_diag(bbim.reshape(g, p, c).transpose(0, 2, 1))], axis=1)
    cmat = jnp.concatenate([_block_diag(c_re.transpose(0, 2, 1)),
                            -_block_diag(c_im.transpose(0, 2, 1))], axis=0)
    return pre, pim, bmat.astype(BF16), cmat.astype(BF16)


def _s5_kernel(u_ref, bmat_ref, cmat_ref, pre_ref, pim_ref, d_ref, gw_ref, gb_ref,
               o_ref, h_ref, carry_ref, *, chunk, ns):
    @pl.when(pl.program_id(1) == 0)
    def _():
        carry_ref[...] = jnp.zeros_like(carry_ref)

    u = u_ref[...]
    bu = _dot(u.astype(BF16), bmat_ref[...])
    nt = chunk // V7X_SUBLANES
    xre = bu[:, :ns].reshape(nt, V7X_SUBLANES, ns)
    xim = bu[:, ns:].reshape(nt, V7X_SUBLANES, ns)
    pre = pre_ref[...]
    pim = pim_ref[...]
    rowi = lax.broadcasted_iota(I32, (V7X_SUBLANES, 1), 0)
    for s in (1, 2, 4):
        cre = jnp.where(rowi >= s, pre[s - 1:s], 0.0)[None]
        cim = jnp.where(rowi >= s, pim[s - 1:s], 0.0)[None]
        sre = pltpu.roll(xre, s, 1)
        sim = pltpu.roll(xim, s, 1)
        xre, xim = xre + (cre * sre - cim * sim), xim + (cre * sim + cim * sre)
    cr = carry_ref[0:1, :]
    ci = carry_ref[1:2, :]
    for t in range(nt):
        hre = xre[t] + (pre * cr - pim * ci)
        him = xim[t] + (pre * ci + pim * cr)
        h_ref[t * V7X_SUBLANES:(t + 1) * V7X_SUBLANES, 0:ns] = hre
        h_ref[t * V7X_SUBLANES:(t + 1) * V7X_SUBLANES, ns:2 * ns] = him
        cr = hre[V7X_SUBLANES - 1:V7X_SUBLANES]
        ci = him[V7X_SUBLANES - 1:V7X_SUBLANES]
    carry_ref[0:1, :] = cr
    carry_ref[1:2, :] = ci
    y = _dot(h_ref[...].astype(BF16), cmat_ref[...]) + d_ref[...] * u
    y = _gelu(y)
    z = _dot(y.astype(BF16), gw_ref[...]) + gb_ref[...]
    o_ref[...] = y * _sigmoid(z)


def _s5_mixer(u, params, d_skip, glu_w, glu_b, bsz, seq):
    pre, pim, bmat, cmat = params
    n, w = u.shape
    ns = pre.shape[1]
    chunk = min(256, seq)
    cpb = seq // chunk
    full = lambda a: pl.BlockSpec(a.shape, lambda b, c: (0,) * a.ndim)
    row = pl.BlockSpec((chunk, w), lambda b, c: (b * cpb + c, 0))
    args = (u, bmat, cmat, pre, pim, d_skip.reshape(1, w), glu_w.astype(BF16), glu_b.reshape(1, w))
    return pl.pallas_call(
        functools.partial(_s5_kernel, chunk=chunk, ns=ns),
        grid=(bsz, cpb),
        in_specs=[row] + [full(a) for a in args[1:]],
        out_specs=row,
        out_shape=jax.ShapeDtypeStruct((n, w), F32),
        scratch_shapes=[pltpu.VMEM((chunk, 2 * ns), F32), pltpu.VMEM((V7X_SUBLANES, ns), F32)],
        compiler_params=_cparams("parallel", "arbitrary"),
        name="s5_mixer",
    )(*args)


def _rglru_kernel(x_ref, gate_ref, cw_ref, cb_ref, wr_ref, br_ref, wi_ref, bi_ref, lam_ref,
                  o_ref, tail_ref, carry_ref, *, chunk):
    w = W_BRANCH

    @pl.when(pl.program_id(1) == 0)
    def _():
        tail_ref[...] = jnp.zeros_like(tail_ref)
        carry_ref[...] = jnp.zeros_like(carry_ref)

    nt = chunk // V7X_SUBLANES
    x = x_ref[...]
    tail = tail_ref[...]
    rowc = lax.broadcasted_iota(I32, (chunk, 1), 0)
    cw = cw_ref[...]
    xc = cw[CONV_WIDTH - 1:CONV_WIDTH] * x + cb_ref[...]
    for sh in range(1, CONV_WIDTH):
        prev = jnp.broadcast_to(pltpu.roll(tail, sh, 0)[None], (nt, V7X_SUBLANES, w)).reshape(chunk, w)
        xs = jnp.where(rowc < sh, prev, pltpu.roll(x, sh, 0))
        xc = xc + cw[CONV_WIDTH - 1 - sh:CONV_WIDTH - sh] * xs
    tail_ref[...] = x[chunk - V7X_SUBLANES:chunk]
    xb = xc.astype(BF16)
    r = _sigmoid(_dot(xb, wr_ref[...]) + br_ref[...])
    i = _sigmoid(_dot(xb, wi_ref[...]) + bi_ref[...])
    log_a = (-RG_C) * r * _softplus(-lam_ref[...])
    a = jnp.exp(log_a)
    mult = jnp.sqrt(1.0 - jnp.exp(2.0 * log_a))
    b = mult * i * xc
    a3 = a.reshape(nt, V7X_SUBLANES, w)
    b3 = b.reshape(nt, V7X_SUBLANES, w)
    rowi = lax.broadcasted_iota(I32, (1, V7X_SUBLANES, 1), 1)
    for s in (1, 2, 4):
        keep = rowi >= s
        ash = jnp.where(keep, pltpu.roll(a3, s, 1), 1.0)
        bsh = jnp.where(keep, pltpu.roll(b3, s, 1), 0.0)
        b3 = b3 + a3 * bsh
        a3 = a3 * ash
    h = carry_ref[0:1, :]
    gate = _gelu(gate_ref[...])
    for t in range(nt):
        ht = b3[t] + a3[t] * h
        o_ref[t * V7X_SUBLANES:(t + 1) * V7X_SUBLANES, :] = ht * gate[t * V7X_SUBLANES:(t + 1) * V7X_SUBLANES]
        h = ht[V7X_SUBLANES - 1:V7X_SUBLANES]
    carry_ref[0:1, :] = h


def _rglru_mixer(xr, gate, conv_w, conv_b, w_r, b_r, w_i, b_i, lam, bsz, seq):
    n, w = xr.shape
    chunk = min(256, seq)
    cpb = seq // chunk
    full = lambda a: pl.BlockSpec(a.shape, lambda b, c: (0,) * a.ndim)
    row = pl.BlockSpec((chunk, w), lambda b, c: (b * cpb + c, 0))
    args = (xr, gate, conv_w, conv_b.reshape(1, w), _block_diag(w_r).astype(BF16), b_r.reshape(1, w),
            _block_diag(w_i).astype(BF16), b_i.reshape(1, w), lam.reshape(1, w))
    return pl.pallas_call(
        functools.partial(_rglru_kernel, chunk=chunk),
        grid=(bsz, cpb),
        in_specs=[row, row] + [full(a) for a in args[2:]],
        out_specs=row,
        out_shape=jax.ShapeDtypeStruct((n, w), F32),
        scratch_shapes=[pltpu.VMEM((V7X_SUBLANES, w), F32), pltpu.VMEM((V7X_SUBLANES, w), F32)],
        compiler_params=_cparams("parallel", "arbitrary"),
        name="rglru_mixer",
    )(*args)


RW_CHUNK = 64


def _segsum(x, ones_bf16):
    hi, lo = _split_bf16(x)
    return _dot(hi, ones_bf16) + _dot(lo, ones_bf16)


def _stack_heads(x, head_of_lane):
    return jnp.concatenate([jnp.where(head_of_lane == h, x, 0.0) for h in range(RW_HEADS)], axis=0)


def _rwkv_chunk_kernel(f_ref, fprev_ref, mu_ref, w0_ref, wup_ref, a0_ref, aup_ref, gup_ref, kk_ref, ka_ref,
                       rk_ref, ones_ref, rt2_ref, y1_ref, m_ref, s1_ref, pl_ref, g_ref, bonus_ref,
                       kt_s, rt_s, kib_s, bib_s, vsb_s, vst_s, kip_s, bip_s, a0_s, a1_s, a2_s, avk_s, bbk_s, bvk_s,
                       pw_s, kt2_s, kt2t_s, u0_s, u0t_s, tinv_s, *, bsz, chunk):
    w = W_BRANCH
    sl = V7X_SUBLANES
    n4 = RW_HEADS * chunk
    first = pl.program_id(0) == 0
    ones = ones_ref[...]
    rowc = lax.broadcasted_iota(I32, (chunk, 1), 0)
    head_of_lane = lax.broadcasted_iota(I32, (1, w), 1) // RW_DIM
    ri = lax.broadcasted_iota(I32, (n4, n4), 0)
    ci = lax.broadcasted_iota(I32, (n4, n4), 1)
    strict = ci < ri
    incl = ci <= ri
    same16 = (ri // 16) == (ci // 16)
    same32 = (ri // 32) == (ci // 32)
    eye = (ri == ci).astype(F32)
    tri_c = (lax.broadcasted_iota(I32, (chunk, chunk), 1)
             <= lax.broadcasted_iota(I32, (chunk, chunk), 0)).astype(BF16)

    for b in range(bsz):
        f = f_ref[b]
        tail = jnp.where(first, 0.0, fprev_ref[b])
        prev = jnp.where(rowc < 1,
                         jnp.broadcast_to(pltpu.roll(tail, 1, 0)[None], (chunk // sl, sl, RW_COLS)).reshape(chunk, RW_COLS),
                         pltpu.roll(f, 1, 0))
        f = f + (prev - f) * mu_ref[...]
        r = f[:, 0:w]
        k = f[:, w:2 * w]
        v = f[:, 2 * w:3 * w]
        wd = f[:, 3 * w:3 * w + RW_DECAY_LORA]
        ad = f[:, 3 * w + RW_DECAY_LORA:3 * w + RW_DECAY_LORA + RW_A_LORA]
        gd = f[:, 3 * w + RW_DECAY_LORA + RW_A_LORA:RW_COLS]
        wlog = -_softplus(-(w0_ref[...] + _dot(jnp.tanh(wd).astype(BF16), wup_ref[...]))) - 0.5
        logd = -jnp.exp(wlog)
        a = _sigmoid(a0_ref[...] + _dot(ad.astype(BF16), aup_ref[...]))
        g = _dot(_sigmoid(gd).astype(BF16), gup_ref[...])
        kk = k * kk_ref[...]
        kk = kk / jnp.maximum(jnp.sqrt(_segsum(kk * kk, ones)), 1e-12)
        k2 = k * (1.0 + (a - 1.0) * ka_ref[...])
        beta = kk * a
        l1 = logd.astype(BF16)
        rem = logd - l1.astype(F32)
        l2 = rem.astype(BF16)
        l3 = (rem - l2.astype(F32)).astype(BF16)
        cum = _dot(tri_c, l1) + _dot(tri_c, l2) + _dot(tri_c, l3)
        p = jnp.exp(cum)
        pinv = jnp.exp(-cum)
        kt = _stack_heads(kk * jnp.exp(cum - logd), head_of_lane).astype(BF16)
        rt = _stack_heads(r * p, head_of_lane).astype(BF16)
        ki = _stack_heads(k2 * pinv, head_of_lane)
        bi = _stack_heads(beta * pinv, head_of_lane)
        vs = _stack_heads(v, head_of_lane)
        pl_row = p[chunk - 1:chunk]
        kt_s[b] = kt
        rt_s[b] = rt
        kib_s[b] = ki.astype(BF16)
        bib_s[b] = bi.astype(BF16)
        vsb_s[b] = vs.astype(BF16)
        vst_s[b] = vs.T.astype(BF16)
        kip_s[b] = (ki * pl_row).astype(BF16)
        bip_s[b] = (bi * pl_row).astype(BF16)
        pl_ref[b, 0] = jnp.broadcast_to(pl_row, (sl, w))
        g_ref[b] = g
        bonus_ref[b] = _segsum(r * k2 * rk_ref[...], ones) * v

    for b in range(bsz):
        kt, rt, kib, bib = kt_s[b], rt_s[b], kib_s[b], bib_s[b]
        a_bk = jnp.where(strict, _dot_nt(kt, bib), 0.0)
        a0 = jnp.where(same16, a_bk, 0.0)
        a0_s[b] = a0.astype(BF16)
        a1_s[b] = jnp.where(same32 & jnp.logical_not(same16), a_bk, 0.0).astype(BF16)
        a2_s[b] = jnp.where(same32, 0.0, a_bk).astype(BF16)
        tinv_s[b] = eye - a0
        avk_s[b] = jnp.where(strict, _dot_nt(kt, kib), 0.0).astype(BF16)
        bbk_s[b] = jnp.where(incl, _dot_nt(rt, bib), 0.0).astype(BF16)
        bvk_s[b] = jnp.where(incl, _dot_nt(rt, kib), 0.0).astype(BF16)
    for b in range(bsz):
        pw_s[b] = _dot(a0_s[b], a0_s[b]).astype(BF16)
    for it in range(3):
        for b in range(bsz):
            tinv_s[b] = tinv_s[b] + _dot(tinv_s[b].astype(BF16), pw_s[b])
        if it < 2:
            for b in range(bsz):
                pw_s[b] = _dot(pw_s[b], pw_s[b]).astype(BF16)
    for off_s in (a1_s, a2_s):
        for b in range(bsz):
            pw_s[b] = _dot(tinv_s[b].astype(BF16), off_s[b]).astype(BF16)
        for b in range(bsz):
            tinv_s[b] = tinv_s[b] - _dot(pw_s[b], tinv_s[b].astype(BF16))
    for b in range(bsz):
        kt2 = _dot(tinv_s[b].astype(BF16), kt_s[b])
        kt2_s[b] = kt2.astype(BF16)
        kt2t_s[b] = kt2.T.astype(BF16)
        pw_s[b] = _dot(avk_s[b], vsb_s[b]).astype(BF16)
    for b in range(bsz):
        u0 = _dot(tinv_s[b].astype(BF16), pw_s[b])
        u0_s[b] = u0.astype(BF16)
        u0t_s[b] = u0.T.astype(BF16)
    for b in range(bsz):
        bbk = bbk_s[b]
        y1s = _dot(bvk_s[b], vsb_s[b]) - _dot(bbk, u0_s[b])
        y1 = y1s[0:chunk]
        for h in range(1, RW_HEADS):
            y1 = y1 + y1s[h * chunk:(h + 1) * chunk]
        y1_ref[b] = y1
        rt2_ref[b, 0] = (rt_s[b].astype(F32) - _dot(bbk, kt2_s[b])).astype(BF16)
        m_ref[b, 0] = _dot(kt2t_s[b], bip_s[b]).astype(BF16)
        s1_ref[b, 0] = _dot(vst_s[b], kip_s[b]) - _dot(u0t_s[b], bip_s[b])


def _rwkv_state_kernel(rt2_ref, y1_ref, m_ref, s1_ref, pl_ref, g_ref, bonus_ref, gnw_ref, gnb_ref, ones_ref,
                       o_ref, state_ref, *, bsz, chunk):
    @pl.when(pl.program_id(0) == 0)
    def _():
        state_ref[...] = jnp.zeros_like(state_ref)

    ones = ones_ref[...]
    ys_all, yc_all = [], []
    for b in range(bsz):
        st = state_ref[b]
        stb = st.astype(BF16)
        ys = _dot_nt(rt2_ref[b, 0], stb)
        y = y1_ref[b] + ys[0:chunk]
        for h in range(1, RW_HEADS):
            y = y + ys[h * chunk:(h + 1) * chunk]
        ys_all.append(y)
        state_ref[b] = st * pl_ref[b, 0][0:1] + s1_ref[b, 0] - _dot(stb, m_ref[b, 0])
    for b in range(bsz):
        yc_all.append(ys_all[b] - _segsum(ys_all[b], ones) * (1.0 / RW_DIM))
    for b in range(bsz):
        yc = yc_all[b]
        var = _segsum(yc * yc, ones) * (1.0 / RW_DIM)
        yn = yc * lax.rsqrt(var + RW_GN_EPS) * gnw_ref[...] + gnb_ref[...]
        o_ref[b] = (yn + bonus_ref[b]) * g_ref[b]


def _rwkv_mixer(feat, mu, w0, w_up, a0, a_up, g_up, k_k, k_a, r_k, gn_w, gn_b, bsz, seq):
    w = W_BRANCH
    sl = V7X_SUBLANES
    chunk = min(RW_CHUNK, seq)
    nch = seq // chunk
    n4 = RW_HEADS * chunk
    f3 = feat.reshape(bsz, seq, RW_COLS)
    row1 = lambda a: a.reshape(1, -1)
    ones = _block_diag(jnp.ones((RW_HEADS, RW_DIM, RW_DIM), F32)).astype(BF16)
    full = lambda a: pl.BlockSpec(a.shape, lambda c: (0,) * a.ndim)
    tok = lambda width: pl.BlockSpec((bsz, chunk, width), lambda c: (0, c, 0))
    per_chunk = lambda rows, cols: pl.BlockSpec((bsz, 1, rows, cols), lambda c: (0, c, 0, 0))
    a_args = (row1(mu), row1(w0), w_up.astype(BF16), row1(a0), a_up.astype(BF16), g_up.astype(BF16),
              row1(k_k), row1(k_a), row1(r_k), ones)
    mids = pl.pallas_call(
        functools.partial(_rwkv_chunk_kernel, bsz=bsz, chunk=chunk),
        grid=(nch,),
        in_specs=[tok(RW_COLS),
                  pl.BlockSpec((bsz, sl, RW_COLS), lambda c: (0, jnp.maximum(c * (chunk // sl) - 1, 0), 0))]
        + [full(a) for a in a_args],
        out_specs=[per_chunk(n4, w), tok(w), per_chunk(w, w), per_chunk(w, w), per_chunk(sl, w), tok(w), tok(w)],
        out_shape=[jax.ShapeDtypeStruct((bsz, nch, n4, w), BF16), jax.ShapeDtypeStruct((bsz, seq, w), F32),
                   jax.ShapeDtypeStruct((bsz, nch, w, w), BF16), jax.ShapeDtypeStruct((bsz, nch, w, w), F32),
                   jax.ShapeDtypeStruct((bsz, nch, sl, w), F32), jax.ShapeDtypeStruct((bsz, seq, w), F32),
                   jax.ShapeDtypeStruct((bsz, seq, w), F32)],
        scratch_shapes=[pltpu.VMEM((bsz, n4, w), BF16)] * 19 + [pltpu.VMEM((bsz, n4, w), F32)],
        compiler_params=_cparams("parallel"),
        name="rwkv_chunks",
    )(f3, f3, *a_args)
    b_args = (row1(gn_w), row1(gn_b), ones)
    out = pl.pallas_call(
        functools.partial(_rwkv_state_kernel, bsz=bsz, chunk=chunk),
        grid=(nch,),
        in_specs=[per_chunk(n4, w), tok(w), per_chunk(w, w), per_chunk(w, w), per_chunk(sl, w), tok(w), tok(w)]
        + [full(a) for a in b_args],
        out_specs=tok(w),
        out_shape=jax.ShapeDtypeStruct((bsz, seq, w), F32),
        scratch_shapes=[pltpu.VMEM((bsz, w, w), F32)],
        compiler_params=_cparams("arbitrary"),
        name="rwkv_state",
    )(*mids, *b_args)
    return out.reshape(bsz * seq, w)


DSA_TQ = 256
DSA_KC = 512


DSA_SUB = 128
DSA_VROWS = ATT_DIM + 16
KEY_NEG_INF = INT_MIN + 0x7FFFFF


def _ordered_to_f32(key):
    return lax.bitcast_convert_type(jnp.where(key >= 0, key, key ^ 0x7FFFFFFF), F32)


def _dsa_kernel(*refs, seq, tq, kc, topk):
    v_ref, vx_ref = refs[2], refs[-1]
    hd = ATT_DIM

    def prep(c, carry):
        s0 = pl.multiple_of(c * kc, kc)
        vt = v_ref[pl.ds(s0, kc), :].astype(F32).T
        for h in range(ATT_HEADS):
            vx_ref[c, h, 0:hd, :] = vt[h * hd:(h + 1) * hd].astype(BF16)
            vx_ref[c, h, hd:DSA_VROWS, :] = jnp.ones((DSA_VROWS - hd, kc), BF16)
        return carry

    lax.fori_loop(0, seq // kc, prep, 0)

    def block(i, carry):
        _dsa_block(i, *refs, seq=seq, tq=tq, kc=kc, topk=topk)
        return carry

    lax.fori_loop(0, seq // tq, block, 0)


def _dsa_block(i, q_ref, k4_ref, v_ref, iqh_ref, iql_ref, ik3_ref, iwT_ref, o_ref, sc_ref, lga_ref, lgb_ref, bias_ref,
               p_ref, acc_ref, vx_ref, *, seq, tq, kc, topk):
    nchunks = lax.div((i + 1) * tq + (kc - 1), kc)
    qpos = i * tq + lax.broadcasted_iota(I32, (1, tq), 1)
    rows = lax.broadcasted_iota(I32, (kc, 1), 0)
    wT = iwT_ref[0, i]
    q0 = pl.multiple_of(i * tq, tq)
    iqh_t = iqh_ref[pl.ds(q0, tq), :].astype(F32).T
    iql_t = iql_ref[pl.ds(q0, tq), :].astype(F32).T
    iq3 = jnp.concatenate(
        [jnp.concatenate([t[h * IDX_DIM:(h + 1) * IDX_DIM] for t in (iqh_t, iql_t, iqh_t)], axis=0)
         for h in range(IDX_HEADS)], axis=1).astype(BF16)

    def score_body(c, carry):
        s0 = pl.multiple_of(c * kc, kc)
        d = _dot(ik3_ref[pl.ds(s0, kc), :], iq3)
        sc = wT[0:1] * jnp.maximum(d[:, 0:tq], 0.0)
        for h in range(1, IDX_HEADS):
            sc = sc + wT[h:h + 1] * jnp.maximum(d[:, h * tq:(h + 1) * tq], 0.0)
        sc_ref[pl.ds(s0, kc), :] = jnp.where(s0 + rows <= qpos, sc, -jnp.inf)
        return carry

    lax.fori_loop(0, nchunks, score_body, 0)

    def count(ind_fn):
        def body(c, acc):
            s0 = pl.multiple_of(c * kc, kc)
            ind = ind_fn(sc_ref[pl.ds(s0, kc), :], s0 + rows)
            return acc + ind.reshape(kc // V7X_SUBLANES, V7X_SUBLANES, tq).sum(axis=0)
        acc = lax.fori_loop(0, nchunks, body, jnp.zeros((V7X_SUBLANES, tq), I32))
        return jnp.sum(acc, axis=0, keepdims=True)

    c0 = count(lambda s, idx: jnp.where(s >= 0.0, 1, 0))
    ans = jnp.where(c0 >= topk, 0, INT_MIN).astype(I32)

    def bit_body(j, ans):
        cand = ans | lax.shift_left(jnp.int32(1), 30 - j)
        cf = _ordered_to_f32(cand)
        cnt = count(lambda s, idx: jnp.where(s >= cf, 1, 0))
        return jnp.where(cand <= KEY_NEG_INF, cand, jnp.where(cnt >= topk, cand, ans))

    thr = _ordered_to_f32(lax.fori_loop(0, 31, bit_body, ans))
    cnt_gt = count(lambda s, idx: jnp.where(s > thr, 1, 0))
    need = (topk - cnt_gt).astype(F32)

    tri = (lax.broadcasted_iota(I32, (kc, kc), 1) <= lax.broadcasted_iota(I32, (kc, kc), 0)).astype(BF16)

    def tie_body(c, seen):
        s0 = pl.multiple_of(c * kc, kc)
        sch = sc_ref[pl.ds(s0, kc), :]
        eq = sch == thr
        rank = _dot(tri, jnp.where(eq, 1.0, 0.0).astype(BF16)) + seen
        sc_ref[pl.ds(s0, kc), :] = jnp.where(eq, jnp.where(rank > need, -jnp.inf, sch), sch)
        return rank[kc - 1:kc]

    lax.fori_loop(0, nchunks, tie_body, jnp.zeros((1, tq), F32))
    thr = jnp.maximum(thr, jnp.finfo(F32).min)

    q_all = q_ref[pl.ds(q0, tq), :].astype(F32).T.astype(BF16)
    hd = ATT_DIM

    def logits_into(buf_ref, c, live):
        s0 = pl.multiple_of(c * kc, kc)
        sch = sc_ref[pl.ds(s0, kc), :]
        dead = jnp.where(live, 0.0, NEG_BIG)
        bias_ref[...] = jnp.where(sch >= thr, dead, NEG_BIG)
        for h in range(ATT_HEADS):
            buf_ref[h] = _dot(k4_ref[pl.ds(s0, kc), h * hd:(h + 1) * hd], q_all[h * hd:(h + 1) * hd]) + bias_ref[...]

    def softmax_step(buf_ref, c, state):
        for h in range(ATT_HEADS):
            m = state[h]
            part = buf_ref[h].reshape(kc // V7X_SUBLANES, V7X_SUBLANES, tq).max(axis=0)
            mn = jnp.maximum(m, jnp.max(part, axis=0, keepdims=True))
            state[h] = mn
            p_ref[h] = jnp.exp2(buf_ref[h] - mn).astype(BF16)
            acc_ref[h] = jnp.exp2(m - mn) * acc_ref[h] + _dot(vx_ref[c, h], p_ref[h])

    last = nchunks - 1
    acc_ref[...] = jnp.zeros_like(acc_ref)
    logits_into(lga_ref, 0, True)

    def att_body(t, carry):
        state = list(carry)
        c0 = 2 * t
        c1 = jnp.minimum(c0 + 1, last)
        logits_into(lgb_ref, c1, c0 + 1 <= last)
        softmax_step(lga_ref, c0, state)
        logits_into(lga_ref, jnp.minimum(c0 + 2, last), True)
        softmax_step(lgb_ref, c1, state)
        return tuple(state)

    init = (jnp.full((1, tq), NEG_BIG, F32),) * ATT_HEADS
    lax.fori_loop(0, lax.div(nchunks + 1, 2), att_body, init)
    for h in range(ATT_HEADS):
        acc = acc_ref[h]
        o_ref[0, i, h * hd:(h + 1) * hd, :] = acc[0:hd] / acc[hd:hd + 1]


def _dsa_mixer(q, k, v, iqh, iql, ikh, ikl, iw, bsz, seq):
    w = W_BRANCH
    tq = min(DSA_TQ, seq)
    kc = min(DSA_KC, seq)
    nq = seq // tq
    nc = seq // kc
    topk = min(TOPK_MAX, seq // 4)
    assert kc >= topk and kc % DSA_SUB == 0
    ik3 = jnp.concatenate([ikh, ikh, ikl], axis=1)
    iwT = iw.reshape(bsz, nq, tq, IDX_HEADS).transpose(0, 1, 3, 2)
    per_seq = lambda width: pl.BlockSpec((seq, width), lambda b: (b, 0))
    out = pl.pallas_call(
        functools.partial(_dsa_kernel, seq=seq, tq=tq, kc=kc, topk=topk),
        grid=(bsz,),
        in_specs=[per_seq(w), per_seq(w), per_seq(w), per_seq(IDX_HEADS * IDX_DIM), per_seq(IDX_HEADS * IDX_DIM),
                  per_seq(3 * IDX_DIM), pl.BlockSpec((1, nq, IDX_HEADS, tq), lambda b: (b, 0, 0, 0))],
        out_specs=pl.BlockSpec((1, nq, w, tq), lambda b: (b, 0, 0, 0)),
        out_shape=jax.ShapeDtypeStruct((bsz, nq, w, tq), F32),
        scratch_shapes=[pltpu.VMEM((seq, tq), F32), pltpu.VMEM((ATT_HEADS, kc, tq), F32),
                        pltpu.VMEM((ATT_HEADS, kc, tq), F32), pltpu.VMEM((kc, tq), F32),
                        pltpu.VMEM((ATT_HEADS, kc, tq), BF16), pltpu.VMEM((ATT_HEADS, DSA_VROWS, tq), F32),
                        pltpu.VMEM((nc, ATT_HEADS, DSA_VROWS, kc), BF16)],
        compiler_params=_cparams("parallel"),
        name="dsa_mixer",
    )(q, k, v, iqh, iql, ik3, iwT)
    return out


def _merge_kernel(x_ref, mod_ref, g_ref, o0_ref, o1_ref, o2_ref, o3_ref, gw_ref, gb_ref, bw_ref, ow_ref, x1_ref):
    d = x_ref.shape[1]
    mod = mod_ref[0]
    x = x_ref[...]
    hb = _modulated_norm(x, g_ref[...], mod[1:2], mod[0:1]).astype(BF16)
    mixed = jnp.zeros(x.shape, F32)
    o_dsa = jnp.concatenate([o1_ref[0, j].T for j in range(o1_ref.shape[1])], axis=0)
    for n, o in enumerate((o0_ref[...], o_dsa, o2_ref[...], o3_ref[...])):
        gate = _sigmoid(_dot(hb, gw_ref[:, n * d:(n + 1) * d]) + gb_ref[:, n * d:(n + 1) * d])
        mixed = mixed + gate * _dot(o.astype(BF16), bw_ref[n])
    x1_ref[...] = x + mod[2:3] * _dot(mixed.astype(BF16), ow_ref[...])


def _merge(x2, mod_l, g, branches, gate_w, gate_b, branch_w, out_w, seq):
    n, d = x2.shape
    w = W_BRANCH
    tm = min(512, seq)
    tpb = seq // tm
    row = lambda width: pl.BlockSpec((tm, width), lambda i: (i, 0))
    full = lambda a: pl.BlockSpec(a.shape, lambda i: (0,) * a.ndim)
    args = (gate_w.astype(BF16), gate_b.reshape(1, -1), branch_w.astype(BF16), out_w.astype(BF16))
    tq = branches[1].shape[-1]
    qpt = tm // tq
    dsa_spec = pl.BlockSpec((1, qpt, w, tq), lambda i: (i // tpb, i % tpb, 0, 0))
    return pl.pallas_call(
        _merge_kernel,
        grid=(n // tm,),
        in_specs=[row(d), pl.BlockSpec((1, 6, d), lambda i: (i // tpb, 0, 0)), full(g)]
        + [row(w), dsa_spec, row(w), row(w)] + [full(a) for a in args],
        out_specs=row(d),
        out_shape=jax.ShapeDtypeStruct((n, d), F32),
        compiler_params=_cparams("parallel"),
        name="merge",
    )(x2, mod_l, g, *branches, *args)


def _route_combine(scores, rb):
    biased = scores + rb
    col = lambda a, e: a[:, e:e + 1]
    npg = EXP_PER_GROUP
    gs = []
    for g in range(N_GROUPS):
        best = None
        for j1 in range(npg):
            for j2 in range(j1 + 1, npg):
                s = col(biased, npg * g + j1) + col(biased, npg * g + j2)
                best = s if best is None else jnp.maximum(best, s)
        gs.append(best)
    bg = jnp.zeros(gs[0].shape, I32)
    bv = gs[0]
    for g in range(1, N_GROUPS):
        upd = gs[g] > bv
        bv = jnp.where(upd, gs[g], bv)
        bg = jnp.where(upd, g, bg)
    bsel, ssel = [], []
    for j in range(npg):
        bj, sj = col(biased, j), col(scores, j)
        for g in range(1, N_GROUPS):
            bj = jnp.where(bg == g, col(biased, npg * g + j), bj)
            sj = jnp.where(bg == g, col(scores, npg * g + j), sj)
        bsel.append(bj)
        ssel.append(sj)
    i1, v1, s1 = jnp.zeros(bg.shape, I32), bsel[0], ssel[0]
    for j in range(1, npg):
        upd = bsel[j] > v1
        v1 = jnp.where(upd, bsel[j], v1)
        s1 = jnp.where(upd, ssel[j], s1)
        i1 = jnp.where(upd, j, i1)
    i2, v2, s2 = jnp.zeros(bg.shape, I32), jnp.where(i1 == 0, -jnp.inf, bsel[0]), ssel[0]
    for j in range(1, npg):
        cand = jnp.where(i1 == j, -jnp.inf, bsel[j])
        upd = cand > v2
        v2 = jnp.where(upd, cand, v2)
        s2 = jnp.where(upd, ssel[j], s2)
        i2 = jnp.where(upd, j, i2)
    tot = s1 + s2
    lane = lax.broadcasted_iota(I32, scores.shape, 1)
    return (jnp.where(lane == npg * bg + i1, s1 / tot, 0.0)
            + jnp.where(lane == npg * bg + i2, s2 / tot, 0.0))


def _route_kernel(x1_ref, mod_ref, g_ref, rwh_ref, rwl_ref, rb_ref, h2_ref, comb_ref):
    mod = mod_ref[0]
    h2 = _modulated_norm(x1_ref[...], g_ref[...], mod[4:5], mod[3:4])
    h2_ref[...] = h2.astype(BF16)
    hh, hl = _split_bf16(h2)
    logits = _dot(hh, rwh_ref[...]) + _dot(hh, rwl_ref[...]) + _dot(hl, rwh_ref[...])
    comb_ref[...] = _route_combine(_sigmoid(logits[:, 0:N_EXPERTS]), rb_ref[...])


def _route(x1, mod_l, g, router_w, router_b, seq):
    n, d = x1.shape
    tm = min(512, seq)
    tpb = seq // tm
    rwh, rwl = _split_bf16(jnp.pad(router_w, ((0, 0), (0, V7X_LANES - N_EXPERTS))))
    row = lambda width: pl.BlockSpec((tm, width), lambda i: (i, 0))
    full = lambda a: pl.BlockSpec(a.shape, lambda i: (0,) * a.ndim)
    rb = router_b.reshape(1, N_EXPERTS)
    return pl.pallas_call(
        _route_kernel,
        grid=(n // tm,),
        in_specs=[row(d), pl.BlockSpec((1, 6, d), lambda i: (i // tpb, 0, 0)), full(g), full(rwh), full(rwl), full(rb)],
        out_specs=[row(d), row(N_EXPERTS)],
        out_shape=[jax.ShapeDtypeStruct((n, d), BF16), jax.ShapeDtypeStruct((n, N_EXPERTS), F32)],
        compiler_params=_cparams("parallel"),
        name="route",
    )(x1, mod_l, g, rwh, rwl, rb)


def _moe_kernel(h2_ref, c_ref, w1_ref, w3_ref, w2_ref, x1_ref, mod_ref, fg_ref, o_ref, acc_ref, *, final_norm):
    e = pl.program_id(1)

    @pl.when(e == 0)
    def _():
        acc_ref[...] = jnp.zeros_like(acc_ref)

    h2 = h2_ref[...]
    a = _dot(h2, w1_ref[0, 0].astype(BF16))
    b = _dot(h2, w3_ref[0, 0].astype(BF16))
    y = _dot((a * _sigmoid(a) * b).astype(BF16), w2_ref[0, 0].astype(BF16))
    comb = c_ref[...]
    lane = lax.broadcasted_iota(I32, comb.shape, 1)
    acc_ref[...] += jnp.sum(jnp.where(lane == e, comb, 0.0), axis=1, keepdims=True) * y

    @pl.when(e == pl.num_programs(1) - 1)
    def _():
        out = x1_ref[...] + mod_ref[0][5:6] * acc_ref[...]
        if final_norm:
            ms = jnp.mean(out * out, axis=-1, keepdims=True)
            out = out * lax.rsqrt(ms + NORM_EPS) * fg_ref[...]
        o_ref[...] = out


def _moe_dense(h2, comb, exp_w1, exp_w3, exp_w2, layer, x1, mod_l, final_g, seq, final_norm):
    n, d = x1.shape
    ne, f = exp_w1.shape[1], exp_w1.shape[3]
    tm = min(1024, seq)
    tpb = seq // tm
    row = lambda width: pl.BlockSpec((tm, width), lambda i, e: (i, 0))
    return pl.pallas_call(
        functools.partial(_moe_kernel, final_norm=final_norm),
        grid=(n // tm, ne),
        in_specs=[row(d), row(ne),
                  pl.BlockSpec((1, 1, d, f), lambda i, e: (layer, e, 0, 0)),
                  pl.BlockSpec((1, 1, d, f), lambda i, e: (layer, e, 0, 0)),
                  pl.BlockSpec((1, 1, f, d), lambda i, e: (layer, e, 0, 0)),
                  row(d), pl.BlockSpec((1, 6, d), lambda i, e: (i // tpb, 0, 0)),
                  pl.BlockSpec((1, d), lambda i, e: (0, 0))],
        out_specs=row(d),
        out_shape=jax.ShapeDtypeStruct((n, d), F32),
        scratch_shapes=[pltpu.VMEM((tm, d), F32)],
        compiler_params=_cparams("parallel", "arbitrary"),
        name="moe_dense",
    )(h2, comb, exp_w1, exp_w3, exp_w2, x1, mod_l, final_g)


def kernel(x, c, positions, ada_w, ada_b, mix_norm_g, w_in, gate_w, gate_b, branch_w, out_w, s5_lam_re, s5_lam_im, s5_log_step, s5_b_re, s5_b_im, s5_c_re, s5_c_im, s5_d, s5_glu_w, s5_glu_b, rg_conv_w, rg_conv_b, rg_wr, rg_br, rg_wi, rg_bi, rg_lam, rw_mu, rw_w0, rw_w_up, rw_a0, rw_a_up, rw_g_up, rw_k_k, rw_k_a, rw_r_k, rw_gn_w, rw_gn_b, ffn_norm_g, router_w, router_b, exp_w1, exp_w3, exp_w2, final_norm_g):
    bsz, seq, d = x.shape
    depth = ada_w.shape[0]
    n = bsz * seq
    x2 = x.reshape(n, d)
    mod = _ada_mod(c, ada_w, ada_b)
    tabs = _rope_tables(positions)
    fg = final_norm_g.reshape(1, d)
    for l in range(depth):
        mod_l = mod[l]
        g1 = mix_norm_g[l].reshape(1, d)
        (u, q, k, v, iqh, iql, ikh, ikl, iw, rgx, rgg, rwf) = _inproj(x2, mod_l, g1, w_in[l], tabs, seq)
        s5p = _s5_params(s5_lam_re[l], s5_lam_im[l], s5_log_step[l], s5_b_re[l], s5_b_im[l], s5_c_re[l], s5_c_im[l])
        o_s5 = _s5_mixer(u, s5p, s5_d[l], s5_glu_w[l], s5_glu_b[l], bsz, seq)
        o_dsa = _dsa_mixer(q, k, v, iqh, iql, ikh, ikl, iw, bsz, seq)
        o_rg = _rglru_mixer(rgx, rgg, rg_conv_w[l], rg_conv_b[l], rg_wr[l], rg_br[l], rg_wi[l], rg_bi[l],
                            rg_lam[l], bsz, seq)
        o_rw = _rwkv_mixer(rwf, rw_mu[l], rw_w0[l], rw_w_up[l], rw_a0[l], rw_a_up[l], rw_g_up[l], rw_k_k[l],
                           rw_k_a[l], rw_r_k[l], rw_gn_w[l], rw_gn_b[l], bsz, seq)
        x1 = _merge(x2, mod_l, g1, (o_s5, o_dsa, o_rg, o_rw), gate_w[l], gate_b[l], branch_w[l], out_w[l], seq)
        h2, comb = _route(x1, mod_l, ffn_norm_g[l].reshape(1, d), router_w, router_b, seq)
        x2 = _moe_dense(h2, comb, exp_w1, exp_w3, exp_w2, l, x1, mod_l, fg, seq, l == depth - 1)
    return x2.reshape(bsz, seq, d)
```

```python
import functools
import math

import jax
import jax.numpy as jnp
from jax import lax
from jax.experimental import pallas as pl
from jax.experimental.pallas import tpu as pltpu

F32 = jnp.float32
BF16 = jnp.bfloat16
I32 = jnp.int32

W_BRANCH = 256
N_BRANCH = 4
NORM_EPS = 1e-6
S5_GROUP = 16
S5_GROUPS = W_BRANCH // S5_GROUP
S5_STATE = 64
ATT_HEADS = 4
ATT_DIM = W_BRANCH // ATT_HEADS
IDX_HEADS = 4
IDX_DIM = 32
TOPK_MAX = 256
ROPE_THETA = 10000.0
RG_BLOCKS = 4
RG_C = 8.0
CONV_WIDTH = 4
RW_HEADS = 4
RW_DIM = W_BRANCH // RW_HEADS
RW_DECAY_LORA = 32
RW_A_LORA = 32
RW_GATE_LORA = 64
RW_GN_EPS = 64e-5
RW_COLS = 3 * W_BRANCH + RW_DECAY_LORA + RW_A_LORA + RW_GATE_LORA
N_EXPERTS = 16
N_GROUPS = 4
EXP_PER_GROUP = N_EXPERTS // N_GROUPS
D_EXPERT = 512

V7X_SUBLANES = 8
V7X_LANES = 128
V7X_VMEM_LIMIT_BYTES = 56 * 1024 * 1024

INT_MIN = -(2 ** 31)
NEG_BIG = -1e30


def _cparams(*sem, flags=None):
    return pltpu.CompilerParams(dimension_semantics=sem, vmem_limit_bytes=V7X_VMEM_LIMIT_BYTES, flags=flags)


def _split_bf16(x):
    hi = x.astype(BF16)
    lo = (x - hi.astype(F32)).astype(BF16)
    return hi, lo


def _dot(a, b):
    return jnp.dot(a, b, preferred_element_type=F32)


def _dot_nt(a, b):
    return lax.dot_general(a, b, (((1,), (1,)), ((), ())), preferred_element_type=F32)


def _dot_tn(a, b):
    return lax.dot_general(a, b, (((0,), (0,)), ((), ())), preferred_element_type=F32)


def _dot3(a, b):
    ah, al = _split_bf16(a)
    bh, bl = _split_bf16(b)
    return _dot(ah, bh) + _dot(ah, bl) + _dot(al, bh)


def _sigmoid(x):
    return 1.0 / (1.0 + jnp.exp(-x))


def _softplus(x):
    return jnp.maximum(x, 0.0) + jnp.log(1.0 + jnp.exp(-jnp.abs(x)))


def _gelu(x):
    c = math.sqrt(2.0 / math.pi)
    return 0.5 * x * (1.0 + jnp.tanh(c * (x + 0.044715 * (x * x * x))))


def _rot_half_cols(w, n_heads, dim):
    lead = w.shape[0]
    w4 = w.reshape(lead, n_heads, 2, dim // 2)
    return jnp.concatenate([-w4[:, :, 1:2], w4[:, :, 0:1]], axis=2).reshape(lead, n_heads * dim)


def _ada_kernel(c_ref, w_ref, b_ref, o_ref):
    c = c_ref[...]
    ca = c * _sigmoid(c)
    o_ref[0] = _dot3(ca, w_ref[0]) + b_ref[0]


def _ada_mod(c, ada_w, ada_b):
    depth, d, d6 = ada_w.shape
    bsz = c.shape[0]
    rows = -(-bsz // V7X_SUBLANES) * V7X_SUBLANES
    cp = jnp.zeros((rows, d), F32).at[:bsz].set(c)
    out = pl.pallas_call(
        _ada_kernel,
        grid=(depth, d6 // d),
        in_specs=[pl.BlockSpec((rows, d), lambda l, j: (0, 0)),
                  pl.BlockSpec((1, d, d), lambda l, j: (l, 0, j)),
                  pl.BlockSpec((1, 1, d), lambda l, j: (l, 0, j))],
        out_specs=pl.BlockSpec((1, rows, d), lambda l, j: (l, 0, j)),
        out_shape=jax.ShapeDtypeStruct((depth, rows, d6), F32),
        compiler_params=_cparams("parallel", "parallel"),
        name="ada_mod",
    )(cp, ada_w, ada_b.reshape(depth, 1, d6))
    return out[:, :bsz].reshape(depth, bsz, 6, d)


def _rope_kernel(pos_ref, fa_ref, fi_ref, ca_ref, sa_ref, ci_ref, si_ref):
    pos = pos_ref[...]
    ang_a = pos * fa_ref[...]
    ang_i = pos * fi_ref[...]
    ca_ref[...] = jnp.cos(ang_a)
    sa_ref[...] = jnp.sin(ang_a)
    ci_ref[...] = jnp.cos(ang_i)
    si_ref[...] = jnp.sin(ang_i)


def _rope_tables(positions):
    n = positions.size
    pos = positions.reshape(n, 1).astype(F32)

    def freq_row(dim):
        half = dim // 2
        inv = ROPE_THETA ** (-jnp.arange(half, dtype=F32) / half)
        return jnp.tile(inv, V7X_LANES // half).reshape(1, V7X_LANES)

    tm = min(n, 2048)
    row = pl.BlockSpec((tm, V7X_LANES), lambda i: (i, 0))
    frq = pl.BlockSpec((1, V7X_LANES), lambda i: (0, 0))
    shp = jax.ShapeDtypeStruct((n, V7X_LANES), F32)
    return pl.pallas_call(
        _rope_kernel,
        grid=(n // tm,),
        in_specs=[pl.BlockSpec((tm, 1), lambda i: (i, 0)), frq, frq],
        out_specs=[row, row, row, row],
        out_shape=[shp, shp, shp, shp],
        compiler_params=_cparams("parallel"),
        name="rope_tables",
    )(pos, freq_row(ATT_DIM), freq_row(IDX_DIM))


def _modulated_norm(x, g, scale, shift):
    ms = jnp.mean(x * x, axis=-1, keepdims=True)
    return (x * lax.rsqrt(ms + NORM_EPS)) * g * (1.0 + scale) + shift


def _inproj_kernel(x_ref, mod_ref, g_ref, wm_ref, wih_ref, wil_ref,
                   ca_ref, sa_ref, ci_ref, si_ref,
                   u_ref, q_ref, k_ref, v_ref, iqh_ref, iql_ref, ikh_ref, ikl_ref, iw_ref,
                   rgx_ref, rgg_ref, rw_ref):
    w = W_BRANCH
    mod = mod_ref[0]
    h = _modulated_norm(x_ref[...], g_ref[...], mod[1:2], mod[0:1])
    hh, hl = _split_bf16(h)
    main = _dot(hh, wm_ref[...])
    idx = _dot(hh, wih_ref[...]) + _dot(hh, wil_ref[...]) + _dot(hl, wih_ref[...])
    ca = jnp.concatenate([ca_ref[...], ca_ref[...]], axis=1)
    sa = jnp.concatenate([sa_ref[...], sa_ref[...]], axis=1)
    u_ref[...] = main[:, 0:w]
    q_ref[...] = (main[:, w:2 * w] * ca + main[:, 2 * w:3 * w] * sa).astype(BF16)
    k_ref[...] = (main[:, 3 * w:4 * w] * ca + main[:, 4 * w:5 * w] * sa).astype(BF16)
    v_ref[...] = main[:, 5 * w:6 * w].astype(BF16)
    rgx_ref[...] = main[:, 6 * w:7 * w]
    rgg_ref[...] = main[:, 7 * w:8 * w]
    rw_ref[...] = main[:, 8 * w:8 * w + RW_COLS]
    ci = ci_ref[...]
    si = si_ref[...]
    iq = idx[:, 0:128] * ci + idx[:, 128:256] * si
    ik = idx[:, 256:288] * ci[:, 0:IDX_DIM] + idx[:, 288:320] * si[:, 0:IDX_DIM]
    iqh, iql = _split_bf16(iq)
    ikh, ikl = _split_bf16(ik)
    iqh_ref[...] = iqh
    iql_ref[...] = iql
    ikh_ref[...] = ikh
    ikl_ref[...] = ikl
    iw_ref[...] = idx[:, 320:320 + IDX_HEADS]


def _inproj_weights(w_in):
    w = W_BRANCH
    o = 0
    parts = {}
    for name, width in (("u", w), ("q", w), ("k", w), ("v", w), ("iq", IDX_HEADS * IDX_DIM),
                        ("ik", IDX_DIM), ("iw", IDX_HEADS), ("rgx", w), ("rgg", w), ("rw", RW_COLS)):
        parts[name] = w_in[:, o:o + width]
        o += width
    wq = parts["q"] * (ATT_DIM ** -0.5 * math.log2(math.e))
    main = jnp.concatenate([parts["u"], wq, _rot_half_cols(wq, ATT_HEADS, ATT_DIM),
                            parts["k"], _rot_half_cols(parts["k"], ATT_HEADS, ATT_DIM),
                            parts["v"], parts["rgx"], parts["rgg"], parts["rw"]], axis=1)
    pad = (-main.shape[1]) % V7X_LANES
    main = jnp.pad(main, ((0, 0), (0, pad)))
    idx = jnp.concatenate([parts["iq"], _rot_half_cols(parts["iq"], IDX_HEADS, IDX_DIM),
                           parts["ik"], _rot_half_cols(parts["ik"], 1, IDX_DIM), parts["iw"]], axis=1)
    idx = jnp.pad(idx, ((0, 0), (0, 384 - idx.shape[1])))
    ih, il = _split_bf16(idx)
    return main.astype(BF16), ih, il


def _inproj(x2, mod_l, g, w_in, tabs, seq):
    n, d = x2.shape
    tm = min(512, seq)
    tpb = seq // tm
    wm, wih, wil = _inproj_weights(w_in)
    w = W_BRANCH
    row = lambda width: pl.BlockSpec((tm, width), lambda i: (i, 0))
    full = lambda a: pl.BlockSpec(a.shape, lambda i: (0,) * a.ndim)
    shp = lambda width, dt: jax.ShapeDtypeStruct((n, width), dt)
    tab = row(V7X_LANES)
    return pl.pallas_call(
        _inproj_kernel,
        grid=(n // tm,),
        in_specs=[row(d), pl.BlockSpec((1, 6, d), lambda i: (i // tpb, 0, 0)), full(g),
                  full(wm), full(wih), full(wil), tab, tab, tab, tab],
        out_specs=[row(w), row(w), row(w), row(w), row(128), row(128), row(IDX_DIM), row(IDX_DIM),
                   row(IDX_HEADS), row(w), row(w), row(RW_COLS)],
        out_shape=[shp(w, F32), shp(w, BF16), shp(w, BF16), shp(w, BF16), shp(128, BF16), shp(128, BF16),
                   shp(IDX_DIM, BF16), shp(IDX_DIM, BF16), shp(IDX_HEADS, F32),
                   shp(w, F32), shp(w, F32), shp(RW_COLS, F32)],
        compiler_params=_cparams("parallel"),
        name="inproj",
    )(x2, mod_l, g, wm, wih, wil, *tabs)


def _s5_prep_kernel(lre_r, lim_r, st_r, lre_c, lim_c, st_c, bre_ref, bim_ref,
                    pre_ref, pim_ref, bbre_ref, bbim_ref):
    kk = (lax.broadcasted_iota(I32, (V7X_SUBLANES, 1), 0) + 1).astype(F32)
    step = jnp.exp(st_r[...])
    mag = jnp.exp(kk * (lre_r[...] * step))
    ang = kk * (lim_r[...] * step)
    pre_ref[...] = mag * jnp.cos(ang)
    pim_ref[...] = mag * jnp.sin(ang)
    a = lre_c[...]
    b = lim_c[...]
    stc = jnp.exp(st_c[...])
    m1 = jnp.exp(a * stc)
    nr = m1 * jnp.cos(b * stc) - 1.0
    ni = m1 * jnp.sin(b * stc)
    den = a * a + b * b
    cr = (nr * a + ni * b) / den
    ci = (ni * a - nr * b) / den
    bre = bre_ref[...]
    bim = bim_ref[...]
    bbre_ref[...] = cr * bre - ci * bim
    bbim_ref[...] = cr * bim + ci * bre


def _block_diag(blocks):
    g, r, c = blocks.shape
    eye = jnp.eye(g, dtype=blocks.dtype)
    return (blocks[:, :, None, :] * eye[:, None, :, None]).reshape(g * r, g * c)


def _s5_params(lam_re, lam_im, log_step, b_re, b_im, c_re, c_im):
    g, p, c = b_re.shape
    ns = g * p
    st = jnp.repeat(log_step, p)
    rows = [a.reshape(1, ns) for a in (lam_re, lam_im, st)]
    cols = [a.reshape(ns, 1) for a in (lam_re, lam_im, st)]
    full = lambda a: pl.BlockSpec(a.shape, lambda: (0,) * a.ndim)
    args = rows + cols + [b_re.reshape(ns, c), b_im.reshape(ns, c)]
    pre, pim, bbre, bbim = pl.pallas_call(
        _s5_prep_kernel,
        in_specs=[full(a) for a in args],
        out_specs=[pl.BlockSpec((V7X_SUBLANES, ns), lambda: (0, 0))] * 2
        + [pl.BlockSpec((ns, c), lambda: (0, 0))] * 2,
        out_shape=[jax.ShapeDtypeStruct((V7X_SUBLANES, ns), F32)] * 2
        + [jax.ShapeDtypeStruct((ns, c), F32)] * 2,
        name="s5_prep",
    )(*args)
    bmat = jnp.concatenate([_block_diag(bbre.reshape(g, p, c).transpose(0, 2, 1)),
                            _block_diag(bbim.reshape(g, p, c).transpose(0, 2, 1))], axis=1)
    cmat = jnp.concatenate([_block_diag(c_re.transpose(0, 2, 1)),
                            -_block_diag(c_im.transpose(0, 2, 1))], axis=0)
    return pre, pim, bmat.astype(BF16), cmat.astype(BF16)


def _s5_kernel(u_ref, bmat_ref, cmat_ref, pre_ref, pim_ref, d_ref, gw_ref, gb_ref,
               o_ref, h_ref, carry_ref, *, chunk, ns):
    @pl.when(pl.program_id(1) == 0)
    def _():
        carry_ref[...] = jnp.zeros_like(carry_ref)

    u = u_ref[...]
    bu = _dot(u.astype(BF16), bmat_ref[...])
    nt = chunk // V7X_SUBLANES
    xre = bu[:, :ns].reshape(nt, V7X_SUBLANES, ns)
    xim = bu[:, ns:].reshape(nt, V7X_SUBLANES, ns)
    pre = pre_ref[...]
    pim = pim_ref[...]
    rowi = lax.broadcasted_iota(I32, (V7X_SUBLANES, 1), 0)
    for s in (1, 2, 4):
        cre = jnp.where(rowi >= s, pre[s - 1:s], 0.0)[None]
        cim = jnp.where(rowi >= s, pim[s - 1:s], 0.0)[None]
        sre = pltpu.roll(xre, s, 1)
        sim = pltpu.roll(xim, s, 1)
        xre, xim = xre + (cre * sre - cim * sim), xim + (cre * sim + cim * sre)
    cr = carry_ref[0:1, :]
    ci = carry_ref[1:2, :]
    for t in range(nt):
        hre = xre[t] + (pre * cr - pim * ci)
        him = xim[t] + (pre * ci + pim * cr)
        h_ref[t * V7X_SUBLANES:(t + 1) * V7X_SUBLANES, 0:ns] = hre
        h_ref[t * V7X_SUBLANES:(t + 1) * V7X_SUBLANES, ns:2 * ns] = him
        cr = hre[V7X_SUBLANES - 1:V7X_SUBLANES]
        ci = him[V7X_SUBLANES - 1:V7X_SUBLANES]
    carry_ref[0:1, :] = cr
    carry_ref[1:2, :] = ci
    y = _dot(h_ref[...].astype(BF16), cmat_ref[...]) + d_ref[...] * u
    y = _gelu(y)
    z = _dot(y.astype(BF16), gw_ref[...]) + gb_ref[...]
    o_ref[...] = y * _sigmoid(z)


def _s5_mixer(u, params, d_skip, glu_w, glu_b, bsz, seq):
    pre, pim, bmat, cmat = params
    n, w = u.shape
    ns = pre.shape[1]
    chunk = min(256, seq)
    cpb = seq // chunk
    full = lambda a: pl.BlockSpec(a.shape, lambda b, c: (0,) * a.ndim)
    row = pl.BlockSpec((chunk, w), lambda b, c: (b * cpb + c, 0))
    args = (u, bmat, cmat, pre, pim, d_skip.reshape(1, w), glu_w.astype(BF16), glu_b.reshape(1, w))
    return pl.pallas_call(
        functools.partial(_s5_kernel, chunk=chunk, ns=ns),
        grid=(bsz, cpb),
        in_specs=[row] + [full(a) for a in args[1:]],
        out_specs=row,
        out_shape=jax.ShapeDtypeStruct((n, w), F32),
        scratch_shapes=[pltpu.VMEM((chunk, 2 * ns), F32), pltpu.VMEM((V7X_SUBLANES, ns), F32)],
        compiler_params=_cparams("parallel", "arbitrary"),
        name="s5_mixer",
    )(*args)


def _rglru_kernel(x_ref, gate_ref, cw_ref, cb_ref, wr_ref, br_ref, wi_ref, bi_ref, lam_ref,
                  o_ref, tail_ref, carry_ref, *, chunk):
    w = W_BRANCH

    @pl.when(pl.program_id(1) == 0)
    def _():
        tail_ref[...] = jnp.zeros_like(tail_ref)
        carry_ref[...] = jnp.zeros_like(carry_ref)

    nt = chunk // V7X_SUBLANES
    x = x_ref[...]
    tail = tail_ref[...]
    rowc = lax.broadcasted_iota(I32, (chunk, 1), 0)
    cw = cw_ref[...]
    xc = cw[CONV_WIDTH - 1:CONV_WIDTH] * x + cb_ref[...]
    for sh in range(1, CONV_WIDTH):
        prev = jnp.broadcast_to(pltpu.roll(tail, sh, 0)[None], (nt, V7X_SUBLANES, w)).reshape(chunk, w)
        xs = jnp.where(rowc < sh, prev, pltpu.roll(x, sh, 0))
        xc = xc + cw[CONV_WIDTH - 1 - sh:CONV_WIDTH - sh] * xs
    tail_ref[...] = x[chunk - V7X_SUBLANES:chunk]
    xb = xc.astype(BF16)
    r = _sigmoid(_dot(xb, wr_ref[...]) + br_ref[...])
    i = _sigmoid(_dot(xb, wi_ref[...]) + bi_ref[...])
    log_a = (-RG_C) * r * _softplus(-lam_ref[...])
    a = jnp.exp(log_a)
    mult = jnp.sqrt(1.0 - jnp.exp(2.0 * log_a))
    b = mult * i * xc
    a3 = a.reshape(nt, V7X_SUBLANES, w)
    b3 = b.reshape(nt, V7X_SUBLANES, w)
    rowi = lax.broadcasted_iota(I32, (1, V7X_SUBLANES, 1), 1)
    for s in (1, 2, 4):
        keep = rowi >= s
        ash = jnp.where(keep, pltpu.roll(a3, s, 1), 1.0)
        bsh = jnp.where(keep, pltpu.roll(b3, s, 1), 0.0)
        b3 = b3 + a3 * bsh
        a3 = a3 * ash
    h = carry_ref[0:1, :]
    gate = _gelu(gate_ref[...])
    for t in range(nt):
        ht = b3[t] + a3[t] * h
        o_ref[t * V7X_SUBLANES:(t + 1) * V7X_SUBLANES, :] = ht * gate[t * V7X_SUBLANES:(t + 1) * V7X_SUBLANES]
        h = ht[V7X_SUBLANES - 1:V7X_SUBLANES]
    carry_ref[0:1, :] = h


def _rglru_mixer(xr, gate, conv_w, conv_b, w_r, b_r, w_i, b_i, lam, bsz, seq):
    n, w = xr.shape
    chunk = min(256, seq)
    cpb = seq // chunk
    full = lambda a: pl.BlockSpec(a.shape, lambda b, c: (0,) * a.ndim)
    row = pl.BlockSpec((chunk, w), lambda b, c: (b * cpb + c, 0))
    args = (xr, gate, conv_w, conv_b.reshape(1, w), _block_diag(w_r).astype(BF16), b_r.reshape(1, w),
            _block_diag(w_i).astype(BF16), b_i.reshape(1, w), lam.reshape(1, w))
    return pl.pallas_call(
        functools.partial(_rglru_kernel, chunk=chunk),
        grid=(bsz, cpb),
        in_specs=[row, row] + [full(a) for a in args[2:]],
        out_specs=row,
        out_shape=jax.ShapeDtypeStruct((n, w), F32),
        scratch_shapes=[pltpu.VMEM((V7X_SUBLANES, w), F32), pltpu.VMEM((V7X_SUBLANES, w), F32)],
        compiler_params=_cparams("parallel", "arbitrary"),
        name="rglru_mixer",
    )(*args)


RW_CHUNK = 64


def _segsum(x, ones_bf16):
    hi, lo = _split_bf16(x)
    return _dot(hi, ones_bf16) + _dot(lo, ones_bf16)


def _stack_heads(x, head_of_lane):
    return jnp.concatenate([jnp.where(head_of_lane == h, x, 0.0) for h in range(RW_HEADS)], axis=0)


def _rwkv_chunk_kernel(f_ref, fprev_ref, mu_ref, w0_ref, wup_ref, a0_ref, aup_ref, gup_ref, kk_ref, ka_ref,
                       rk_ref, ones_ref, rt2_ref, y1_ref, m_ref, s1_ref, pl_ref, g_ref, bonus_ref,
                       kt_s, rt_s, kib_s, bib_s, vsb_s, vst_s, kip_s, bip_s, a0_s, a1_s, a2_s, avk_s, bbk_s, bvk_s,
                       pw_s, kt2_s, kt2t_s, u0_s, u0t_s, tinv_s, *, bsz, chunk):
    w = W_BRANCH
    sl = V7X_SUBLANES
    n4 = RW_HEADS * chunk
    first = pl.program_id(0) == 0
    ones = ones_ref[...]
    rowc = lax.broadcasted_iota(I32, (chunk, 1), 0)
    head_of_lane = lax.broadcasted_iota(I32, (1, w), 1) // RW_DIM
    ri = lax.broadcasted_iota(I32, (n4, n4), 0)
    ci = lax.broadcasted_iota(I32, (n4, n4), 1)
    strict = ci < ri
    incl = ci <= ri
    same16 = (ri // 16) == (ci // 16)
    same32 = (ri // 32) == (ci // 32)
    eye = (ri == ci).astype(F32)
    tri_c = (lax.broadcasted_iota(I32, (chunk, chunk), 1)
             <= lax.broadcasted_iota(I32, (chunk, chunk), 0)).astype(BF16)

    for b in range(bsz):
        f = f_ref[b]
        tail = jnp.where(first, 0.0, fprev_ref[b])
        prev = jnp.where(rowc < 1,
                         jnp.broadcast_to(pltpu.roll(tail, 1, 0)[None], (chunk // sl, sl, RW_COLS)).reshape(chunk, RW_COLS),
                         pltpu.roll(f, 1, 0))
        f = f + (prev - f) * mu_ref[...]
        r = f[:, 0:w]
        k = f[:, w:2 * w]
        v = f[:, 2 * w:3 * w]
        wd = f[:, 3 * w:3 * w + RW_DECAY_LORA]
        ad = f[:, 3 * w + RW_DECAY_LORA:3 * w + RW_DECAY_LORA + RW_A_LORA]
        gd = f[:, 3 * w + RW_DECAY_LORA + RW_A_LORA:RW_COLS]
        wlog = -_softplus(-(w0_ref[...] + _dot(jnp.tanh(wd).astype(BF16), wup_ref[...]))) - 0.5
        logd = -jnp.exp(wlog)
        a = _sigmoid(a0_ref[...] + _dot(ad.astype(BF16), aup_ref[...]))
        g = _dot(_sigmoid(gd).astype(BF16), gup_ref[...])
        kk = k * kk_ref[...]
        kk = kk / jnp.maximum(jnp.sqrt(_segsum(kk * kk, ones)), 1e-12)
        k2 = k * (1.0 + (a - 1.0) * ka_ref[...])
        beta = kk * a
        l1 = logd.astype(BF16)
        rem = logd - l1.astype(F32)
        l2 = rem.astype(BF16)
        l3 = (rem - l2.astype(F32)).astype(BF16)
        cum = _dot(tri_c, l1) + _dot(tri_c, l2) + _dot(tri_c, l3)
        p = jnp.exp(cum)
        pinv = jnp.exp(-cum)
        kt = _stack_heads(kk * jnp.exp(cum - logd), head_of_lane).astype(BF16)
        rt = _stack_heads(r * p, head_of_lane).astype(BF16)
        ki = _stack_heads(k2 * pinv, head_of_lane)
        bi = _stack_heads(beta * pinv, head_of_lane)
        vs = _stack_heads(v, head_of_lane)
        pl_row = p[chunk - 1:chunk]
        kt_s[b] = kt
        rt_s[b] = rt
        kib_s[b] = ki.astype(BF16)
        bib_s[b] = bi.astype(BF16)
        vsb_s[b] = vs.astype(BF16)
        vst_s[b] = vs.T.astype(BF16)
        kip_s[b] = (ki * pl_row).astype(BF16)
        bip_s[b] = (bi * pl_row).astype(BF16)
        pl_ref[b, 0] = jnp.broadcast_to(pl_row, (sl, w))
        g_ref[b] = g
        bonus_ref[b] = _segsum(r * k2 * rk_ref[...], ones) * v

    for b in range(bsz):
        kt, rt, kib, bib = kt_s[b], rt_s[b], kib_s[b], bib_s[b]
        a_bk = jnp.where(strict, _dot_nt(kt, bib), 0.0)
        a0 = jnp.where(same16, a_bk, 0.0)
        a0_s[b] = a0.astype(BF16)
        a1_s[b] = jnp.where(same32 & jnp.logical_not(same16), a_bk, 0.0).astype(BF16)
        a2_s[b] = jnp.where(same32, 0.0, a_bk).astype(BF16)
        tinv_s[b] = eye - a0
        avk_s[b] = jnp.where(strict, _dot_nt(kt, kib), 0.0).astype(BF16)
        bbk_s[b] = jnp.where(incl, _dot_nt(rt, bib), 0.0).astype(BF16)
        bvk_s[b] = jnp.where(incl, _dot_nt(rt, kib), 0.0).astype(BF16)
    for b in range(bsz):
        pw_s[b] = _dot(a0_s[b], a0_s[b]).astype(BF16)
    for it in range(3):
        for b in range(bsz):
            tinv_s[b] = tinv_s[b] + _dot(tinv_s[b].astype(BF16), pw_s[b])
        if it < 2:
            for b in range(bsz):
                pw_s[b] = _dot(pw_s[b], pw_s[b]).astype(BF16)
    for off_s in (a1_s, a2_s):
        for b in range(bsz):
            pw_s[b] = _dot(tinv_s[b].astype(BF16), off_s[b]).astype(BF16)
        for b in range(bsz):
            tinv_s[b] = tinv_s[b] - _dot(pw_s[b], tinv_s[b].astype(BF16))
    for b in range(bsz):
        kt2 = _dot(tinv_s[b].astype(BF16), kt_s[b])
        kt2_s[b] = kt2.astype(BF16)
        kt2t_s[b] = kt2.T.astype(BF16)
        pw_s[b] = _dot(avk_s[b], vsb_s[b]).astype(BF16)
    for b in range(bsz):
        u0 = _dot(tinv_s[b].astype(BF16), pw_s[b])
        u0_s[b] = u0.astype(BF16)
        u0t_s[b] = u0.T.astype(BF16)
    for b in range(bsz):
        bbk = bbk_s[b]
        y1s = _dot(bvk_s[b], vsb_s[b]) - _dot(bbk, u0_s[b])
        y1 = y1s[0:chunk]
        for h in range(1, RW_HEADS):
            y1 = y1 + y1s[h * chunk:(h + 1) * chunk]
        y1_ref[b] = y1
        rt2_ref[b, 0] = (rt_s[b].astype(F32) - _dot(bbk, kt2_s[b])).astype(BF16)
        m_ref[b, 0] = _dot(kt2t_s[b], bip_s[b]).astype(BF16)
        s1_ref[b, 0] = _dot(vst_s[b], kip_s[b]) - _dot(u0t_s[b], bip_s[b])


def _rwkv_state_kernel(rt2_ref, y1_ref, m_ref, s1_ref, pl_ref, g_ref, bonus_ref, gnw_ref, gnb_ref, ones_ref,
                       o_ref, state_ref, *, bsz, chunk):
    @pl.when(pl.program_id(0) == 0)
    def _():
        state_ref[...] = jnp.zeros_like(state_ref)

    ones = ones_ref[...]
    ys_all, yc_all = [], []
    for b in range(bsz):
        st = state_ref[b]
        stb = st.astype(BF16)
        ys = _dot_nt(rt2_ref[b, 0], stb)
        y = y1_ref[b] + ys[0:chunk]
        for h in range(1, RW_HEADS):
            y = y + ys[h * chunk:(h + 1) * chunk]
        ys_all.append(y)
        state_ref[b] = st * pl_ref[b, 0][0:1] + s1_ref[b, 0] - _dot(stb, m_ref[b, 0])
    for b in range(bsz):
        yc_all.append(ys_all[b] - _segsum(ys_all[b], ones) * (1.0 / RW_DIM))
    for b in range(bsz):
        yc = yc_all[b]
        var = _segsum(yc * yc, ones) * (1.0 / RW_DIM)
        yn = yc * lax.rsqrt(var + RW_GN_EPS) * gnw_ref[...] + gnb_ref[...]
        o_ref[b] = (yn + bonus_ref[b]) * g_ref[b]


def _rwkv_mixer(feat, mu, w0, w_up, a0, a_up, g_up, k_k, k_a, r_k, gn_w, gn_b, bsz, seq):
    w = W_BRANCH
    sl = V7X_SUBLANES
    chunk = min(RW_CHUNK, seq)
    nch = seq // chunk
    n4 = RW_HEADS * chunk
    f3 = feat.reshape(bsz, seq, RW_COLS)
    row1 = lambda a: a.reshape(1, -1)
    ones = _block_diag(jnp.ones((RW_HEADS, RW_DIM, RW_DIM), F32)).astype(BF16)
    full = lambda a: pl.BlockSpec(a.shape, lambda c: (0,) * a.ndim)
    tok = lambda width: pl.BlockSpec((bsz, chunk, width), lambda c: (0, c, 0))
    per_chunk = lambda rows, cols: pl.BlockSpec((bsz, 1, rows, cols), lambda c: (0, c, 0, 0))
    a_args = (row1(mu), row1(w0), w_up.astype(BF16), row1(a0), a_up.astype(BF16), g_up.astype(BF16),
              row1(k_k), row1(k_a), row1(r_k), ones)
    mids = pl.pallas_call(
        functools.partial(_rwkv_chunk_kernel, bsz=bsz, chunk=chunk),
        grid=(nch,),
        in_specs=[tok(RW_COLS),
                  pl.BlockSpec((bsz, sl, RW_COLS), lambda c: (0, jnp.maximum(c * (chunk // sl) - 1, 0), 0))]
        + [full(a) for a in a_args],
        out_specs=[per_chunk(n4, w), tok(w), per_chunk(w, w), per_chunk(w, w), per_chunk(sl, w), tok(w), tok(w)],
        out_shape=[jax.ShapeDtypeStruct((bsz, nch, n4, w), BF16), jax.ShapeDtypeStruct((bsz, seq, w), F32),
                   jax.ShapeDtypeStruct((bsz, nch, w, w), BF16), jax.ShapeDtypeStruct((bsz, nch, w, w), F32),
                   jax.ShapeDtypeStruct((bsz, nch, sl, w), F32), jax.ShapeDtypeStruct((bsz, seq, w), F32),
                   jax.ShapeDtypeStruct((bsz, seq, w), F32)],
        scratch_shapes=[pltpu.VMEM((bsz, n4, w), BF16)] * 19 + [pltpu.VMEM((bsz, n4, w), F32)],
        compiler_params=_cparams("parallel"),
        name="rwkv_chunks",
    )(f3, f3, *a_args)
    b_args = (row1(gn_w), row1(gn_b), ones)
    out = pl.pallas_call(
        functools.partial(_rwkv_state_kernel, bsz=bsz, chunk=chunk),
        grid=(nch,),
        in_specs=[per_chunk(n4, w), tok(w), per_chunk(w, w), per_chunk(w, w), per_chunk(sl, w), tok(w), tok(w)]
        + [full(a) for a in b_args],
        out_specs=tok(w),
        out_shape=jax.ShapeDtypeStruct((bsz, seq, w), F32),
        scratch_shapes=[pltpu.VMEM((bsz, w, w), F32)],
        compiler_params=_cparams("arbitrary"),
        name="rwkv_state",
    )(*mids, *b_args)
    return out.reshape(bsz * seq, w)


DSA_TQ = 256
DSA_KC = 512


DSA_SUB = 128
DSA_VROWS = ATT_DIM + 16
KEY_NEG_INF = INT_MIN + 0x7FFFFF


def _ordered_to_f32(key):
    return lax.bitcast_convert_type(jnp.where(key >= 0, key, key ^ 0x7FFFFFFF), F32)


def _dsa_kernel(*refs, seq, tq, kc, topk):
    v_ref, vx_ref = refs[2], refs[-1]
    hd = ATT_DIM

    def prep(c, carry):
        s0 = pl.multiple_of(c * kc, kc)
        vt = v_ref[pl.ds(s0, kc), :].astype(F32).T
        for h in range(ATT_HEADS):
            vx_ref[c, h, 0:hd, :] = vt[h * hd:(h + 1) * hd].astype(BF16)
            vx_ref[c, h, hd:DSA_VROWS, :] = jnp.ones((DSA_VROWS - hd, kc), BF16)
        return carry

    lax.fori_loop(0, seq // kc, prep, 0)

    def block(i, carry):
        _dsa_block(i, *refs, seq=seq, tq=tq, kc=kc, topk=topk)
        return carry

    lax.fori_loop(0, seq // tq, block, 0)


def _dsa_block(i, q_ref, k4_ref, v_ref, iqh_ref, iql_ref, ik3_ref, iwT_ref, o_ref, sc_ref, lga_ref, lgb_ref, bias_ref,
               p_ref, acc_ref, vx_ref, *, seq, tq, kc, topk):
    nchunks = lax.div((i + 1) * tq + (kc - 1), kc)
    qpos = i * tq + lax.broadcasted_iota(I32, (1, tq), 1)
    rows = lax.broadcasted_iota(I32, (kc, 1), 0)
    wT = iwT_ref[0, i]
    q0 = pl.multiple_of(i * tq, tq)
    iqh_t = iqh_ref[pl.ds(q0, tq), :].astype(F32).T
    iql_t = iql_ref[pl.ds(q0, tq), :].astype(F32).T
    iq3 = jnp.concatenate(
        [jnp.concatenate([t[h * IDX_DIM:(h + 1) * IDX_DIM] for t in (iqh_t, iql_t, iqh_t)], axis=0)
         for h in range(IDX_HEADS)], axis=1).astype(BF16)

    def score_body(c, carry):
        s0 = pl.multiple_of(c * kc, kc)
        d = _dot(ik3_ref[pl.ds(s0, kc), :], iq3)
        sc = wT[0:1] * jnp.maximum(d[:, 0:tq], 0.0)
        for h in range(1, IDX_HEADS):
            sc = sc + wT[h:h + 1] * jnp.maximum(d[:, h * tq:(h + 1) * tq], 0.0)
        sc_ref[pl.ds(s0, kc), :] = jnp.where(s0 + rows <= qpos, sc, -jnp.inf)
        return carry

    lax.fori_loop(0, nchunks, score_body, 0)

    def count(ind_fn):
        def body(c, acc):
            s0 = pl.multiple_of(c * kc, kc)
            ind = ind_fn(sc_ref[pl.ds(s0, kc), :], s0 + rows)
            return acc + ind.reshape(kc // V7X_SUBLANES, V7X_SUBLANES, tq).sum(axis=0)
        acc = lax.fori_loop(0, nchunks, body, jnp.zeros((V7X_SUBLANES, tq), I32))
        return jnp.sum(acc, axis=0, keepdims=True)

    c0 = count(lambda s, idx: jnp.where(s >= 0.0, 1, 0))
    ans = jnp.where(c0 >= topk, 0, INT_MIN).astype(I32)

    def bit_body(j, ans):
        cand = ans | lax.shift_left(jnp.int32(1), 30 - j)
        cf = _ordered_to_f32(cand)
        cnt = count(lambda s, idx: jnp.where(s >= cf, 1, 0))
        return jnp.where(cand <= KEY_NEG_INF, cand, jnp.where(cnt >= topk, cand, ans))

    thr = _ordered_to_f32(lax.fori_loop(0, 31, bit_body, ans))
    cnt_gt = count(lambda s, idx: jnp.where(s > thr, 1, 0))
    need = (topk - cnt_gt).astype(F32)

    tri = (lax.broadcasted_iota(I32, (kc, kc), 1) <= lax.broadcasted_iota(I32, (kc, kc), 0)).astype(BF16)

    def tie_body(c, seen):
        s0 = pl.multiple_of(c * kc, kc)
        sch = sc_ref[pl.ds(s0, kc), :]
        eq = sch == thr
        rank = _dot(tri, jnp.where(eq, 1.0, 0.0).astype(BF16)) + seen
        sc_ref[pl.ds(s0, kc), :] = jnp.where(eq, jnp.where(rank > need, -jnp.inf, sch), sch)
        return rank[kc - 1:kc]

    lax.fori_loop(0, nchunks, tie_body, jnp.zeros((1, tq), F32))
    thr = jnp.maximum(thr, jnp.finfo(F32).min)

    q_all = q_ref[pl.ds(q0, tq), :].astype(F32).T.astype(BF16)
    hd = ATT_DIM

    def logits_into(buf_ref, c, live):
        s0 = pl.multiple_of(c * kc, kc)
        sch = sc_ref[pl.ds(s0, kc), :]
        dead = jnp.where(live, 0.0, NEG_BIG)
        bias_ref[...] = jnp.where(sch >= thr, dead, NEG_BIG)
        for h in range(ATT_HEADS):
            buf_ref[h] = _dot(k4_ref[pl.ds(s0, kc), h * hd:(h + 1) * hd], q_all[h * hd:(h + 1) * hd]) + bias_ref[...]

    def softmax_step(buf_ref, c, state):
        for h in range(ATT_HEADS):
            m = state[h]
            part = buf_ref[h].reshape(kc // V7X_SUBLANES, V7X_SUBLANES, tq).max(axis=0)
            mn = jnp.maximum(m, jnp.max(part, axis=0, keepdims=True))
            state[h] = mn
            p_ref[h] = jnp.exp2(buf_ref[h] - mn).astype(BF16)
            acc_ref[h] = jnp.exp2(m - mn) * acc_ref[h] + _dot(vx_ref[c, h], p_ref[h])

    last = nchunks - 1
    acc_ref[...] = jnp.zeros_like(acc_ref)
    logits_into(lga_ref, 0, True)

    def att_body(t, carry):
        state = list(carry)
        c0 = 2 * t
        c1 = jnp.minimum(c0 + 1, last)
        logits_into(lgb_ref, c1, c0 + 1 <= last)
        softmax_step(lga_ref, c0, state)
        logits_into(lga_ref, jnp.minimum(c0 + 2, last), True)
        softmax_step(lgb_ref, c1, state)
        return tuple(state)

    init = (jnp.full((1, tq), NEG_BIG, F32),) * ATT_HEADS
    lax.fori_loop(0, lax.div(nchunks + 1, 2), att_body, init)
    for h in range(ATT_HEADS):
        acc = acc_ref[h]
        o_ref[0, i, h * hd:(h + 1) * hd, :] = acc[0:hd] / acc[hd:hd + 1]


def _dsa_mixer(q, k, v, iqh, iql, ikh, ikl, iw, bsz, seq):
    w = W_BRANCH
    tq = min(DSA_TQ, seq)
    kc = min(DSA_KC, seq)
    nq = seq // tq
    nc = seq // kc
    topk = min(TOPK_MAX, seq // 4)
    assert kc >= topk and kc % DSA_SUB == 0
    ik3 = jnp.concatenate([ikh, ikh, ikl], axis=1)
    iwT = iw.reshape(bsz, nq, tq, IDX_HEADS).transpose(0, 1, 3, 2)
    per_seq = lambda width: pl.BlockSpec((seq, width), lambda b: (b, 0))
    out = pl.pallas_call(
        functools.partial(_dsa_kernel, seq=seq, tq=tq, kc=kc, topk=topk),
        grid=(bsz,),
        in_specs=[per_seq(w), per_seq(w), per_seq(w), per_seq(IDX_HEADS * IDX_DIM), per_seq(IDX_HEADS * IDX_DIM),
                  per_seq(3 * IDX_DIM), pl.BlockSpec((1, nq, IDX_HEADS, tq), lambda b: (b, 0, 0, 0))],
        out_specs=pl.BlockSpec((1, nq, w, tq), lambda b: (b, 0, 0, 0)),
        out_shape=jax.ShapeDtypeStruct((bsz, nq, w, tq), F32),
        scratch_shapes=[pltpu.VMEM((seq, tq), F32), pltpu.VMEM((ATT_HEADS, kc, tq), F32),
                        pltpu.VMEM((ATT_HEADS, kc, tq), F32), pltpu.VMEM((kc, tq), F32),
                        pltpu.VMEM((ATT_HEADS, kc, tq), BF16), pltpu.VMEM((ATT_HEADS, DSA_VROWS, tq), F32),
                        pltpu.VMEM((nc, ATT_HEADS, DSA_VROWS, kc), BF16)],
        compiler_params=_cparams("parallel"),
        name="dsa_mixer",
    )(q, k, v, iqh, iql, ik3, iwT)
    return out


def _merge_kernel(x_ref, mod_ref, g_ref, o0_ref, o1_ref, o2_ref, o3_ref, gw_ref, gb_ref, bw_ref, ow_ref, x1_ref):
    d = x_ref.shape[1]
    mod = mod_ref[0]
    x = x_ref[...]
    hb = _modulated_norm(x, g_ref[...], mod[1:2], mod[0:1]).astype(BF16)
    mixed = jnp.zeros(x.shape, F32)
    o_dsa = jnp.concatenate([o1_ref[0, j].T for j in range(o1_ref.shape[1])], axis=0)
    for n, o in enumerate((o0_ref[...], o_dsa, o2_ref[...], o3_ref[...])):
        gate = _sigmoid(_dot(hb, gw_ref[:, n * d:(n + 1) * d]) + gb_ref[:, n * d:(n + 1) * d])
        mixed = mixed + gate * _dot(o.astype(BF16), bw_ref[n])
    x1_ref[...] = x + mod[2:3] * _dot(mixed.astype(BF16), ow_ref[...])


def _merge(x2, mod_l, g, branches, gate_w, gate_b, branch_w, out_w, seq):
    n, d = x2.shape
    w = W_BRANCH
    tm = min(512, seq)
    tpb = seq // tm
    row = lambda width: pl.BlockSpec((tm, width), lambda i: (i, 0))
    full = lambda a: pl.BlockSpec(a.shape, lambda i: (0,) * a.ndim)
    args = (gate_w.astype(BF16), gate_b.reshape(1, -1), branch_w.astype(BF16), out_w.astype(BF16))
    tq = branches[1].shape[-1]
    qpt = tm // tq
    dsa_spec = pl.BlockSpec((1, qpt, w, tq), lambda i: (i // tpb, i % tpb, 0, 0))
    return pl.pallas_call(
        _merge_kernel,
        grid=(n // tm,),
        in_specs=[row(d), pl.BlockSpec((1, 6, d), lambda i: (i // tpb, 0, 0)), full(g)]
        + [row(w), dsa_spec, row(w), row(w)] + [full(a) for a in args],
        out_specs=row(d),
        out_shape=jax.ShapeDtypeStruct((n, d), F32),
        compiler_params=_cparams("parallel"),
        name="merge",
    )(x2, mod_l, g, *branches, *args)


def _route_combine(scores, rb):
    biased = scores + rb
    col = lambda a, e: a[:, e:e + 1]
    npg = EXP_PER_GROUP
    gs = []
    for g in range(N_GROUPS):
        best = None
        for j1 in range(npg):
            for j2 in range(j1 + 1, npg):
                s = col(biased, npg * g + j1) + col(biased, npg * g + j2)
                best = s if best is None else jnp.maximum(best, s)
        gs.append(best)
    bg = jnp.zeros(gs[0].shape, I32)
    bv = gs[0]
    for g in range(1, N_GROUPS):
        upd = gs[g] > bv
        bv = jnp.where(upd, gs[g], bv)
        bg = jnp.where(upd, g, bg)
    bsel, ssel = [], []
    for j in range(npg):
        bj, sj = col(biased, j), col(scores, j)
        for g in range(1, N_GROUPS):
            bj = jnp.where(bg == g, col(biased, npg * g + j), bj)
            sj = jnp.where(bg == g, col(scores, npg * g + j), sj)
        bsel.append(bj)
        ssel.append(sj)
    i1, v1, s1 = jnp.zeros(bg.shape, I32), bsel[0], ssel[0]
    for j in range(1, npg):
        upd = bsel[j] > v1
        v1 = jnp.where(upd, bsel[j], v1)
        s1 = jnp.where(upd, ssel[j], s1)
        i1 = jnp.where(upd, j, i1)
    i2, v2, s2 = jnp.zeros(bg.shape, I32), jnp.where(i1 == 0, -jnp.inf, bsel[0]), ssel[0]
    for j in range(1, npg):
        cand = jnp.where(i1 == j, -jnp.inf, bsel[j])
        upd = cand > v2
        v2 = jnp.where(upd, cand, v2)
        s2 = jnp.where(upd, ssel[j], s2)
        i2 = jnp.where(upd, j, i2)
    tot = s1 + s2
    return bg, npg * bg + i1, s1 / tot, npg * bg + i2, s2 / tot


MOE_CHUNK = 128
MOE_POS_LANE = N_EXPERTS


def _route_kernel(x1_ref, mod_ref, g_ref, rwh_ref, rwl_ref, rb_ref, tri_ref, h2_ref, slab_ref, post_ref, cnt_ref):
    mod = mod_ref[0]
    h2 = _modulated_norm(x1_ref[...], g_ref[...], mod[4:5], mod[3:4])
    h2_ref[...] = h2.astype(BF16)
    hh, hl = _split_bf16(h2)
    logits = _dot(hh, rwh_ref[...]) + _dot(hh, rwl_ref[...]) + _dot(hl, rwh_ref[...])
    bg, e1, w1, e2, w2 = _route_combine(_sigmoid(logits[:, 0:N_EXPERTS]), rb_ref[...])
    lane = lax.broadcasted_iota(I32, logits.shape, 1)
    member = jnp.where(lane == bg, 1.0, 0.0)
    upto = _dot(tri_ref[...], member.astype(BF16))
    rank = jnp.sum(member * upto, axis=1, keepdims=True)
    sizes = upto[upto.shape[0] - 1:]
    padded = jnp.floor((sizes + (MOE_CHUNK - 1.0)) * (1.0 / MOE_CHUNK)) * MOE_CHUNK
    offs = [jnp.zeros((1, 1), F32)]
    for g in range(1, N_GROUPS):
        offs.append(offs[-1] + padded[:, g - 1:g])
    off = offs[0]
    info = jnp.where(lane[0:1] < N_GROUPS, sizes, 0.0)
    for g in range(1, N_GROUPS):
        off = jnp.where(bg == g, offs[g], off)
        info = info + jnp.where(lane[0:1] == N_GROUPS + g, offs[g], 0.0)
    pos = off + rank - 1.0
    slab = (jnp.where(lane == e1, w1, 0.0) + jnp.where(lane == e2, w2, 0.0)
            + jnp.where(lane == MOE_POS_LANE, pos, 0.0))
    slab_ref[...] = slab
    post_ref[...] = slab.T[MOE_POS_LANE:MOE_POS_LANE + V7X_SUBLANES]
    cnt_ref[...] = jnp.broadcast_to(info, cnt_ref.shape)


def _route(x1, mod_l, g, router_w, router_b, seq):
    n, d = x1.shape
    tm = min(1024, seq)
    tpb = seq // tm
    sl = V7X_SUBLANES
    rwh, rwl = _split_bf16(jnp.pad(router_w, ((0, 0), (0, V7X_LANES - N_EXPERTS))))
    tri = jnp.tril(jnp.ones((tm, tm), BF16))
    row = lambda width: pl.BlockSpec((tm, width), lambda i: (i, 0))
    full = lambda a: pl.BlockSpec(a.shape, lambda i: (0,) * a.ndim)
    rb = router_b.reshape(1, N_EXPERTS)
    return pl.pallas_call(
        _route_kernel,
        grid=(n // tm,),
        in_specs=[row(d), pl.BlockSpec((1, 6, d), lambda i: (i // tpb, 0, 0)), full(g), full(rwh), full(rwl),
                  full(rb), full(tri)],
        out_specs=[row(d), row(V7X_LANES), pl.BlockSpec((sl, tm), lambda i: (0, i)),
                   pl.BlockSpec((sl, V7X_LANES), lambda i: (i, 0))],
        out_shape=[jax.ShapeDtypeStruct((n, d), BF16), jax.ShapeDtypeStruct((n, V7X_LANES), F32),
                   jax.ShapeDtypeStruct((sl, n), F32), jax.ShapeDtypeStruct((n // tm * sl, V7X_LANES), F32)],
        compiler_params=_cparams("parallel"),
        name="route",
    )(x1, mod_l, g, rwh, rwl, rb, tri)


def _moe_compact_kernel(goff_ref, gcnt_ref, h2_ref, slab_ref, post_ref, w1_ref, w3_ref, w2_ref, x1_ref, mod_ref,
                        fg_ref, o_ref, hc_ref, cc_ref, yacc_ref, *, final_norm):
    i = pl.program_id(0)
    e = pl.program_id(1)
    rows = hc_ref.shape[0]

    @pl.when(e == 0)
    def _():
        sel = jnp.where(lax.broadcasted_iota(I32, (rows, 1), 0).astype(F32) == post_ref[0:1, :], 1.0, 0.0).astype(BF16)
        hc_ref[...] = _dot(sel, h2_ref[...]).astype(BF16)
        sh, sl_ = _split_bf16(slab_ref[...])
        cc_ref[...] = _dot(sel, sh) + _dot(sel, sl_)
        yacc_ref[...] = jnp.zeros_like(yacc_ref)

    g = lax.div(e, EXP_PER_GROUP)
    off = goff_ref[i * N_GROUPS + g]
    nsteps = lax.div(gcnt_ref[i * N_GROUPS + g] + (MOE_CHUNK - 1), MOE_CHUNK)
    w1 = w1_ref[0, 0]
    w3 = w3_ref[0, 0]
    w2 = w2_ref[0, 0]

    def step(c, carry):
        r0 = pl.multiple_of(off + c * MOE_CHUNK, MOE_CHUNK)
        hg = hc_ref[pl.ds(r0, MOE_CHUNK), :]
        a = _dot(hg, w1)
        b = _dot(hg, w3)
        y = _dot((a * _sigmoid(a) * b).astype(BF16), w2)
        cg = cc_ref[pl.ds(r0, MOE_CHUNK), :]
        lane = lax.broadcasted_iota(I32, cg.shape, 1)
        yacc_ref[pl.ds(r0, MOE_CHUNK), :] += jnp.sum(jnp.where(lane == e, cg, 0.0), axis=1, keepdims=True) * y
        return carry

    lax.fori_loop(0, nsteps, step, 0)

    @pl.when(e == pl.num_programs(1) - 1)
    def _():
        pos = slab_ref[...][:, MOE_POS_LANE:MOE_POS_LANE + 1]
        back = jnp.where(lax.broadcasted_iota(I32, (1, rows), 1).astype(F32) == pos, 1.0, 0.0).astype(BF16)
        out = x1_ref[...] + mod_ref[0][5:6] * _dot(back, yacc_ref[...].astype(BF16))
        if final_norm:
            ms = jnp.mean(out * out, axis=-1, keepdims=True)
            out = out * lax.rsqrt(ms + NORM_EPS) * fg_ref[...]
        o_ref[...] = out


def _moe_compact(h2, slab, post, info, exp_w1, exp_w3, exp_w2, layer, x1, mod_l, final_g, seq, final_norm):
    n, d = x1.shape
    ne, f = exp_w1.shape[1], exp_w1.shape[3]
    tm = min(1024, seq)
    tpb = seq // tm
    rows = tm + N_GROUPS * MOE_CHUNK
    per_tile = info[::V7X_SUBLANES]
    gcnt = per_tile[:, 0:N_GROUPS].astype(I32).reshape(-1)
    goff = per_tile[:, N_GROUPS:2 * N_GROUPS].astype(I32).reshape(-1)
    row = lambda width: pl.BlockSpec((tm, width), lambda i, e, *_: (i, 0))
    return pl.pallas_call(
        functools.partial(_moe_compact_kernel, final_norm=final_norm),
        grid_spec=pltpu.PrefetchScalarGridSpec(
            num_scalar_prefetch=2,
            grid=(n // tm, ne),
            in_specs=[row(d), row(V7X_LANES), pl.BlockSpec((V7X_SUBLANES, tm), lambda i, e, *_: (0, i)),
                      pl.BlockSpec((1, 1, d, f), lambda i, e, *_: (layer, e, 0, 0)),
                      pl.BlockSpec((1, 1, d, f), lambda i, e, *_: (layer, e, 0, 0)),
                      pl.BlockSpec((1, 1, f, d), lambda i, e, *_: (layer, e, 0, 0)),
                      row(d), pl.BlockSpec((1, 6, d), lambda i, e, *_: (i // tpb, 0, 0)),
                      pl.BlockSpec((1, d), lambda i, e, *_: (0, 0))],
            out_specs=row(d),
            scratch_shapes=[pltpu.VMEM((rows, d), BF16), pltpu.VMEM((rows, V7X_LANES), F32),
                            pltpu.VMEM((rows, d), F32)]),
        out_shape=jax.ShapeDtypeStruct((n, d), F32),
        compiler_params=_cparams("parallel", "arbitrary"),
        name="moe_compact",
    )(goff, gcnt, h2, slab, post, exp_w1, exp_w3, exp_w2, x1, mod_l, final_g)


def kernel(x, c, positions, ada_w, ada_b, mix_norm_g, w_in, gate_w, gate_b, branch_w, out_w, s5_lam_re, s5_lam_im, s5_log_step, s5_b_re, s5_b_im, s5_c_re, s5_c_im, s5_d, s5_glu_w, s5_glu_b, rg_conv_w, rg_conv_b, rg_wr, rg_br, rg_wi, rg_bi, rg_lam, rw_mu, rw_w0, rw_w_up, rw_a0, rw_a_up, rw_g_up, rw_k_k, rw_k_a, rw_r_k, rw_gn_w, rw_gn_b, ffn_norm_g, router_w, router_b, exp_w1, exp_w3, exp_w2, final_norm_g):
    bsz, seq, d = x.shape
    depth = ada_w.shape[0]
    n = bsz * seq
    x2 = x.reshape(n, d)
    mod = _ada_mod(c, ada_w, ada_b)
    tabs = _rope_tables(positions)
    fg = final_norm_g.reshape(1, d)
    ew1, ew3, ew2 = exp_w1.astype(BF16), exp_w3.astype(BF16), exp_w2.astype(BF16)
    for l in range(depth):
        mod_l = mod[l]
        g1 = mix_norm_g[l].reshape(1, d)
        (u, q, k, v, iqh, iql, ikh, ikl, iw, rgx, rgg, rwf) = _inproj(x2, mod_l, g1, w_in[l], tabs, seq)
        s5p = _s5_params(s5_lam_re[l], s5_lam_im[l], s5_log_step[l], s5_b_re[l], s5_b_im[l], s5_c_re[l], s5_c_im[l])
        o_s5 = _s5_mixer(u, s5p, s5_d[l], s5_glu_w[l], s5_glu_b[l], bsz, seq)
        o_dsa = _dsa_mixer(q, k, v, iqh, iql, ikh, ikl, iw, bsz, seq)
        o_rg = _rglru_mixer(rgx, rgg, rg_conv_w[l], rg_conv_b[l], rg_wr[l], rg_br[l], rg_wi[l], rg_bi[l],
                            rg_lam[l], bsz, seq)
        o_rw = _rwkv_mixer(rwf, rw_mu[l], rw_w0[l], rw_w_up[l], rw_a0[l], rw_a_up[l], rw_g_up[l], rw_k_k[l],
                           rw_k_a[l], rw_r_k[l], rw_gn_w[l], rw_gn_b[l], bsz, seq)
        x1 = _merge(x2, mod_l, g1, (o_s5, o_dsa, o_rg, o_rw), gate_w[l], gate_b[l], branch_w[l], out_w[l], seq)
        h2, slab, post, info = _route(x1, mod_l, ffn_norm_g[l].reshape(1, d), router_w, router_b, seq)
        x2 = _moe_compact(h2, slab, post, info, ew1, ew3, ew2, l, x1, mod_l, fg, seq, l == depth - 1)
    return x2.reshape(bsz, seq, d)
```

```python
import functools
import math

import jax
import jax.numpy as jnp
from jax import lax
from jax.experimental import pallas as pl
from jax.experimental.pallas import tpu as pltpu

F32 = jnp.float32
BF16 = jnp.bfloat16
I32 = jnp.int32

W_BRANCH = 256
N_BRANCH = 4
NORM_EPS = 1e-6
S5_GROUP = 16
S5_GROUPS = W_BRANCH // S5_GROUP
S5_STATE = 64
ATT_HEADS = 4
ATT_DIM = W_BRANCH // ATT_HEADS
IDX_HEADS = 4
IDX_DIM = 32
TOPK_MAX = 256
ROPE_THETA = 10000.0
RG_BLOCKS = 4
RG_C = 8.0
CONV_WIDTH = 4
RW_HEADS = 4
RW_DIM = W_BRANCH // RW_HEADS
RW_DECAY_LORA = 32
RW_A_LORA = 32
RW_GATE_LORA = 64
RW_GN_EPS = 64e-5
RW_COLS = 3 * W_BRANCH + RW_DECAY_LORA + RW_A_LORA + RW_GATE_LORA
N_EXPERTS = 16
N_GROUPS = 4
EXP_PER_GROUP = N_EXPERTS // N_GROUPS
D_EXPERT = 512

V7X_SUBLANES = 8
V7X_LANES = 128
V7X_VMEM_LIMIT_BYTES = 56 * 1024 * 1024

INT_MIN = -(2 ** 31)
NEG_BIG = -1e30


def _cparams(*sem, flags=None):
    return pltpu.CompilerParams(dimension_semantics=sem, vmem_limit_bytes=V7X_VMEM_LIMIT_BYTES, flags=flags)


def _split_bf16(x):
    hi = x.astype(BF16)
    lo = (x - hi.astype(F32)).astype(BF16)
    return hi, lo


def _dot(a, b):
    return jnp.dot(a, b, preferred_element_type=F32)


def _dot_nt(a, b):
    return lax.dot_general(a, b, (((1,), (1,)), ((), ())), preferred_element_type=F32)


def _dot_tn(a, b):
    return lax.dot_general(a, b, (((0,), (0,)), ((), ())), preferred_element_type=F32)


def _dot3(a, b):
    ah, al = _split_bf16(a)
    bh, bl = _split_bf16(b)
    return _dot(ah, bh) + _dot(ah, bl) + _dot(al, bh)


def _sigmoid(x):
    return 1.0 / (1.0 + jnp.exp(-x))


def _softplus(x):
    return jnp.maximum(x, 0.0) + jnp.log(1.0 + jnp.exp(-jnp.abs(x)))


def _gelu(x):
    c = math.sqrt(2.0 / math.pi)
    return 0.5 * x * (1.0 + jnp.tanh(c * (x + 0.044715 * (x * x * x))))


def _rot_half_cols(w, n_heads, dim):
    lead = w.shape[0]
    w4 = w.reshape(lead, n_heads, 2, dim // 2)
    return jnp.concatenate([-w4[:, :, 1:2], w4[:, :, 0:1]], axis=2).reshape(lead, n_heads * dim)


def _ada_kernel(c_ref, w_ref, b_ref, o_ref):
    c = c_ref[...]
    ca = c * _sigmoid(c)
    o_ref[0] = _dot3(ca, w_ref[0]) + b_ref[0]


def _ada_mod(c, ada_w, ada_b):
    depth, d, d6 = ada_w.shape
    bsz = c.shape[0]
    rows = -(-bsz // V7X_SUBLANES) * V7X_SUBLANES
    cp = jnp.zeros((rows, d), F32).at[:bsz].set(c)
    out = pl.pallas_call(
        _ada_kernel,
        grid=(depth, d6 // d),
        in_specs=[pl.BlockSpec((rows, d), lambda l, j: (0, 0)),
                  pl.BlockSpec((1, d, d), lambda l, j: (l, 0, j)),
                  pl.BlockSpec((1, 1, d), lambda l, j: (l, 0, j))],
        out_specs=pl.BlockSpec((1, rows, d), lambda l, j: (l, 0, j)),
        out_shape=jax.ShapeDtypeStruct((depth, rows, d6), F32),
        compiler_params=_cparams("parallel", "parallel"),
        name="ada_mod",
    )(cp, ada_w, ada_b.reshape(depth, 1, d6))
    return out[:, :bsz].reshape(depth, bsz, 6, d)


def _rope_kernel(pos_ref, fa_ref, fi_ref, ca_ref, sa_ref, ci_ref, si_ref):
    pos = pos_ref[...]
    ang_a = pos * fa_ref[...]
    ang_i = pos * fi_ref[...]
    ca_ref[...] = jnp.cos(ang_a)
    sa_ref[...] = jnp.sin(ang_a)
    ci_ref[...] = jnp.cos(ang_i)
    si_ref[...] = jnp.sin(ang_i)


def _rope_tables(positions):
    n = positions.size
    pos = positions.reshape(n, 1).astype(F32)

    def freq_row(dim):
        half = dim // 2
        inv = ROPE_THETA ** (-jnp.arange(half, dtype=F32) / half)
        return jnp.tile(inv, V7X_LANES // half).reshape(1, V7X_LANES)

    tm = min(n, 2048)
    row = pl.BlockSpec((tm, V7X_LANES), lambda i: (i, 0))
    frq = pl.BlockSpec((1, V7X_LANES), lambda i: (0, 0))
    shp = jax.ShapeDtypeStruct((n, V7X_LANES), F32)
    return pl.pallas_call(
        _rope_kernel,
        grid=(n // tm,),
        in_specs=[pl.BlockSpec((tm, 1), lambda i: (i, 0)), frq, frq],
        out_specs=[row, row, row, row],
        out_shape=[shp, shp, shp, shp],
        compiler_params=_cparams("parallel"),
        name="rope_tables",
    )(pos, freq_row(ATT_DIM), freq_row(IDX_DIM))


def _modulated_norm(x, g, scale, shift):
    ms = jnp.mean(x * x, axis=-1, keepdims=True)
    return (x * lax.rsqrt(ms + NORM_EPS)) * g * (1.0 + scale) + shift


def _inproj_kernel(x_ref, mod_ref, g_ref, wm_ref, wih_ref, wil_ref,
                   ca_ref, sa_ref, ci_ref, si_ref,
                   u_ref, q_ref, k_ref, v_ref, iqh_ref, iql_ref, ikh_ref, ikl_ref, iw_ref,
                   rgx_ref, rgg_ref, rw_ref):
    w = W_BRANCH
    mod = mod_ref[0]
    h = _modulated_norm(x_ref[...], g_ref[...], mod[1:2], mod[0:1])
    hh, hl = _split_bf16(h)
    main = _dot(hh, wm_ref[...])
    idx = _dot(hh, wih_ref[...]) + _dot(hh, wil_ref[...]) + _dot(hl, wih_ref[...])
    ca = jnp.concatenate([ca_ref[...], ca_ref[...]], axis=1)
    sa = jnp.concatenate([sa_ref[...], sa_ref[...]], axis=1)
    u_ref[...] = main[:, 0:w]
    q_ref[...] = (main[:, w:2 * w] * ca + main[:, 2 * w:3 * w] * sa).astype(BF16)
    k_ref[...] = (main[:, 3 * w:4 * w] * ca + main[:, 4 * w:5 * w] * sa).astype(BF16)
    v_ref[...] = main[:, 5 * w:6 * w].astype(BF16)
    rgx_ref[...] = main[:, 6 * w:7 * w]
    rgg_ref[...] = main[:, 7 * w:8 * w]
    rw_ref[...] = main[:, 8 * w:8 * w + RW_COLS]
    ci = ci_ref[...]
    si = si_ref[...]
    iq = idx[:, 0:128] * ci + idx[:, 128:256] * si
    ik = idx[:, 256:288] * ci[:, 0:IDX_DIM] + idx[:, 288:320] * si[:, 0:IDX_DIM]
    iqh, iql = _split_bf16(iq)
    ikh, ikl = _split_bf16(ik)
    iqh_ref[...] = iqh
    iql_ref[...] = iql
    ikh_ref[...] = ikh
    ikl_ref[...] = ikl
    iw_ref[...] = idx[:, 320:320 + IDX_HEADS]


def _inproj_weights(w_in):
    w = W_BRANCH
    o = 0
    parts = {}
    for name, width in (("u", w), ("q", w), ("k", w), ("v", w), ("iq", IDX_HEADS * IDX_DIM),
                        ("ik", IDX_DIM), ("iw", IDX_HEADS), ("rgx", w), ("rgg", w), ("rw", RW_COLS)):
        parts[name] = w_in[:, o:o + width]
        o += width
    wq = parts["q"] * (ATT_DIM ** -0.5 * math.log2(math.e))
    main = jnp.concatenate([parts["u"], wq, _rot_half_cols(wq, ATT_HEADS, ATT_DIM),
                            parts["k"], _rot_half_cols(parts["k"], ATT_HEADS, ATT_DIM),
                            parts["v"], parts["rgx"], parts["rgg"], parts["rw"]], axis=1)
    pad = (-main.shape[1]) % V7X_LANES
    main = jnp.pad(main, ((0, 0), (0, pad)))
    idx = jnp.concatenate([parts["iq"], _rot_half_cols(parts["iq"], IDX_HEADS, IDX_DIM),
                           parts["ik"], _rot_half_cols(parts["ik"], 1, IDX_DIM), parts["iw"]], axis=1)
    idx = jnp.pad(idx, ((0, 0), (0, 384 - idx.shape[1])))
    ih, il = _split_bf16(idx)
    return main.astype(BF16), ih, il


def _inproj(x2, mod_l, g, w_in, tabs, seq):
    n, d = x2.shape
    tm = min(512, seq)
    tpb = seq // tm
    wm, wih, wil = _inproj_weights(w_in)
    w = W_BRANCH
    row = lambda width: pl.BlockSpec((tm, width), lambda i: (i, 0))
    full = lambda a: pl.BlockSpec(a.shape, lambda i: (0,) * a.ndim)
    shp = lambda width, dt: jax.ShapeDtypeStruct((n, width), dt)
    tab = row(V7X_LANES)
    return pl.pallas_call(
        _inproj_kernel,
        grid=(n // tm,),
        in_specs=[row(d), pl.BlockSpec((1, 6, d), lambda i: (i // tpb, 0, 0)), full(g),
                  full(wm), full(wih), full(wil), tab, tab, tab, tab],
        out_specs=[row(w), row(w), row(w), row(w), row(128), row(128), row(IDX_DIM), row(IDX_DIM),
                   row(IDX_HEADS), row(w), row(w), row(RW_COLS)],
        out_shape=[shp(w, F32), shp(w, BF16), shp(w, BF16), shp(w, BF16), shp(128, BF16), shp(128, BF16),
                   shp(IDX_DIM, BF16), shp(IDX_DIM, BF16), shp(IDX_HEADS, F32),
                   shp(w, F32), shp(w, F32), shp(RW_COLS, F32)],
        compiler_params=_cparams("parallel"),
        name="inproj",
    )(x2, mod_l, g, wm, wih, wil, *tabs)


def _s5_prep_kernel(lre_r, lim_r, st_r, lre_c, lim_c, st_c, bre_ref, bim_ref,
                    pre_ref, pim_ref, bbre_ref, bbim_ref):
    kk = (lax.broadcasted_iota(I32, (V7X_SUBLANES, 1), 0) + 1).astype(F32)
    step = jnp.exp(st_r[...])
    mag = jnp.exp(kk * (lre_r[...] * step))
    ang = kk * (lim_r[...] * step)
    pre_ref[...] = mag * jnp.cos(ang)
    pim_ref[...] = mag * jnp.sin(ang)
    a = lre_c[...]
    b = lim_c[...]
    stc = jnp.exp(st_c[...])
    m1 = jnp.exp(a * stc)
    nr = m1 * jnp.cos(b * stc) - 1.0
    ni = m1 * jnp.sin(b * stc)
    den = a * a + b * b
    cr = (nr * a + ni * b) / den
    ci = (ni * a - nr * b) / den
    bre = bre_ref[...]
    bim = bim_ref[...]
    bbre_ref[...] = cr * bre - ci * bim
    bbim_ref[...] = cr * bim + ci * bre


def _block_diag(blocks):
    g, r, c = blocks.shape
    eye = jnp.eye(g, dtype=blocks.dtype)
    return (blocks[:, :, None, :] * eye[:, None, :, None]).reshape(g * r, g * c)


def _s5_params(lam_re, lam_im, log_step, b_re, b_im, c_re, c_im):
    g, p, c = b_re.shape
    ns = g * p
    st = jnp.repeat(log_step, p)
    rows = [a.reshape(1, ns) for a in (lam_re, lam_im, st)]
    cols = [a.reshape(ns, 1) for a in (lam_re, lam_im, st)]
    full = lambda a: pl.BlockSpec(a.shape, lambda: (0,) * a.ndim)
    args = rows + cols + [b_re.reshape(ns, c), b_im.reshape(ns, c)]
    pre, pim, bbre, bbim = pl.pallas_call(
        _s5_prep_kernel,
        in_specs=[full(a) for a in args],
        out_specs=[pl.BlockSpec((V7X_SUBLANES, ns), lambda: (0, 0))] * 2
        + [pl.BlockSpec((ns, c), lambda: (0, 0))] * 2,
        out_shape=[jax.ShapeDtypeStruct((V7X_SUBLANES, ns), F32)] * 2
        + [jax.ShapeDtypeStruct((ns, c), F32)] * 2,
        name="s5_prep",
    )(*args)
    bmat = jnp.concatenate([_block_diag(bbre.reshape(g, p, c).transpose(0, 2, 1)),
                            _block_diag(bbim.reshape(g, p, c).transpose(0, 2, 1))], axis=1)
    cmat = jnp.concatenate([_block_diag(c_re.transpose(0, 2, 1)),
                            -_block_diag(c_im.transpose(0, 2, 1))], axis=0)
    return pre, pim, bmat.astype(BF16), cmat.astype(BF16)


def _s5_kernel(u_ref, bmat_ref, cmat_ref, pre_ref, pim_ref, d_ref, gw_ref, gb_ref,
               o_ref, h_ref, carry_ref, *, chunk, ns):
    @pl.when(pl.program_id(1) == 0)
    def _():
        carry_ref[...] = jnp.zeros_like(carry_ref)

    u = u_ref[...]
    bu = _dot(u.astype(BF16), bmat_ref[...])
    nt = chunk // V7X_SUBLANES
    xre = bu[:, :ns].reshape(nt, V7X_SUBLANES, ns)
    xim = bu[:, ns:].reshape(nt, V7X_SUBLANES, ns)
    pre = pre_ref[...]
    pim = pim_ref[...]
    rowi = lax.broadcasted_iota(I32, (V7X_SUBLANES, 1), 0)
    for s in (1, 2, 4):
        cre = jnp.where(rowi >= s, pre[s - 1:s], 0.0)[None]
        cim = jnp.where(rowi >= s, pim[s - 1:s], 0.0)[None]
        sre = pltpu.roll(xre, s, 1)
        sim = pltpu.roll(xim, s, 1)
        xre, xim = xre + (cre * sre - cim * sim), xim + (cre * sim + cim * sre)
    cr = carry_ref[0:1, :]
    ci = carry_ref[1:2, :]
    for t in range(nt):
        hre = xre[t] + (pre * cr - pim * ci)
        him = xim[t] + (pre * ci + pim * cr)
        h_ref[t * V7X_SUBLANES:(t + 1) * V7X_SUBLANES, 0:ns] = hre
        h_ref[t * V7X_SUBLANES:(t + 1) * V7X_SUBLANES, ns:2 * ns] = him
        cr = hre[V7X_SUBLANES - 1:V7X_SUBLANES]
        ci = him[V7X_SUBLANES - 1:V7X_SUBLANES]
    carry_ref[0:1, :] = cr
    carry_ref[1:2, :] = ci
    y = _dot(h_ref[...].astype(BF16), cmat_ref[...]) + d_ref[...] * u
    y = _gelu(y)
    z = _dot(y.astype(BF16), gw_ref[...]) + gb_ref[...]
    o_ref[...] = y * _sigmoid(z)


def _s5_mixer(u, params, d_skip, glu_w, glu_b, bsz, seq):
    pre, pim, bmat, cmat = params
    n, w = u.shape
    ns = pre.shape[1]
    chunk = min(256, seq)
    cpb = seq // chunk
    full = lambda a: pl.BlockSpec(a.shape, lambda b, c: (0,) * a.ndim)
    row = pl.BlockSpec((chunk, w), lambda b, c: (b * cpb + c, 0))
    args = (u, bmat, cmat, pre, pim, d_skip.reshape(1, w), glu_w.astype(BF16), glu_b.reshape(1, w))
    return pl.pallas_call(
        functools.partial(_s5_kernel, chunk=chunk, ns=ns),
        grid=(bsz, cpb),
        in_specs=[row] + [full(a) for a in args[1:]],
        out_specs=row,
        out_shape=jax.ShapeDtypeStruct((n, w), F32),
        scratch_shapes=[pltpu.VMEM((chunk, 2 * ns), F32), pltpu.VMEM((V7X_SUBLANES, ns), F32)],
        compiler_params=_cparams("parallel", "arbitrary"),
        name="s5_mixer",
    )(*args)


def _rglru_kernel(x_ref, gate_ref, cw_ref, cb_ref, wr_ref, br_ref, wi_ref, bi_ref, lam_ref,
                  o_ref, tail_ref, carry_ref, *, chunk):
    w = W_BRANCH

    @pl.when(pl.program_id(1) == 0)
    def _():
        tail_ref[...] = jnp.zeros_like(tail_ref)
        carry_ref[...] = jnp.zeros_like(carry_ref)

    nt = chunk // V7X_SUBLANES
    x = x_ref[...]
    tail = tail_ref[...]
    rowc = lax.broadcasted_iota(I32, (chunk, 1), 0)
    cw = cw_ref[...]
    xc = cw[CONV_WIDTH - 1:CONV_WIDTH] * x + cb_ref[...]
    for sh in range(1, CONV_WIDTH):
        prev = jnp.broadcast_to(pltpu.roll(tail, sh, 0)[None], (nt, V7X_SUBLANES, w)).reshape(chunk, w)
        xs = jnp.where(rowc < sh, prev, pltpu.roll(x, sh, 0))
        xc = xc + cw[CONV_WIDTH - 1 - sh:CONV_WIDTH - sh] * xs
    tail_ref[...] = x[chunk - V7X_SUBLANES:chunk]
    xb = xc.astype(BF16)
    r = _sigmoid(_dot(xb, wr_ref[...]) + br_ref[...])
    i = _sigmoid(_dot(xb, wi_ref[...]) + bi_ref[...])
    log_a = (-RG_C) * r * _softplus(-lam_ref[...])
    a = jnp.exp(log_a)
    mult = jnp.sqrt(1.0 - jnp.exp(2.0 * log_a))
    b = mult * i * xc
    a3 = a.reshape(nt, V7X_SUBLANES, w)
    b3 = b.reshape(nt, V7X_SUBLANES, w)
    rowi = lax.broadcasted_iota(I32, (1, V7X_SUBLANES, 1), 1)
    for s in (1, 2, 4):
        keep = rowi >= s
        ash = jnp.where(keep, pltpu.roll(a3, s, 1), 1.0)
        bsh = jnp.where(keep, pltpu.roll(b3, s, 1), 0.0)
        b3 = b3 + a3 * bsh
        a3 = a3 * ash
    h = carry_ref[0:1, :]
    gate = _gelu(gate_ref[...])
    for t in range(nt):
        ht = b3[t] + a3[t] * h
        o_ref[t * V7X_SUBLANES:(t + 1) * V7X_SUBLANES, :] = ht * gate[t * V7X_SUBLANES:(t + 1) * V7X_SUBLANES]
        h = ht[V7X_SUBLANES - 1:V7X_SUBLANES]
    carry_ref[0:1, :] = h


def _rglru_mixer(xr, gate, conv_w, conv_b, w_r, b_r, w_i, b_i, lam, bsz, seq):
    n, w = xr.shape
    chunk = min(256, seq)
    cpb = seq // chunk
    full = lambda a: pl.BlockSpec(a.shape, lambda b, c: (0,) * a.ndim)
    row = pl.BlockSpec((chunk, w), lambda b, c: (b * cpb + c, 0))
    args = (xr, gate, conv_w, conv_b.reshape(1, w), _block_diag(w_r).astype(BF16), b_r.reshape(1, w),
            _block_diag(w_i).astype(BF16), b_i.reshape(1, w), lam.reshape(1, w))
    return pl.pallas_call(
        functools.partial(_rglru_kernel, chunk=chunk),
        grid=(bsz, cpb),
        in_specs=[row, row] + [full(a) for a in args[2:]],
        out_specs=row,
        out_shape=jax.ShapeDtypeStruct((n, w), F32),
        scratch_shapes=[pltpu.VMEM((V7X_SUBLANES, w), F32), pltpu.VMEM((V7X_SUBLANES, w), F32)],
        compiler_params=_cparams("parallel", "arbitrary"),
        name="rglru_mixer",
    )(*args)


RW_CHUNK = 64


def _segsum(x, ones_bf16):
    hi, lo = _split_bf16(x)
    return _dot(hi, ones_bf16) + _dot(lo, ones_bf16)


def _stack_heads(x, head_of_lane):
    return jnp.concatenate([jnp.where(head_of_lane == h, x, 0.0) for h in range(RW_HEADS)], axis=0)


def _rwkv_chunk_kernel(f_ref, fprev_ref, mu_ref, w0_ref, wup_ref, a0_ref, aup_ref, gup_ref, kk_ref, ka_ref,
                       rk_ref, ones_ref, rt2_ref, y1_ref, m_ref, s1_ref, pl_ref, g_ref, bonus_ref,
                       kt_s, rt_s, kib_s, bib_s, vsb_s, vst_s, kip_s, bip_s, a0_s, a1_s, a2_s, avk_s, bbk_s, bvk_s,
                       pw_s, kt2_s, kt2t_s, u0_s, u0t_s, tinv_s, *, bsz, chunk):
    w = W_BRANCH
    sl = V7X_SUBLANES
    n4 = RW_HEADS * chunk
    first = pl.program_id(0) == 0
    ones = ones_ref[...]
    rowc = lax.broadcasted_iota(I32, (chunk, 1), 0)
    head_of_lane = lax.broadcasted_iota(I32, (1, w), 1) // RW_DIM
    ri = lax.broadcasted_iota(I32, (n4, n4), 0)
    ci = lax.broadcasted_iota(I32, (n4, n4), 1)
    strict = ci < ri
    incl = ci <= ri
    same16 = (ri // 16) == (ci // 16)
    same32 = (ri // 32) == (ci // 32)
    eye = (ri == ci).astype(F32)
    tri_c = (lax.broadcasted_iota(I32, (chunk, chunk), 1)
             <= lax.broadcasted_iota(I32, (chunk, chunk), 0)).astype(BF16)

    for b in range(bsz):
        f = f_ref[b]
        tail = jnp.where(first, 0.0, fprev_ref[b])
        prev = jnp.where(rowc < 1,
                         jnp.broadcast_to(pltpu.roll(tail, 1, 0)[None], (chunk // sl, sl, RW_COLS)).reshape(chunk, RW_COLS),
                         pltpu.roll(f, 1, 0))
        f = f + (prev - f) * mu_ref[...]
        r = f[:, 0:w]
        k = f[:, w:2 * w]
        v = f[:, 2 * w:3 * w]
        wd = f[:, 3 * w:3 * w + RW_DECAY_LORA]
        ad = f[:, 3 * w + RW_DECAY_LORA:3 * w + RW_DECAY_LORA + RW_A_LORA]
        gd = f[:, 3 * w + RW_DECAY_LORA + RW_A_LORA:RW_COLS]
        wlog = -_softplus(-(w0_ref[...] + _dot(jnp.tanh(wd).astype(BF16), wup_ref[...]))) - 0.5
        logd = -jnp.exp(wlog)
        a = _sigmoid(a0_ref[...] + _dot(ad.astype(BF16), aup_ref[...]))
        g = _dot(_sigmoid(gd).astype(BF16), gup_ref[...])
        kk = k * kk_ref[...]
        kk = kk / jnp.maximum(jnp.sqrt(_segsum(kk * kk, ones)), 1e-12)
        k2 = k * (1.0 + (a - 1.0) * ka_ref[...])
        beta = kk * a
        l1 = logd.astype(BF16)
        rem = logd - l1.astype(F32)
        l2 = rem.astype(BF16)
        l3 = (rem - l2.astype(F32)).astype(BF16)
        cum = _dot(tri_c, l1) + _dot(tri_c, l2) + _dot(tri_c, l3)
        p = jnp.exp(cum)
        pinv = jnp.exp(-cum)
        kt = _stack_heads(kk * jnp.exp(cum - logd), head_of_lane).astype(BF16)
        rt = _stack_heads(r * p, head_of_lane).astype(BF16)
        ki = _stack_heads(k2 * pinv, head_of_lane)
        bi = _stack_heads(beta * pinv, head_of_lane)
        vs = _stack_heads(v, head_of_lane)
        pl_row = p[chunk - 1:chunk]
        kt_s[b] = kt
        rt_s[b] = rt
        kib_s[b] = ki.astype(BF16)
        bib_s[b] = bi.astype(BF16)
        vsb_s[b] = vs.astype(BF16)
        vst_s[b] = vs.T.astype(BF16)
        kip_s[b] = (ki * pl_row).astype(BF16)
        bip_s[b] = (bi * pl_row).astype(BF16)
        pl_ref[b, 0] = jnp.broadcast_to(pl_row, (sl, w))
        g_ref[b] = g
        bonus_ref[b] = _segsum(r * k2 * rk_ref[...], ones) * v

    for b in range(bsz):
        kt, rt, kib, bib = kt_s[b], rt_s[b], kib_s[b], bib_s[b]
        a_bk = jnp.where(strict, _dot_nt(kt, bib), 0.0)
        a0 = jnp.where(same16, a_bk, 0.0)
        a0_s[b] = a0.astype(BF16)
        a1_s[b] = jnp.where(same32 & jnp.logical_not(same16), a_bk, 0.0).astype(BF16)
        a2_s[b] = jnp.where(same32, 0.0, a_bk).astype(BF16)
        tinv_s[b] = eye - a0
        avk_s[b] = jnp.where(strict, _dot_nt(kt, kib), 0.0).astype(BF16)
        bbk_s[b] = jnp.where(incl, _dot_nt(rt, bib), 0.0).astype(BF16)
        bvk_s[b] = jnp.where(incl, _dot_nt(rt, kib), 0.0).astype(BF16)
    for b in range(bsz):
        pw_s[b] = _dot(a0_s[b], a0_s[b]).astype(BF16)
    for it in range(3):
        for b in range(bsz):
            tinv_s[b] = tinv_s[b] + _dot(tinv_s[b].astype(BF16), pw_s[b])
        if it < 2:
            for b in range(bsz):
                pw_s[b] = _dot(pw_s[b], pw_s[b]).astype(BF16)
    for off_s in (a1_s, a2_s):
        for b in range(bsz):
            pw_s[b] = _dot(tinv_s[b].astype(BF16), off_s[b]).astype(BF16)
        for b in range(bsz):
            tinv_s[b] = tinv_s[b] - _dot(pw_s[b], tinv_s[b].astype(BF16))
    for b in range(bsz):
        kt2 = _dot(tinv_s[b].astype(BF16), kt_s[b])
        kt2_s[b] = kt2.astype(BF16)
        kt2t_s[b] = kt2.T.astype(BF16)
        pw_s[b] = _dot(avk_s[b], vsb_s[b]).astype(BF16)
    for b in range(bsz):
        u0 = _dot(tinv_s[b].astype(BF16), pw_s[b])
        u0_s[b] = u0.astype(BF16)
        u0t_s[b] = u0.T.astype(BF16)
    for b in range(bsz):
        bbk = bbk_s[b]
        y1s = _dot(bvk_s[b], vsb_s[b]) - _dot(bbk, u0_s[b])
        y1 = y1s[0:chunk]
        for h in range(1, RW_HEADS):
            y1 = y1 + y1s[h * chunk:(h + 1) * chunk]
        y1_ref[b] = y1
        rt2_ref[b, 0] = (rt_s[b].astype(F32) - _dot(bbk, kt2_s[b])).astype(BF16)
        m_ref[b, 0] = _dot(kt2t_s[b], bip_s[b]).astype(BF16)
        s1_ref[b, 0] = _dot(vst_s[b], kip_s[b]) - _dot(u0t_s[b], bip_s[b])


def _rwkv_state_kernel(rt2_ref, y1_ref, m_ref, s1_ref, pl_ref, g_ref, bonus_ref, gnw_ref, gnb_ref, ones_ref,
                       o_ref, state_ref, *, bsz, chunk):
    @pl.when(pl.program_id(0) == 0)
    def _():
        state_ref[...] = jnp.zeros_like(state_ref)

    ones = ones_ref[...]
    ys_all, yc_all = [], []
    for b in range(bsz):
        st = state_ref[b]
        stb = st.astype(BF16)
        ys = _dot_nt(rt2_ref[b, 0], stb)
        y = y1_ref[b] + ys[0:chunk]
        for h in range(1, RW_HEADS):
            y = y + ys[h * chunk:(h + 1) * chunk]
        ys_all.append(y)
        state_ref[b] = st * pl_ref[b, 0][0:1] + s1_ref[b, 0] - _dot(stb, m_ref[b, 0])
    for b in range(bsz):
        yc_all.append(ys_all[b] - _segsum(ys_all[b], ones) * (1.0 / RW_DIM))
    for b in range(bsz):
        yc = yc_all[b]
        var = _segsum(yc * yc, ones) * (1.0 / RW_DIM)
        yn = yc * lax.rsqrt(var + RW_GN_EPS) * gnw_ref[...] + gnb_ref[...]
        o_ref[b] = (yn + bonus_ref[b]) * g_ref[b]


def _rwkv_mixer(feat, mu, w0, w_up, a0, a_up, g_up, k_k, k_a, r_k, gn_w, gn_b, bsz, seq):
    w = W_BRANCH
    sl = V7X_SUBLANES
    chunk = min(RW_CHUNK, seq)
    nch = seq // chunk
    n4 = RW_HEADS * chunk
    f3 = feat.reshape(bsz, seq, RW_COLS)
    row1 = lambda a: a.reshape(1, -1)
    ones = _block_diag(jnp.ones((RW_HEADS, RW_DIM, RW_DIM), F32)).astype(BF16)
    full = lambda a: pl.BlockSpec(a.shape, lambda c: (0,) * a.ndim)
    tok = lambda width: pl.BlockSpec((bsz, chunk, width), lambda c: (0, c, 0))
    per_chunk = lambda rows, cols: pl.BlockSpec((bsz, 1, rows, cols), lambda c: (0, c, 0, 0))
    a_args = (row1(mu), row1(w0), w_up.astype(BF16), row1(a0), a_up.astype(BF16), g_up.astype(BF16),
              row1(k_k), row1(k_a), row1(r_k), ones)
    mids = pl.pallas_call(
        functools.partial(_rwkv_chunk_kernel, bsz=bsz, chunk=chunk),
        grid=(nch,),
        in_specs=[tok(RW_COLS),
                  pl.BlockSpec((bsz, sl, RW_COLS), lambda c: (0, jnp.maximum(c * (chunk // sl) - 1, 0), 0))]
        + [full(a) for a in a_args],
        out_specs=[per_chunk(n4, w), tok(w), per_chunk(w, w), per_chunk(w, w), per_chunk(sl, w), tok(w), tok(w)],
        out_shape=[jax.ShapeDtypeStruct((bsz, nch, n4, w), BF16), jax.ShapeDtypeStruct((bsz, seq, w), F32),
                   jax.ShapeDtypeStruct((bsz, nch, w, w), BF16), jax.ShapeDtypeStruct((bsz, nch, w, w), F32),
                   jax.ShapeDtypeStruct((bsz, nch, sl, w), F32), jax.ShapeDtypeStruct((bsz, seq, w), F32),
                   jax.ShapeDtypeStruct((bsz, seq, w), F32)],
        scratch_shapes=[pltpu.VMEM((bsz, n4, w), BF16)] * 19 + [pltpu.VMEM((bsz, n4, w), F32)],
        compiler_params=_cparams("parallel"),
        name="rwkv_chunks",
    )(f3, f3, *a_args)
    b_args = (row1(gn_w), row1(gn_b), ones)
    out = pl.pallas_call(
        functools.partial(_rwkv_state_kernel, bsz=bsz, chunk=chunk),
        grid=(nch,),
        in_specs=[per_chunk(n4, w), tok(w), per_chunk(w, w), per_chunk(w, w), per_chunk(sl, w), tok(w), tok(w)]
        + [full(a) for a in b_args],
        out_specs=tok(w),
        out_shape=jax.ShapeDtypeStruct((bsz, seq, w), F32),
        scratch_shapes=[pltpu.VMEM((bsz, w, w), F32)],
        compiler_params=_cparams("arbitrary"),
        name="rwkv_state",
    )(*mids, *b_args)
    return out.reshape(bsz * seq, w)


DSA_TQ = 256
DSA_KC = 512


DSA_SUB = 128
DSA_VROWS = ATT_DIM + 16
KEY_NEG_INF = INT_MIN + 0x7FFFFF


def _ordered_to_f32(key):
    return lax.bitcast_convert_type(jnp.where(key >= 0, key, key ^ 0x7FFFFFFF), F32)


def _dsa_kernel(*refs, seq, tq, kc, topk):
    v_ref, vx_ref = refs[2], refs[-1]
    hd = ATT_DIM

    def prep(c, carry):
        s0 = pl.multiple_of(c * kc, kc)
        vt = v_ref[pl.ds(s0, kc), :].astype(F32).T
        for h in range(ATT_HEADS):
            vx_ref[c, h, 0:hd, :] = vt[h * hd:(h + 1) * hd].astype(BF16)
            vx_ref[c, h, hd:DSA_VROWS, :] = jnp.ones((DSA_VROWS - hd, kc), BF16)
        return carry

    lax.fori_loop(0, seq // kc, prep, 0)

    def block(i, carry):
        _dsa_block(i, *refs, seq=seq, tq=tq, kc=kc, topk=topk)
        return carry

    lax.fori_loop(0, seq // tq, block, 0)


def _dsa_block(i, q_ref, k4_ref, v_ref, iqh_ref, iql_ref, ik3_ref, iwT_ref, o_ref, sc_ref, lga_ref, lgb_ref, bias_ref,
               p_ref, acc_ref, vx_ref, *, seq, tq, kc, topk):
    nchunks = lax.div((i + 1) * tq + (kc - 1), kc)
    qpos = i * tq + lax.broadcasted_iota(I32, (1, tq), 1)
    rows = lax.broadcasted_iota(I32, (kc, 1), 0)
    wT = iwT_ref[0, i]
    q0 = pl.multiple_of(i * tq, tq)
    iqh_t = iqh_ref[pl.ds(q0, tq), :].astype(F32).T
    iql_t = iql_ref[pl.ds(q0, tq), :].astype(F32).T
    iq3 = jnp.concatenate(
        [jnp.concatenate([t[h * IDX_DIM:(h + 1) * IDX_DIM] for t in (iqh_t, iql_t, iqh_t)], axis=0)
         for h in range(IDX_HEADS)], axis=1).astype(BF16)

    def score_body(c, carry):
        s0 = pl.multiple_of(c * kc, kc)
        d = _dot(ik3_ref[pl.ds(s0, kc), :], iq3)
        sc = wT[0:1] * jnp.maximum(d[:, 0:tq], 0.0)
        for h in range(1, IDX_HEADS):
            sc = sc + wT[h:h + 1] * jnp.maximum(d[:, h * tq:(h + 1) * tq], 0.0)
        sc_ref[pl.ds(s0, kc), :] = jnp.where(s0 + rows <= qpos, sc, -jnp.inf)
        return carry

    lax.fori_loop(0, nchunks, score_body, 0)

    def count(ind_fn):
        def body(c, acc):
            s0 = pl.multiple_of(c * kc, kc)
            ind = ind_fn(sc_ref[pl.ds(s0, kc), :], s0 + rows)
            return acc + ind.reshape(kc // V7X_SUBLANES, V7X_SUBLANES, tq).sum(axis=0)
        acc = lax.fori_loop(0, nchunks, body, jnp.zeros((V7X_SUBLANES, tq), I32))
        return jnp.sum(acc, axis=0, keepdims=True)

    c0 = count(lambda s, idx: jnp.where(s >= 0.0, 1, 0))
    ans = jnp.where(c0 >= topk, 0, INT_MIN).astype(I32)

    def bit_body(j, ans):
        cand = ans | lax.shift_left(jnp.int32(1), 30 - j)
        cf = _ordered_to_f32(cand)
        cnt = count(lambda s, idx: jnp.where(s >= cf, 1, 0))
        return jnp.where(cand <= KEY_NEG_INF, cand, jnp.where(cnt >= topk, cand, ans))

    thr = _ordered_to_f32(lax.fori_loop(0, 31, bit_body, ans))
    cnt_gt = count(lambda s, idx: jnp.where(s > thr, 1, 0))
    need = (topk - cnt_gt).astype(F32)

    tri = (lax.broadcasted_iota(I32, (kc, kc), 1) <= lax.broadcasted_iota(I32, (kc, kc), 0)).astype(BF16)

    def tie_body(c, seen):
        s0 = pl.multiple_of(c * kc, kc)
        sch = sc_ref[pl.ds(s0, kc), :]
        eq = sch == thr
        rank = _dot(tri, jnp.where(eq, 1.0, 0.0).astype(BF16)) + seen
        sc_ref[pl.ds(s0, kc), :] = jnp.where(eq, jnp.where(rank > need, -jnp.inf, sch), sch)
        return rank[kc - 1:kc]

    lax.fori_loop(0, nchunks, tie_body, jnp.zeros((1, tq), F32))
    thr = jnp.maximum(thr, jnp.finfo(F32).min)

    q_all = q_ref[pl.ds(q0, tq), :].astype(F32).T.astype(BF16)
    hd = ATT_DIM

    def logits_into(buf_ref, c, live):
        s0 = pl.multiple_of(c * kc, kc)
        sch = sc_ref[pl.ds(s0, kc), :]
        dead = jnp.where(live, 0.0, NEG_BIG)
        bias_ref[...] = jnp.where(sch >= thr, dead, NEG_BIG)
        for h in range(ATT_HEADS):
            buf_ref[h] = _dot(k4_ref[pl.ds(s0, kc), h * hd:(h + 1) * hd], q_all[h * hd:(h + 1) * hd]) + bias_ref[...]

    def softmax_step(buf_ref, c, state):
        for h in range(ATT_HEADS):
            m = state[h]
            part = buf_ref[h].reshape(kc // V7X_SUBLANES, V7X_SUBLANES, tq).max(axis=0)
            mn = jnp.maximum(m, jnp.max(part, axis=0, keepdims=True))
            state[h] = mn
            p_ref[h] = jnp.exp2(buf_ref[h] - mn).astype(BF16)
            acc_ref[h] = jnp.exp2(m - mn) * acc_ref[h] + _dot(vx_ref[c, h], p_ref[h])

    last = nchunks - 1
    acc_ref[...] = jnp.zeros_like(acc_ref)
    logits_into(lga_ref, 0, True)

    def att_body(t, carry):
        state = list(carry)
        c0 = 2 * t
        c1 = jnp.minimum(c0 + 1, last)
        logits_into(lgb_ref, c1, c0 + 1 <= last)
        softmax_step(lga_ref, c0, state)
        logits_into(lga_ref, jnp.minimum(c0 + 2, last), True)
        softmax_step(lgb_ref, c1, state)
        return tuple(state)

    init = (jnp.full((1, tq), NEG_BIG, F32),) * ATT_HEADS
    lax.fori_loop(0, lax.div(nchunks + 1, 2), att_body, init)
    for h in range(ATT_HEADS):
        acc = acc_ref[h]
        o_ref[0, i, h * hd:(h + 1) * hd, :] = acc[0:hd] / acc[hd:hd + 1]


def _dsa_mixer(q, k, v, iqh, iql, ikh, ikl, iw, bsz, seq):
    w = W_BRANCH
    tq = min(DSA_TQ, seq)
    kc = min(DSA_KC, seq)
    nq = seq // tq
    nc = seq // kc
    topk = min(TOPK_MAX, seq // 4)
    assert kc >= topk and kc % DSA_SUB == 0
    ik3 = jnp.concatenate([ikh, ikh, ikl], axis=1)
    iwT = iw.reshape(bsz, nq, tq, IDX_HEADS).transpose(0, 1, 3, 2)
    per_seq = lambda width: pl.BlockSpec((seq, width), lambda b: (b, 0))
    out = pl.pallas_call(
        functools.partial(_dsa_kernel, seq=seq, tq=tq, kc=kc, topk=topk),
        grid=(bsz,),
        in_specs=[per_seq(w), per_seq(w), per_seq(w), per_seq(IDX_HEADS * IDX_DIM), per_seq(IDX_HEADS * IDX_DIM),
                  per_seq(3 * IDX_DIM), pl.BlockSpec((1, nq, IDX_HEADS, tq), lambda b: (b, 0, 0, 0))],
        out_specs=pl.BlockSpec((1, nq, w, tq), lambda b: (b, 0, 0, 0)),
        out_shape=jax.ShapeDtypeStruct((bsz, nq, w, tq), F32),
        scratch_shapes=[pltpu.VMEM((seq, tq), F32), pltpu.VMEM((ATT_HEADS, kc, tq), F32),
                        pltpu.VMEM((ATT_HEADS, kc, tq), F32), pltpu.VMEM((kc, tq), F32),
                        pltpu.VMEM((ATT_HEADS, kc, tq), BF16), pltpu.VMEM((ATT_HEADS, DSA_VROWS, tq), F32),
                        pltpu.VMEM((nc, ATT_HEADS, DSA_VROWS, kc), BF16)],
        compiler_params=_cparams("parallel"),
        name="dsa_mixer",
    )(q, k, v, iqh, iql, ik3, iwT)
    return out


def _merge_kernel(x_ref, mod_ref, g_ref, o0_ref, o1_ref, o2_ref, o3_ref, gw_ref, gb_ref, bw_ref, ow_ref, x1_ref):
    d = x_ref.shape[1]
    mod = mod_ref[0]
    x = x_ref[...]
    hb = _modulated_norm(x, g_ref[...], mod[1:2], mod[0:1]).astype(BF16)
    mixed = jnp.zeros(x.shape, F32)
    o_dsa = jnp.concatenate([o1_ref[0, j].T for j in range(o1_ref.shape[1])], axis=0)
    for n, o in enumerate((o0_ref[...], o_dsa, o2_ref[...], o3_ref[...])):
        gate = _sigmoid(_dot(hb, gw_ref[:, n * d:(n + 1) * d]) + gb_ref[:, n * d:(n + 1) * d])
        mixed = mixed + gate * _dot(o.astype(BF16), bw_ref[n])
    x1_ref[...] = x + mod[2:3] * _dot(mixed.astype(BF16), ow_ref[...])


def _merge(x2, mod_l, g, branches, gate_w, gate_b, branch_w, out_w, seq):
    n, d = x2.shape
    w = W_BRANCH
    tm = min(512, seq)
    tpb = seq // tm
    row = lambda width: pl.BlockSpec((tm, width), lambda i: (i, 0))
    full = lambda a: pl.BlockSpec(a.shape, lambda i: (0,) * a.ndim)
    args = (gate_w.astype(BF16), gate_b.reshape(1, -1), branch_w.astype(BF16), out_w.astype(BF16))
    tq = branches[1].shape[-1]
    qpt = tm // tq
    dsa_spec = pl.BlockSpec((1, qpt, w, tq), lambda i: (i // tpb, i % tpb, 0, 0))
    return pl.pallas_call(
        _merge_kernel,
        grid=(n // tm,),
        in_specs=[row(d), pl.BlockSpec((1, 6, d), lambda i: (i // tpb, 0, 0)), full(g)]
        + [row(w), dsa_spec, row(w), row(w)] + [full(a) for a in args],
        out_specs=row(d),
        out_shape=jax.ShapeDtypeStruct((n, d), F32),
        compiler_params=_cparams("parallel"),
        name="merge",
    )(x2, mod_l, g, *branches, *args)


def _route_combine(scores, rb):
    biased = scores + rb
    col = lambda a, e: a[e:e + 1, :]
    npg = EXP_PER_GROUP
    gs = []
    for g in range(N_GROUPS):
        best = None
        for j1 in range(npg):
            for j2 in range(j1 + 1, npg):
                s = col(biased, npg * g + j1) + col(biased, npg * g + j2)
                best = s if best is None else jnp.maximum(best, s)
        gs.append(best)
    bg = jnp.zeros(gs[0].shape, I32)
    bv = gs[0]
    for g in range(1, N_GROUPS):
        upd = gs[g] > bv
        bv = jnp.where(upd, gs[g], bv)
        bg = jnp.where(upd, g, bg)
    bsel, ssel = [], []
    for j in range(npg):
        bj, sj = col(biased, j), col(scores, j)
        for g in range(1, N_GROUPS):
            bj = jnp.where(bg == g, col(biased, npg * g + j), bj)
            sj = jnp.where(bg == g, col(scores, npg * g + j), sj)
        bsel.append(bj)
        ssel.append(sj)
    i1, v1, s1 = jnp.zeros(bg.shape, I32), bsel[0], ssel[0]
    for j in range(1, npg):
        upd = bsel[j] > v1
        v1 = jnp.where(upd, bsel[j], v1)
        s1 = jnp.where(upd, ssel[j], s1)
        i1 = jnp.where(upd, j, i1)
    i2, v2, s2 = jnp.zeros(bg.shape, I32), jnp.where(i1 == 0, -jnp.inf, bsel[0]), ssel[0]
    for j in range(1, npg):
        cand = jnp.where(i1 == j, -jnp.inf, bsel[j])
        upd = cand > v2
        v2 = jnp.where(upd, cand, v2)
        s2 = jnp.where(upd, ssel[j], s2)
        i2 = jnp.where(upd, j, i2)
    tot = s1 + s2
    return bg, npg * bg + i1, s1 / tot, npg * bg + i2, s2 / tot


MOE_CHUNK = 128
MOE_POS_LANE = N_EXPERTS


def _route_kernel(x1_ref, mod_ref, g_ref, rwh_ref, rwl_ref, rb_ref, tri_ref, h2_ref, slab_ref, post_ref, cnt_ref):
    mod = mod_ref[0]
    h2 = _modulated_norm(x1_ref[...], g_ref[...], mod[4:5], mod[3:4])
    h2_ref[...] = h2.astype(BF16)
    hh, hl = _split_bf16(h2)
    tm = h2.shape[0]
    logits = _dot_nt(rwh_ref[...], hh) + _dot_nt(rwl_ref[...], hh) + _dot_nt(rwh_ref[...], hl)
    bg, e1, w1, e2, w2 = _route_combine(_sigmoid(logits[0:N_EXPERTS]), rb_ref[...])
    sub = lax.broadcasted_iota(I32, (V7X_SUBLANES, 1), 0)
    member = jnp.where(sub == bg, 1.0, 0.0)
    upto = _dot(member.astype(BF16), tri_ref[...])
    rank = jnp.sum(member * upto, axis=0, keepdims=True)
    sizes = upto[:, tm - 1:tm]
    padded = jnp.floor((sizes + (MOE_CHUNK - 1.0)) * (1.0 / MOE_CHUNK)) * MOE_CHUNK
    offs = [jnp.zeros((1, 1), F32)]
    for g in range(1, N_GROUPS):
        offs.append(offs[-1] + padded[g - 1:g])
    lane = lax.broadcasted_iota(I32, (1, V7X_LANES), 1)
    off = offs[0]
    info = jnp.where(lane == 0, sizes[0:1], 0.0)
    for g in range(1, N_GROUPS):
        off = jnp.where(bg == g, offs[g], off)
        info = info + jnp.where(lane == g, sizes[g:g + 1], 0.0) + jnp.where(lane == N_GROUPS + g, offs[g], 0.0)
    pos = off + rank - 1.0
    sub_l = lax.broadcasted_iota(I32, (V7X_LANES, 1), 0)
    slab_t = (jnp.where(sub_l == e1, w1, 0.0) + jnp.where(sub_l == e2, w2, 0.0)
              + jnp.where(sub_l == MOE_POS_LANE, pos, 0.0))
    slab_ref[...] = slab_t.T
    post_ref[...] = jnp.broadcast_to(pos, post_ref.shape)
    cnt_ref[...] = jnp.broadcast_to(info, cnt_ref.shape)


def _route(x1, mod_l, g, router_w, router_b, seq):
    n, d = x1.shape
    tm = min(1024, seq)
    tpb = seq // tm
    sl = V7X_SUBLANES
    rwh, rwl = _split_bf16(jnp.pad(router_w.T, ((0, V7X_LANES - N_EXPERTS), (0, 0))))
    tri = jnp.triu(jnp.ones((tm, tm), BF16))
    row = lambda width: pl.BlockSpec((tm, width), lambda i: (i, 0))
    full = lambda a: pl.BlockSpec(a.shape, lambda i: (0,) * a.ndim)
    rb = router_b.reshape(N_EXPERTS, 1)
    return pl.pallas_call(
        _route_kernel,
        grid=(n // tm,),
        in_specs=[row(d), pl.BlockSpec((1, 6, d), lambda i: (i // tpb, 0, 0)), full(g), full(rwh), full(rwl),
                  full(rb), full(tri)],
        out_specs=[row(d), row(V7X_LANES), pl.BlockSpec((sl, tm), lambda i: (0, i)),
                   pl.BlockSpec((sl, V7X_LANES), lambda i: (i, 0))],
        out_shape=[jax.ShapeDtypeStruct((n, d), BF16), jax.ShapeDtypeStruct((n, V7X_LANES), F32),
                   jax.ShapeDtypeStruct((sl, n), F32), jax.ShapeDtypeStruct((n // tm * sl, V7X_LANES), F32)],
        compiler_params=_cparams("parallel"),
        name="route",
    )(x1, mod_l, g, rwh, rwl, rb, tri)


def _moe_compact_kernel(goff_ref, gcnt_ref, h2_ref, slab_ref, post_ref, w1_ref, w3_ref, w2_ref, x1_ref, mod_ref,
                        fg_ref, o_ref, hc_ref, cc_ref, yacc_ref, *, final_norm):
    i = pl.program_id(0)
    e = pl.program_id(1)
    rows = hc_ref.shape[0]

    @pl.when(e == 0)
    def _():
        sel = jnp.where(lax.broadcasted_iota(I32, (rows, 1), 0).astype(F32) == post_ref[0:1, :], 1.0, 0.0).astype(BF16)
        hc_ref[...] = _dot(sel, h2_ref[...]).astype(BF16)
        sh, sl_ = _split_bf16(slab_ref[...])
        cc_ref[...] = _dot(sel, sh) + _dot(sel, sl_)
        yacc_ref[...] = jnp.zeros_like(yacc_ref)

    g = lax.div(e, EXP_PER_GROUP)
    off = goff_ref[i * N_GROUPS + g]
    nsteps = lax.div(gcnt_ref[i * N_GROUPS + g] + (MOE_CHUNK - 1), MOE_CHUNK)
    w1 = w1_ref[0, 0]
    w3 = w3_ref[0, 0]
    w2 = w2_ref[0, 0]

    def step(c, carry):
        r0 = pl.multiple_of(off + c * MOE_CHUNK, MOE_CHUNK)
        hg = hc_ref[pl.ds(r0, MOE_CHUNK), :]
        a = _dot(hg, w1)
        b = _dot(hg, w3)
        y = _dot((a * _sigmoid(a) * b).astype(BF16), w2)
        cg = cc_ref[pl.ds(r0, MOE_CHUNK), :]
        lane = lax.broadcasted_iota(I32, cg.shape, 1)
        yacc_ref[pl.ds(r0, MOE_CHUNK), :] += jnp.sum(jnp.where(lane == e, cg, 0.0), axis=1, keepdims=True) * y
        return carry

    lax.fori_loop(0, nsteps, step, 0)

    @pl.when(e == pl.num_programs(1) - 1)
    def _():
        pos = slab_ref[...][:, MOE_POS_LANE:MOE_POS_LANE + 1]
        back = jnp.where(lax.broadcasted_iota(I32, (1, rows), 1).astype(F32) == pos, 1.0, 0.0).astype(BF16)
        out = x1_ref[...] + mod_ref[0][5:6] * _dot(back, yacc_ref[...].astype(BF16))
        if final_norm:
            ms = jnp.mean(out * out, axis=-1, keepdims=True)
            out = out * lax.rsqrt(ms + NORM_EPS) * fg_ref[...]
        o_ref[...] = out


def _moe_compact(h2, slab, post, info, exp_w1, exp_w3, exp_w2, layer, x1, mod_l, final_g, seq, final_norm):
    n, d = x1.shape
    ne, f = exp_w1.shape[1], exp_w1.shape[3]
    tm = min(1024, seq)
    tpb = seq // tm
    rows = tm + N_GROUPS * MOE_CHUNK
    per_tile = info[::V7X_SUBLANES]
    gcnt = per_tile[:, 0:N_GROUPS].astype(I32).reshape(-1)
    goff = per_tile[:, N_GROUPS:2 * N_GROUPS].astype(I32).reshape(-1)
    row = lambda width: pl.BlockSpec((tm, width), lambda i, e, *_: (i, 0))
    return pl.pallas_call(
        functools.partial(_moe_compact_kernel, final_norm=final_norm),
        grid_spec=pltpu.PrefetchScalarGridSpec(
            num_scalar_prefetch=2,
            grid=(n // tm, ne),
            in_specs=[row(d), row(V7X_LANES), pl.BlockSpec((V7X_SUBLANES, tm), lambda i, e, *_: (0, i)),
                      pl.BlockSpec((1, 1, d, f), lambda i, e, *_: (layer, e, 0, 0)),
                      pl.BlockSpec((1, 1, d, f), lambda i, e, *_: (layer, e, 0, 0)),
                      pl.BlockSpec((1, 1, f, d), lambda i, e, *_: (layer, e, 0, 0)),
                      row(d), pl.BlockSpec((1, 6, d), lambda i, e, *_: (i // tpb, 0, 0)),
                      pl.BlockSpec((1, d), lambda i, e, *_: (0, 0))],
            out_specs=row(d),
            scratch_shapes=[pltpu.VMEM((rows, d), BF16), pltpu.VMEM((rows, V7X_LANES), F32),
                            pltpu.VMEM((rows, d), F32)]),
        out_shape=jax.ShapeDtypeStruct((n, d), F32),
        compiler_params=_cparams("parallel", "arbitrary"),
        name="moe_compact",
    )(goff, gcnt, h2, slab, post, exp_w1, exp_w3, exp_w2, x1, mod_l, final_g)


def kernel(x, c, positions, ada_w, ada_b, mix_norm_g, w_in, gate_w, gate_b, branch_w, out_w, s5_lam_re, s5_lam_im, s5_log_step, s5_b_re, s5_b_im, s5_c_re, s5_c_im, s5_d, s5_glu_w, s5_glu_b, rg_conv_w, rg_conv_b, rg_wr, rg_br, rg_wi, rg_bi, rg_lam, rw_mu, rw_w0, rw_w_up, rw_a0, rw_a_up, rw_g_up, rw_k_k, rw_k_a, rw_r_k, rw_gn_w, rw_gn_b, ffn_norm_g, router_w, router_b, exp_w1, exp_w3, exp_w2, final_norm_g):
    bsz, seq, d = x.shape
    depth = ada_w.shape[0]
    n = bsz * seq
    x2 = x.reshape(n, d)
    mod = _ada_mod(c, ada_w, ada_b)
    tabs = _rope_tables(positions)
    fg = final_norm_g.reshape(1, d)
    ew1, ew3, ew2 = exp_w1.astype(BF16), exp_w3.astype(BF16), exp_w2.astype(BF16)
    for l in range(depth):
        mod_l = mod[l]
        g1 = mix_norm_g[l].reshape(1, d)
        (u, q, k, v, iqh, iql, ikh, ikl, iw, rgx, rgg, rwf) = _inproj(x2, mod_l, g1, w_in[l], tabs, seq)
        s5p = _s5_params(s5_lam_re[l], s5_lam_im[l], s5_log_step[l], s5_b_re[l], s5_b_im[l], s5_c_re[l], s5_c_im[l])
        o_s5 = _s5_mixer(u, s5p, s5_d[l], s5_glu_w[l], s5_glu_b[l], bsz, seq)
        o_dsa = _dsa_mixer(q, k, v, iqh, iql, ikh, ikl, iw, bsz, seq)
        o_rg = _rglru_mixer(rgx, rgg, rg_conv_w[l], rg_conv_b[l], rg_wr[l], rg_br[l], rg_wi[l], rg_bi[l],
                            rg_lam[l], bsz, seq)
        o_rw = _rwkv_mixer(rwf, rw_mu[l], rw_w0[l], rw_w_up[l], rw_a0[l], rw_a_up[l], rw_g_up[l], rw_k_k[l],
                           rw_k_a[l], rw_r_k[l], rw_gn_w[l], rw_gn_b[l], bsz, seq)
        x1 = _merge(x2, mod_l, g1, (o_s5, o_dsa, o_rg, o_rw), gate_w[l], gate_b[l], branch_w[l], out_w[l], seq)
        h2, slab, post, info = _route(x1, mod_l, ffn_norm_g[l].reshape(1, d), router_w, router_b, seq)
        x2 = _moe_compact(h2, slab, post, info, ew1, ew3, ew2, l, x1, mod_l, fg, seq, l == depth - 1)
    return x2.reshape(bsz, seq, d)
```

```python
import functools
import math

import jax
import jax.numpy as jnp
from jax import lax
from jax.experimental import pallas as pl
from jax.experimental.pallas import tpu as pltpu

F32 = jnp.float32
BF16 = jnp.bfloat16
I32 = jnp.int32

W_BRANCH = 256
N_BRANCH = 4
NORM_EPS = 1e-6
S5_GROUP = 16
S5_GROUPS = W_BRANCH // S5_GROUP
S5_STATE = 64
ATT_HEADS = 4
ATT_DIM = W_BRANCH // ATT_HEADS
IDX_HEADS = 4
IDX_DIM = 32
TOPK_MAX = 256
ROPE_THETA = 10000.0
RG_BLOCKS = 4
RG_C = 8.0
CONV_WIDTH = 4
RW_HEADS = 4
RW_DIM = W_BRANCH // RW_HEADS
RW_DECAY_LORA = 32
RW_A_LORA = 32
RW_GATE_LORA = 64
RW_GN_EPS = 64e-5
RW_COLS = 3 * W_BRANCH + RW_DECAY_LORA + RW_A_LORA + RW_GATE_LORA
N_EXPERTS = 16
N_GROUPS = 4
EXP_PER_GROUP = N_EXPERTS // N_GROUPS
D_EXPERT = 512

V7X_SUBLANES = 8
V7X_LANES = 128
V7X_VMEM_LIMIT_BYTES = 56 * 1024 * 1024

INT_MIN = -(2 ** 31)
NEG_BIG = -1e30


def _cparams(*sem, flags=None):
    return pltpu.CompilerParams(dimension_semantics=sem, vmem_limit_bytes=V7X_VMEM_LIMIT_BYTES, flags=flags)


def _split_bf16(x):
    hi = x.astype(BF16)
    lo = (x - hi.astype(F32)).astype(BF16)
    return hi, lo


def _dot(a, b):
    return jnp.dot(a, b, preferred_element_type=F32)


def _dot_nt(a, b):
    return lax.dot_general(a, b, (((1,), (1,)), ((), ())), preferred_element_type=F32)


def _dot3(a, b):
    ah, al = _split_bf16(a)
    bh, bl = _split_bf16(b)
    return _dot(ah, bh) + _dot(ah, bl) + _dot(al, bh)


def _sigmoid(x):
    return 1.0 / (1.0 + jnp.exp(-x))


def _softplus(x):
    return jnp.maximum(x, 0.0) + jnp.log(1.0 + jnp.exp(-jnp.abs(x)))


def _gelu(x):
    c = math.sqrt(2.0 / math.pi)
    return 0.5 * x * (1.0 + jnp.tanh(c * (x + 0.044715 * (x * x * x))))


def _rot_half_cols(w, n_heads, dim):
    lead = w.shape[0]
    w4 = w.reshape(lead, n_heads, 2, dim // 2)
    return jnp.concatenate([-w4[:, :, 1:2], w4[:, :, 0:1]], axis=2).reshape(lead, n_heads * dim)


def _ada_kernel(c_ref, w_ref, b_ref, o_ref):
    c = c_ref[...]
    ca = c * _sigmoid(c)
    o_ref[0] = _dot3(ca, w_ref[0]) + b_ref[0]


def _ada_mod(c, ada_w, ada_b):
    depth, d, d6 = ada_w.shape
    bsz = c.shape[0]
    rows = -(-bsz // V7X_SUBLANES) * V7X_SUBLANES
    cp = jnp.zeros((rows, d), F32).at[:bsz].set(c)
    out = pl.pallas_call(
        _ada_kernel,
        grid=(depth, d6 // d),
        in_specs=[pl.BlockSpec((rows, d), lambda l, j: (0, 0)),
                  pl.BlockSpec((1, d, d), lambda l, j: (l, 0, j)),
                  pl.BlockSpec((1, 1, d), lambda l, j: (l, 0, j))],
        out_specs=pl.BlockSpec((1, rows, d), lambda l, j: (l, 0, j)),
        out_shape=jax.ShapeDtypeStruct((depth, rows, d6), F32),
        compiler_params=_cparams("parallel", "parallel"),
        name="ada_mod",
    )(cp, ada_w, ada_b.reshape(depth, 1, d6))
    return out[:, :bsz].reshape(depth, bsz, 6, d)


def _rope_kernel(pos_ref, fa_ref, fi_ref, ca_ref, sa_ref, ci_ref, si_ref):
    pos = pos_ref[...]
    ang_a = pos * fa_ref[...]
    ang_i = pos * fi_ref[...]
    ca_ref[...] = jnp.cos(ang_a)
    sa_ref[...] = jnp.sin(ang_a)
    ci_ref[...] = jnp.cos(ang_i)
    si_ref[...] = jnp.sin(ang_i)


def _rope_tables(positions):
    n = positions.size
    pos = positions.reshape(n, 1).astype(F32)

    def freq_row(dim):
        half = dim // 2
        inv = ROPE_THETA ** (-jnp.arange(half, dtype=F32) / half)
        return jnp.tile(inv, V7X_LANES // half).reshape(1, V7X_LANES)

    tm = min(n, 2048)
    row = pl.BlockSpec((tm, V7X_LANES), lambda i: (i, 0))
    frq = pl.BlockSpec((1, V7X_LANES), lambda i: (0, 0))
    shp = jax.ShapeDtypeStruct((n, V7X_LANES), F32)
    return pl.pallas_call(
        _rope_kernel,
        grid=(n // tm,),
        in_specs=[pl.BlockSpec((tm, 1), lambda i: (i, 0)), frq, frq],
        out_specs=[row, row, row, row],
        out_shape=[shp, shp, shp, shp],
        compiler_params=_cparams("parallel"),
        name="rope_tables",
    )(pos, freq_row(ATT_DIM), freq_row(IDX_DIM))


def _modulated_norm(x, g, scale, shift):
    ms = jnp.mean(x * x, axis=-1, keepdims=True)
    return (x * lax.rsqrt(ms + NORM_EPS)) * g * (1.0 + scale) + shift


def _inproj_kernel(x_ref, mod_ref, g_ref, wm_ref, wih_ref, wil_ref,
                   ca_ref, sa_ref, ci_ref, si_ref,
                   u_ref, q_ref, k_ref, v_ref, iqh_ref, iql_ref, ikh_ref, ikl_ref, iw_ref,
                   rgx_ref, rgg_ref, rw_ref):
    w = W_BRANCH
    mod = mod_ref[0]
    h = _modulated_norm(x_ref[...], g_ref[...], mod[1:2], mod[0:1])
    hh, hl = _split_bf16(h)
    main = _dot(hh, wm_ref[...])
    idx = _dot(hh, wih_ref[...]) + _dot(hh, wil_ref[...]) + _dot(hl, wih_ref[...])
    ca = jnp.concatenate([ca_ref[...], ca_ref[...]], axis=1)
    sa = jnp.concatenate([sa_ref[...], sa_ref[...]], axis=1)
    u_ref[...] = main[:, 0:w]
    q_ref[...] = (main[:, w:2 * w] * ca + main[:, 2 * w:3 * w] * sa).astype(BF16)
    k_ref[...] = (main[:, 3 * w:4 * w] * ca + main[:, 4 * w:5 * w] * sa).astype(BF16)
    v_ref[...] = main[:, 5 * w:6 * w].astype(BF16)
    rgx_ref[...] = main[:, 6 * w:7 * w]
    rgg_ref[...] = main[:, 7 * w:8 * w]
    rw_ref[...] = main[:, 8 * w:8 * w + RW_COLS]
    ci = ci_ref[...]
    si = si_ref[...]
    iq = idx[:, 0:128] * ci + idx[:, 128:256] * si
    ik = idx[:, 256:288] * ci[:, 0:IDX_DIM] + idx[:, 288:320] * si[:, 0:IDX_DIM]
    iqh, iql = _split_bf16(iq)
    ikh, ikl = _split_bf16(ik)
    iqh_ref[...] = iqh
    iql_ref[...] = iql
    ikh_ref[...] = ikh
    ikl_ref[...] = ikl
    iw_ref[...] = idx[:, 320:320 + IDX_HEADS]


def _inproj_weights(w_in):
    w = W_BRANCH
    o = 0
    parts = {}
    for name, width in (("u", w), ("q", w), ("k", w), ("v", w), ("iq", IDX_HEADS * IDX_DIM),
                        ("ik", IDX_DIM), ("iw", IDX_HEADS), ("rgx", w), ("rgg", w), ("rw", RW_COLS)):
        parts[name] = w_in[:, o:o + width]
        o += width
    wq = parts["q"] * (ATT_DIM ** -0.5 * math.log2(math.e))
    main = jnp.concatenate([parts["u"], wq, _rot_half_cols(wq, ATT_HEADS, ATT_DIM),
                            parts["k"], _rot_half_cols(parts["k"], ATT_HEADS, ATT_DIM),
                            parts["v"], parts["rgx"], parts["rgg"], parts["rw"]], axis=1)
    pad = (-main.shape[1]) % V7X_LANES
    main = jnp.pad(main, ((0, 0), (0, pad)))
    idx = jnp.concatenate([parts["iq"], _rot_half_cols(parts["iq"], IDX_HEADS, IDX_DIM),
                           parts["ik"], _rot_half_cols(parts["ik"], 1, IDX_DIM), parts["iw"]], axis=1)
    idx = jnp.pad(idx, ((0, 0), (0, 384 - idx.shape[1])))
    ih, il = _split_bf16(idx)
    return main.astype(BF16), ih, il


def _inproj(x2, mod_l, g, w_in, tabs, seq):
    n, d = x2.shape
    tm = min(512, seq)
    tpb = seq // tm
    wm, wih, wil = _inproj_weights(w_in)
    w = W_BRANCH
    row = lambda width: pl.BlockSpec((tm, width), lambda i: (i, 0))
    full = lambda a: pl.BlockSpec(a.shape, lambda i: (0,) * a.ndim)
    shp = lambda width, dt: jax.ShapeDtypeStruct((n, width), dt)
    tab = row(V7X_LANES)
    return pl.pallas_call(
        _inproj_kernel,
        grid=(n // tm,),
        in_specs=[row(d), pl.BlockSpec((1, 6, d), lambda i: (i // tpb, 0, 0)), full(g),
                  full(wm), full(wih), full(wil), tab, tab, tab, tab],
        out_specs=[row(w), row(w), row(w), row(w), row(128), row(128), row(IDX_DIM), row(IDX_DIM),
                   row(IDX_HEADS), row(w), row(w), row(RW_COLS)],
        out_shape=[shp(w, F32), shp(w, BF16), shp(w, BF16), shp(w, BF16), shp(128, BF16), shp(128, BF16),
                   shp(IDX_DIM, BF16), shp(IDX_DIM, BF16), shp(IDX_HEADS, F32),
                   shp(w, F32), shp(w, F32), shp(RW_COLS, F32)],
        compiler_params=_cparams("parallel"),
        name="inproj",
    )(x2, mod_l, g, wm, wih, wil, *tabs)


def _s5_prep_kernel(lre_r, lim_r, st_r, lre_c, lim_c, st_c, bre_ref, bim_ref,
                    pre_ref, pim_ref, bbre_ref, bbim_ref):
    kk = (lax.broadcasted_iota(I32, (V7X_SUBLANES, 1), 0) + 1).astype(F32)
    step = jnp.exp(st_r[...])
    mag = jnp.exp(kk * (lre_r[...] * step))
    ang = kk * (lim_r[...] * step)
    pre_ref[...] = mag * jnp.cos(ang)
    pim_ref[...] = mag * jnp.sin(ang)
    a = lre_c[...]
    b = lim_c[...]
    stc = jnp.exp(st_c[...])
    m1 = jnp.exp(a * stc)
    nr = m1 * jnp.cos(b * stc) - 1.0
    ni = m1 * jnp.sin(b * stc)
    den = a * a + b * b
    cr = (nr * a + ni * b) / den
    ci = (ni * a - nr * b) / den
    bre = bre_ref[...]
    bim = bim_ref[...]
    bbre_ref[...] = cr * bre - ci * bim
    bbim_ref[...] = cr * bim + ci * bre


def _block_diag(blocks):
    g, r, c = blocks.shape
    eye = jnp.eye(g, dtype=blocks.dtype)
    return (blocks[:, :, None, :] * eye[:, None, :, None]).reshape(g * r, g * c)


def _s5_params(lam_re, lam_im, log_step, b_re, b_im, c_re, c_im):
    g, p, c = b_re.shape
    ns = g * p
    st = jnp.repeat(log_step, p)
    rows = [a.reshape(1, ns) for a in (lam_re, lam_im, st)]
    cols = [a.reshape(ns, 1) for a in (lam_re, lam_im, st)]
    full = lambda a: pl.BlockSpec(a.shape, lambda: (0,) * a.ndim)
    args = rows + cols + [b_re.reshape(ns, c), b_im.reshape(ns, c)]
    pre, pim, bbre, bbim = pl.pallas_call(
        _s5_prep_kernel,
        in_specs=[full(a) for a in args],
        out_specs=[pl.BlockSpec((V7X_SUBLANES, ns), lambda: (0, 0))] * 2
        + [pl.BlockSpec((ns, c), lambda: (0, 0))] * 2,
        out_shape=[jax.ShapeDtypeStruct((V7X_SUBLANES, ns), F32)] * 2
        + [jax.ShapeDtypeStruct((ns, c), F32)] * 2,
        name="s5_prep",
    )(*args)
    bmat = jnp.concatenate([_block_diag(bbre.reshape(g, p, c).transpose(0, 2, 1)),
                            _block_diag(bbim.reshape(g, p, c).transpose(0, 2, 1))], axis=1)
    cmat = jnp.concatenate([_block_diag(c_re.transpose(0, 2, 1)),
                            -_block_diag(c_im.transpose(0, 2, 1))], axis=0)
    return pre, pim, bmat.astype(BF16), cmat.astype(BF16)


def _s5_kernel(u_ref, bmat_ref, cmat_ref, pre_ref, pim_ref, d_ref, gw_ref, gb_ref,
               o_ref, h_ref, carry_ref, *, chunk, ns):
    @pl.when(pl.program_id(1) == 0)
    def _():
        carry_ref[...] = jnp.zeros_like(carry_ref)

    u = u_ref[...]
    bu = _dot(u.astype(BF16), bmat_ref[...])
    nt = chunk // V7X_SUBLANES
    xre = bu[:, :ns].reshape(nt, V7X_SUBLANES, ns)
    xim = bu[:, ns:].reshape(nt, V7X_SUBLANES, ns)
    pre = pre_ref[...]
    pim = pim_ref[...]
    rowi = lax.broadcasted_iota(I32, (V7X_SUBLANES, 1), 0)
    for s in (1, 2, 4):
        cre = jnp.where(rowi >= s, pre[s - 1:s], 0.0)[None]
        cim = jnp.where(rowi >= s, pim[s - 1:s], 0.0)[None]
        sre = pltpu.roll(xre, s, 1)
        sim = pltpu.roll(xim, s, 1)
        xre, xim = xre + (cre * sre - cim * sim), xim + (cre * sim + cim * sre)
    cr = carry_ref[0:1, :]
    ci = carry_ref[1:2, :]
    for t in range(nt):
        hre = xre[t] + (pre * cr - pim * ci)
        him = xim[t] + (pre * ci + pim * cr)
        h_ref[t * V7X_SUBLANES:(t + 1) * V7X_SUBLANES, 0:ns] = hre
        h_ref[t * V7X_SUBLANES:(t + 1) * V7X_SUBLANES, ns:2 * ns] = him
        cr = hre[V7X_SUBLANES - 1:V7X_SUBLANES]
        ci = him[V7X_SUBLANES - 1:V7X_SUBLANES]
    carry_ref[0:1, :] = cr
    carry_ref[1:2, :] = ci
    y = _dot(h_ref[...].astype(BF16), cmat_ref[...]) + d_ref[...] * u
    y = _gelu(y)
    z = _dot(y.astype(BF16), gw_ref[...]) + gb_ref[...]
    o_ref[...] = y * _sigmoid(z)


def _s5_mixer(u, params, d_skip, glu_w, glu_b, bsz, seq):
    pre, pim, bmat, cmat = params
    n, w = u.shape
    ns = pre.shape[1]
    chunk = min(256, seq)
    cpb = seq // chunk
    full = lambda a: pl.BlockSpec(a.shape, lambda b, c: (0,) * a.ndim)
    row = pl.BlockSpec((chunk, w), lambda b, c: (b * cpb + c, 0))
    args = (u, bmat, cmat, pre, pim, d_skip.reshape(1, w), glu_w.astype(BF16), glu_b.reshape(1, w))
    return pl.pallas_call(
        functools.partial(_s5_kernel, chunk=chunk, ns=ns),
        grid=(bsz, cpb),
        in_specs=[row] + [full(a) for a in args[1:]],
        out_specs=row,
        out_shape=jax.ShapeDtypeStruct((n, w), F32),
        scratch_shapes=[pltpu.VMEM((chunk, 2 * ns), F32), pltpu.VMEM((V7X_SUBLANES, ns), F32)],
        compiler_params=_cparams("parallel", "arbitrary"),
        name="s5_mixer",
    )(*args)


def _rglru_kernel(x_ref, gate_ref, cw_ref, cb_ref, wr_ref, br_ref, wi_ref, bi_ref, lam_ref,
                  o_ref, tail_ref, carry_ref, *, chunk):
    w = W_BRANCH

    @pl.when(pl.program_id(1) == 0)
    def _():
        tail_ref[...] = jnp.zeros_like(tail_ref)
        carry_ref[...] = jnp.zeros_like(carry_ref)

    nt = chunk // V7X_SUBLANES
    x = x_ref[...]
    tail = tail_ref[...]
    rowc = lax.broadcasted_iota(I32, (chunk, 1), 0)
    cw = cw_ref[...]
    xc = cw[CONV_WIDTH - 1:CONV_WIDTH] * x + cb_ref[...]
    for sh in range(1, CONV_WIDTH):
        prev = jnp.broadcast_to(pltpu.roll(tail, sh, 0)[None], (nt, V7X_SUBLANES, w)).reshape(chunk, w)
        xs = jnp.where(rowc < sh, prev, pltpu.roll(x, sh, 0))
        xc = xc + cw[CONV_WIDTH - 1 - sh:CONV_WIDTH - sh] * xs
    tail_ref[...] = x[chunk - V7X_SUBLANES:chunk]
    xb = xc.astype(BF16)
    r = _sigmoid(_dot(xb, wr_ref[...]) + br_ref[...])
    i = _sigmoid(_dot(xb, wi_ref[...]) + bi_ref[...])
    log_a = (-RG_C) * r * _softplus(-lam_ref[...])
    a = jnp.exp(log_a)
    mult = jnp.sqrt(1.0 - jnp.exp(2.0 * log_a))
    b = mult * i * xc
    a3 = a.reshape(nt, V7X_SUBLANES, w)
    b3 = b.reshape(nt, V7X_SUBLANES, w)
    rowi = lax.broadcasted_iota(I32, (1, V7X_SUBLANES, 1), 1)
    for s in (1, 2, 4):
        keep = rowi >= s
        ash = jnp.where(keep, pltpu.roll(a3, s, 1), 1.0)
        bsh = jnp.where(keep, pltpu.roll(b3, s, 1), 0.0)
        b3 = b3 + a3 * bsh
        a3 = a3 * ash
    h = carry_ref[0:1, :]
    gate = _gelu(gate_ref[...])
    for t in range(nt):
        ht = b3[t] + a3[t] * h
        o_ref[t * V7X_SUBLANES:(t + 1) * V7X_SUBLANES, :] = ht * gate[t * V7X_SUBLANES:(t + 1) * V7X_SUBLANES]
        h = ht[V7X_SUBLANES - 1:V7X_SUBLANES]
    carry_ref[0:1, :] = h


def _rglru_mixer(xr, gate, conv_w, conv_b, w_r, b_r, w_i, b_i, lam, bsz, seq):
    n, w = xr.shape
    chunk = min(256, seq)
    cpb = seq // chunk
    full = lambda a: pl.BlockSpec(a.shape, lambda b, c: (0,) * a.ndim)
    row = pl.BlockSpec((chunk, w), lambda b, c: (b * cpb + c, 0))
    args = (xr, gate, conv_w, conv_b.reshape(1, w), _block_diag(w_r).astype(BF16), b_r.reshape(1, w),
            _block_diag(w_i).astype(BF16), b_i.reshape(1, w), lam.reshape(1, w))
    return pl.pallas_call(
        functools.partial(_rglru_kernel, chunk=chunk),
        grid=(bsz, cpb),
        in_specs=[row, row] + [full(a) for a in args[2:]],
        out_specs=row,
        out_shape=jax.ShapeDtypeStruct((n, w), F32),
        scratch_shapes=[pltpu.VMEM((V7X_SUBLANES, w), F32), pltpu.VMEM((V7X_SUBLANES, w), F32)],
        compiler_params=_cparams("parallel", "arbitrary"),
        name="rglru_mixer",
    )(*args)


RW_CHUNK = 64


def _segsum(x, ones_bf16):
    hi, lo = _split_bf16(x)
    return _dot(hi, ones_bf16) + _dot(lo, ones_bf16)


def _stack_heads(x, head_of_lane):
    return jnp.concatenate([jnp.where(head_of_lane == h, x, 0.0) for h in range(RW_HEADS)], axis=0)


def _rwkv_chunk_kernel(f_ref, fprev_ref, mu_ref, w0_ref, wup_ref, a0_ref, aup_ref, gup_ref, kk_ref, ka_ref,
                       rk_ref, ones_ref, rt2_ref, y1_ref, m_ref, s1_ref, pl_ref, g_ref, bonus_ref,
                       kt_s, rt_s, kib_s, bib_s, vsb_s, vst_s, kip_s, bip_s, a0_s, a1_s, a2_s, avk_s, bbk_s, bvk_s,
                       pw_s, kt2_s, kt2t_s, u0_s, u0t_s, tinv_s, *, bsz, chunk):
    w = W_BRANCH
    sl = V7X_SUBLANES
    n4 = RW_HEADS * chunk
    first = pl.program_id(0) == 0
    ones = ones_ref[...]
    rowc = lax.broadcasted_iota(I32, (chunk, 1), 0)
    head_of_lane = lax.broadcasted_iota(I32, (1, w), 1) // RW_DIM
    ri = lax.broadcasted_iota(I32, (n4, n4), 0)
    ci = lax.broadcasted_iota(I32, (n4, n4), 1)
    strict = ci < ri
    incl = ci <= ri
    same16 = (ri // 16) == (ci // 16)
    same32 = (ri // 32) == (ci // 32)
    eye = (ri == ci).astype(F32)
    tri_c = (lax.broadcasted_iota(I32, (chunk, chunk), 1)
             <= lax.broadcasted_iota(I32, (chunk, chunk), 0)).astype(BF16)

    for b in range(bsz):
        f = f_ref[b]
        tail = jnp.where(first, 0.0, fprev_ref[b])
        prev = jnp.where(rowc < 1,
                         jnp.broadcast_to(pltpu.roll(tail, 1, 0)[None], (chunk // sl, sl, RW_COLS)).reshape(chunk, RW_COLS),
                         pltpu.roll(f, 1, 0))
        f = f + (prev - f) * mu_ref[...]
        r = f[:, 0:w]
        k = f[:, w:2 * w]
        v = f[:, 2 * w:3 * w]
        wd = f[:, 3 * w:3 * w + RW_DECAY_LORA]
        ad = f[:, 3 * w + RW_DECAY_LORA:3 * w + RW_DECAY_LORA + RW_A_LORA]
        gd = f[:, 3 * w + RW_DECAY_LORA + RW_A_LORA:RW_COLS]
        wlog = -_softplus(-(w0_ref[...] + _dot(jnp.tanh(wd).astype(BF16), wup_ref[...]))) - 0.5
        logd = -jnp.exp(wlog)
        a = _sigmoid(a0_ref[...] + _dot(ad.astype(BF16), aup_ref[...]))
        g = _dot(_sigmoid(gd).astype(BF16), gup_ref[...])
        kk = k * kk_ref[...]
        kk = kk / jnp.maximum(jnp.sqrt(_segsum(kk * kk, ones)), 1e-12)
        k2 = k * (1.0 + (a - 1.0) * ka_ref[...])
        beta = kk * a
        l1 = logd.astype(BF16)
        rem = logd - l1.astype(F32)
        l2 = rem.astype(BF16)
        l3 = (rem - l2.astype(F32)).astype(BF16)
        cum = _dot(tri_c, l1) + _dot(tri_c, l2) + _dot(tri_c, l3)
        p = jnp.exp(cum)
        pinv = jnp.exp(-cum)
        kt = _stack_heads(kk * jnp.exp(cum - logd), head_of_lane).astype(BF16)
        rt = _stack_heads(r * p, head_of_lane).astype(BF16)
        ki = _stack_heads(k2 * pinv, head_of_lane)
        bi = _stack_heads(beta * pinv, head_of_lane)
        vs = _stack_heads(v, head_of_lane)
        pl_row = p[chunk - 1:chunk]
        kt_s[b] = kt
        rt_s[b] = rt
        kib_s[b] = ki.astype(BF16)
        bib_s[b] = bi.astype(BF16)
        vsb_s[b] = vs.astype(BF16)
        vst_s[b] = vs.T.astype(BF16)
        kip_s[b] = (ki * pl_row).astype(BF16)
        bip_s[b] = (bi * pl_row).astype(BF16)
        pl_ref[b, 0] = jnp.broadcast_to(pl_row, (sl, w))
        g_ref[b] = g
        bonus_ref[b] = _segsum(r * k2 * rk_ref[...], ones) * v

    for b in range(bsz):
        kt, rt, kib, bib = kt_s[b], rt_s[b], kib_s[b], bib_s[b]
        a_bk = jnp.where(strict, _dot_nt(kt, bib), 0.0)
        a0 = jnp.where(same16, a_bk, 0.0)
        a0_s[b] = a0.astype(BF16)
        a1_s[b] = jnp.where(same32 & jnp.logical_not(same16), a_bk, 0.0).astype(BF16)
        a2_s[b] = jnp.where(same32, 0.0, a_bk).astype(BF16)
        tinv_s[b] = eye - a0
        avk_s[b] = jnp.where(strict, _dot_nt(kt, kib), 0.0).astype(BF16)
        bbk_s[b] = jnp.where(incl, _dot_nt(rt, bib), 0.0).astype(BF16)
        bvk_s[b] = jnp.where(incl, _dot_nt(rt, kib), 0.0).astype(BF16)
    for b in range(bsz):
        pw_s[b] = _dot(a0_s[b], a0_s[b]).astype(BF16)
    for it in range(3):
        for b in range(bsz):
            tinv_s[b] = tinv_s[b] + _dot(tinv_s[b].astype(BF16), pw_s[b])
        if it < 2:
            for b in range(bsz):
                pw_s[b] = _dot(pw_s[b], pw_s[b]).astype(BF16)
    for off_s in (a1_s, a2_s):
        for b in range(bsz):
            pw_s[b] = _dot(tinv_s[b].astype(BF16), off_s[b]).astype(BF16)
        for b in range(bsz):
            tinv_s[b] = tinv_s[b] - _dot(pw_s[b], tinv_s[b].astype(BF16))
    for b in range(bsz):
        kt2 = _dot(tinv_s[b].astype(BF16), kt_s[b])
        kt2_s[b] = kt2.astype(BF16)
        kt2t_s[b] = kt2.T.astype(BF16)
        pw_s[b] = _dot(avk_s[b], vsb_s[b]).astype(BF16)
    for b in range(bsz):
        u0 = _dot(tinv_s[b].astype(BF16), pw_s[b])
        u0_s[b] = u0.astype(BF16)
        u0t_s[b] = u0.T.astype(BF16)
    for b in range(bsz):
        bbk = bbk_s[b]
        y1s = _dot(bvk_s[b], vsb_s[b]) - _dot(bbk, u0_s[b])
        y1 = y1s[0:chunk]
        for h in range(1, RW_HEADS):
            y1 = y1 + y1s[h * chunk:(h + 1) * chunk]
        y1_ref[b] = y1
        rt2_ref[b, 0] = (rt_s[b].astype(F32) - _dot(bbk, kt2_s[b])).astype(BF16)
        m_ref[b, 0] = _dot(kt2t_s[b], bip_s[b]).astype(BF16)
        s1_ref[b, 0] = _dot(vst_s[b], kip_s[b]) - _dot(u0t_s[b], bip_s[b])


def _rwkv_state_kernel(rt2_ref, y1_ref, m_ref, s1_ref, pl_ref, g_ref, bonus_ref, gnw_ref, gnb_ref, ones_ref,
                       o_ref, state_ref, *, bsz, chunk):
    @pl.when(pl.program_id(0) == 0)
    def _():
        state_ref[...] = jnp.zeros_like(state_ref)

    ones = ones_ref[...]
    ys_all, yc_all = [], []
    for b in range(bsz):
        st = state_ref[b]
        stb = st.astype(BF16)
        ys = _dot_nt(rt2_ref[b, 0], stb)
        y = y1_ref[b] + ys[0:chunk]
        for h in range(1, RW_HEADS):
            y = y + ys[h * chunk:(h + 1) * chunk]
        ys_all.append(y)
        state_ref[b] = st * pl_ref[b, 0][0:1] + s1_ref[b, 0] - _dot(stb, m_ref[b, 0])
    for b in range(bsz):
        yc_all.append(ys_all[b] - _segsum(ys_all[b], ones) * (1.0 / RW_DIM))
    for b in range(bsz):
        yc = yc_all[b]
        var = _segsum(yc * yc, ones) * (1.0 / RW_DIM)
        yn = yc * lax.rsqrt(var + RW_GN_EPS) * gnw_ref[...] + gnb_ref[...]
        o_ref[b] = (yn + bonus_ref[b]) * g_ref[b]


def _rwkv_mixer(feat, mu, w0, w_up, a0, a_up, g_up, k_k, k_a, r_k, gn_w, gn_b, bsz, seq):
    w = W_BRANCH
    sl = V7X_SUBLANES
    chunk = min(RW_CHUNK, seq)
    nch = seq // chunk
    n4 = RW_HEADS * chunk
    f3 = feat.reshape(bsz, seq, RW_COLS)
    row1 = lambda a: a.reshape(1, -1)
    ones = _block_diag(jnp.ones((RW_HEADS, RW_DIM, RW_DIM), F32)).astype(BF16)
    full = lambda a: pl.BlockSpec(a.shape, lambda c: (0,) * a.ndim)
    tok = lambda width: pl.BlockSpec((bsz, chunk, width), lambda c: (0, c, 0))
    per_chunk = lambda rows, cols: pl.BlockSpec((bsz, 1, rows, cols), lambda c: (0, c, 0, 0))
    a_args = (row1(mu), row1(w0), w_up.astype(BF16), row1(a0), a_up.astype(BF16), g_up.astype(BF16),
              row1(k_k), row1(k_a), row1(r_k), ones)
    mids = pl.pallas_call(
        functools.partial(_rwkv_chunk_kernel, bsz=bsz, chunk=chunk),
        grid=(nch,),
        in_specs=[tok(RW_COLS),
                  pl.BlockSpec((bsz, sl, RW_COLS), lambda c: (0, jnp.maximum(c * (chunk // sl) - 1, 0), 0))]
        + [full(a) for a in a_args],
        out_specs=[per_chunk(n4, w), tok(w), per_chunk(w, w), per_chunk(w, w), per_chunk(sl, w), tok(w), tok(w)],
        out_shape=[jax.ShapeDtypeStruct((bsz, nch, n4, w), BF16), jax.ShapeDtypeStruct((bsz, seq, w), F32),
                   jax.ShapeDtypeStruct((bsz, nch, w, w), BF16), jax.ShapeDtypeStruct((bsz, nch, w, w), F32),
                   jax.ShapeDtypeStruct((bsz, nch, sl, w), F32), jax.ShapeDtypeStruct((bsz, seq, w), F32),
                   jax.ShapeDtypeStruct((bsz, seq, w), F32)],
        scratch_shapes=[pltpu.VMEM((bsz, n4, w), BF16)] * 19 + [pltpu.VMEM((bsz, n4, w), F32)],
        compiler_params=_cparams("parallel"),
        name="rwkv_chunks",
    )(f3, f3, *a_args)
    b_args = (row1(gn_w), row1(gn_b), ones)
    out = pl.pallas_call(
        functools.partial(_rwkv_state_kernel, bsz=bsz, chunk=chunk),
        grid=(nch,),
        in_specs=[per_chunk(n4, w), tok(w), per_chunk(w, w), per_chunk(w, w), per_chunk(sl, w), tok(w), tok(w)]
        + [full(a) for a in b_args],
        out_specs=tok(w),
        out_shape=jax.ShapeDtypeStruct((bsz, seq, w), F32),
        scratch_shapes=[pltpu.VMEM((bsz, w, w), F32)],
        compiler_params=_cparams("arbitrary"),
        name="rwkv_state",
    )(*mids, *b_args)
    return out.reshape(bsz * seq, w)


DSA_TQ = 256
DSA_KC = 512


DSA_VROWS = ATT_DIM + 16
KEY_NEG_INF = INT_MIN + 0x7FFFFF


def _ordered_to_f32(key):
    return lax.bitcast_convert_type(jnp.where(key >= 0, key, key ^ 0x7FFFFFFF), F32)


def _dsa_kernel(*refs, seq, tq, kc, topk):
    v_ref, vx_ref = refs[2], refs[-1]
    hd = ATT_DIM

    def prep(c, carry):
        s0 = pl.multiple_of(c * kc, kc)
        vt = v_ref[pl.ds(s0, kc), :].astype(F32).T
        for h in range(ATT_HEADS):
            vx_ref[c, h, 0:hd, :] = vt[h * hd:(h + 1) * hd].astype(BF16)
            vx_ref[c, h, hd:DSA_VROWS, :] = jnp.ones((DSA_VROWS - hd, kc), BF16)
        return carry

    lax.fori_loop(0, seq // kc, prep, 0)

    def block(i, carry):
        _dsa_block(i, *refs, seq=seq, tq=tq, kc=kc, topk=topk)
        return carry

    lax.fori_loop(0, seq // tq, block, 0)


def _dsa_block(i, q_ref, k_ref, v_ref, iqh_ref, iql_ref, ik3_ref, iwT_ref, o_ref, sc_ref, lga_ref, lgb_ref, bias_ref,
               p_ref, acc_ref, vx_ref, *, seq, tq, kc, topk):
    nchunks = lax.div((i + 1) * tq + (kc - 1), kc)
    qpos = i * tq + lax.broadcasted_iota(I32, (1, tq), 1)
    rows = lax.broadcasted_iota(I32, (kc, 1), 0)
    wT = iwT_ref[0, i]
    q0 = pl.multiple_of(i * tq, tq)
    iqh_t = iqh_ref[pl.ds(q0, tq), :].astype(F32).T
    iql_t = iql_ref[pl.ds(q0, tq), :].astype(F32).T
    iq3 = jnp.concatenate(
        [jnp.concatenate([t[h * IDX_DIM:(h + 1) * IDX_DIM] for t in (iqh_t, iql_t, iqh_t)], axis=0)
         for h in range(IDX_HEADS)], axis=1).astype(BF16)

    def score_body(c, carry):
        s0 = pl.multiple_of(c * kc, kc)
        d = _dot(ik3_ref[pl.ds(s0, kc), :], iq3)
        sc = wT[0:1] * jnp.maximum(d[:, 0:tq], 0.0)
        for h in range(1, IDX_HEADS):
            sc = sc + wT[h:h + 1] * jnp.maximum(d[:, h * tq:(h + 1) * tq], 0.0)
        sc_ref[pl.ds(s0, kc), :] = jnp.where(s0 + rows <= qpos, sc, -jnp.inf)
        return carry

    lax.fori_loop(0, nchunks, score_body, 0)

    def count(ind_fn):
        def body(c, acc):
            s0 = pl.multiple_of(c * kc, kc)
            ind = ind_fn(sc_ref[pl.ds(s0, kc), :], s0 + rows)
            return acc + ind.reshape(kc // V7X_SUBLANES, V7X_SUBLANES, tq).sum(axis=0)
        acc = lax.fori_loop(0, nchunks, body, jnp.zeros((V7X_SUBLANES, tq), I32))
        return jnp.sum(acc, axis=0, keepdims=True)

    c0 = count(lambda s, idx: jnp.where(s >= 0.0, 1, 0))
    ans = jnp.where(c0 >= topk, 0, INT_MIN).astype(I32)

    def bit_body(j, ans):
        cand = ans | lax.shift_left(jnp.int32(1), 30 - j)
        cf = _ordered_to_f32(cand)
        cnt = count(lambda s, idx: jnp.where(s >= cf, 1, 0))
        return jnp.where(cand <= KEY_NEG_INF, cand, jnp.where(cnt >= topk, cand, ans))

    thr = _ordered_to_f32(lax.fori_loop(0, 31, bit_body, ans))
    cnt_gt = count(lambda s, idx: jnp.where(s > thr, 1, 0))
    need = (topk - cnt_gt).astype(F32)

    tri = (lax.broadcasted_iota(I32, (kc, kc), 1) <= lax.broadcasted_iota(I32, (kc, kc), 0)).astype(BF16)

    def tie_body(c, seen):
        s0 = pl.multiple_of(c * kc, kc)
        sch = sc_ref[pl.ds(s0, kc), :]
        eq = sch == thr
        rank = _dot(tri, jnp.where(eq, 1.0, 0.0).astype(BF16)) + seen
        sc_ref[pl.ds(s0, kc), :] = jnp.where(eq, jnp.where(rank > need, -jnp.inf, sch), sch)
        return rank[kc - 1:kc]

    lax.fori_loop(0, nchunks, tie_body, jnp.zeros((1, tq), F32))
    thr = jnp.maximum(thr, jnp.finfo(F32).min)

    q_all = q_ref[pl.ds(q0, tq), :].astype(F32).T.astype(BF16)
    hd = ATT_DIM

    def logits_into(buf_ref, c, live):
        s0 = pl.multiple_of(c * kc, kc)
        sch = sc_ref[pl.ds(s0, kc), :]
        dead = jnp.where(live, 0.0, NEG_BIG)
        bias_ref[...] = jnp.where(sch >= thr, dead, NEG_BIG)
        for h in range(ATT_HEADS):
            buf_ref[h] = _dot(k_ref[pl.ds(s0, kc), h * hd:(h + 1) * hd], q_all[h * hd:(h + 1) * hd]) + bias_ref[...]

    def softmax_step(buf_ref, c, state):
        for h in range(ATT_HEADS):
            m = state[h]
            part = buf_ref[h].reshape(kc // V7X_SUBLANES, V7X_SUBLANES, tq).max(axis=0)
            mn = jnp.maximum(m, jnp.max(part, axis=0, keepdims=True))
            state[h] = mn
            p_ref[h] = jnp.exp2(buf_ref[h] - mn).astype(BF16)
            acc_ref[h] = jnp.exp2(m - mn) * acc_ref[h] + _dot(vx_ref[c, h], p_ref[h])

    last = nchunks - 1
    acc_ref[...] = jnp.zeros_like(acc_ref)
    logits_into(lga_ref, 0, True)

    def att_body(t, carry):
        state = list(carry)
        c0 = 2 * t
        c1 = jnp.minimum(c0 + 1, last)
        logits_into(lgb_ref, c1, c0 + 1 <= last)
        softmax_step(lga_ref, c0, state)
        logits_into(lga_ref, jnp.minimum(c0 + 2, last), True)
        softmax_step(lgb_ref, c1, state)
        return tuple(state)

    init = (jnp.full((1, tq), NEG_BIG, F32),) * ATT_HEADS
    lax.fori_loop(0, lax.div(nchunks + 1, 2), att_body, init)
    for h in range(ATT_HEADS):
        acc = acc_ref[h]
        o_ref[0, i, h * hd:(h + 1) * hd, :] = acc[0:hd] / acc[hd:hd + 1]


def _dsa_mixer(q, k, v, iqh, iql, ikh, ikl, iw, bsz, seq):
    w = W_BRANCH
    tq = min(DSA_TQ, seq)
    kc = min(DSA_KC, seq)
    nq = seq // tq
    nc = seq // kc
    topk = min(TOPK_MAX, seq // 4)
    assert kc >= topk
    ik3 = jnp.concatenate([ikh, ikh, ikl], axis=1)
    iwT = iw.reshape(bsz, nq, tq, IDX_HEADS).transpose(0, 1, 3, 2)
    per_seq = lambda width: pl.BlockSpec((seq, width), lambda b: (b, 0))
    out = pl.pallas_call(
        functools.partial(_dsa_kernel, seq=seq, tq=tq, kc=kc, topk=topk),
        grid=(bsz,),
        in_specs=[per_seq(w), per_seq(w), per_seq(w), per_seq(IDX_HEADS * IDX_DIM), per_seq(IDX_HEADS * IDX_DIM),
                  per_seq(3 * IDX_DIM), pl.BlockSpec((1, nq, IDX_HEADS, tq), lambda b: (b, 0, 0, 0))],
        out_specs=pl.BlockSpec((1, nq, w, tq), lambda b: (b, 0, 0, 0)),
        out_shape=jax.ShapeDtypeStruct((bsz, nq, w, tq), F32),
        scratch_shapes=[pltpu.VMEM((seq, tq), F32), pltpu.VMEM((ATT_HEADS, kc, tq), F32),
                        pltpu.VMEM((ATT_HEADS, kc, tq), F32), pltpu.VMEM((kc, tq), F32),
                        pltpu.VMEM((ATT_HEADS, kc, tq), BF16), pltpu.VMEM((ATT_HEADS, DSA_VROWS, tq), F32),
                        pltpu.VMEM((nc, ATT_HEADS, DSA_VROWS, kc), BF16)],
        compiler_params=_cparams("parallel"),
        name="dsa_mixer",
    )(q, k, v, iqh, iql, ik3, iwT)
    return out


def _merge_kernel(x_ref, mod_ref, g_ref, o0_ref, o1_ref, o2_ref, o3_ref, gw_ref, gb_ref, bw_ref, ow_ref, x1_ref):
    d = x_ref.shape[1]
    mod = mod_ref[0]
    x = x_ref[...]
    hb = _modulated_norm(x, g_ref[...], mod[1:2], mod[0:1]).astype(BF16)
    mixed = jnp.zeros(x.shape, F32)
    o_dsa = jnp.concatenate([o1_ref[0, j].T for j in range(o1_ref.shape[1])], axis=0)
    for n, o in enumerate((o0_ref[...], o_dsa, o2_ref[...], o3_ref[...])):
        gate = _sigmoid(_dot(hb, gw_ref[:, n * d:(n + 1) * d]) + gb_ref[:, n * d:(n + 1) * d])
        mixed = mixed + gate * _dot(o.astype(BF16), bw_ref[n])
    x1_ref[...] = x + mod[2:3] * _dot(mixed.astype(BF16), ow_ref[...])


def _merge(x2, mod_l, g, branches, gate_w, gate_b, branch_w, out_w, seq):
    n, d = x2.shape
    w = W_BRANCH
    tm = min(512, seq)
    tpb = seq // tm
    row = lambda width: pl.BlockSpec((tm, width), lambda i: (i, 0))
    full = lambda a: pl.BlockSpec(a.shape, lambda i: (0,) * a.ndim)
    args = (gate_w.astype(BF16), gate_b.reshape(1, -1), branch_w.astype(BF16), out_w.astype(BF16))
    tq = branches[1].shape[-1]
    qpt = tm // tq
    dsa_spec = pl.BlockSpec((1, qpt, w, tq), lambda i: (i // tpb, i % tpb, 0, 0))
    return pl.pallas_call(
        _merge_kernel,
        grid=(n // tm,),
        in_specs=[row(d), pl.BlockSpec((1, 6, d), lambda i: (i // tpb, 0, 0)), full(g)]
        + [row(w), dsa_spec, row(w), row(w)] + [full(a) for a in args],
        out_specs=row(d),
        out_shape=jax.ShapeDtypeStruct((n, d), F32),
        compiler_params=_cparams("parallel"),
        name="merge",
    )(x2, mod_l, g, *branches, *args)


def _route_combine(scores, rb):
    biased = scores + rb
    col = lambda a, e: a[e:e + 1, :]
    npg = EXP_PER_GROUP
    gs = []
    for g in range(N_GROUPS):
        best = None
        for j1 in range(npg):
            for j2 in range(j1 + 1, npg):
                s = col(biased, npg * g + j1) + col(biased, npg * g + j2)
                best = s if best is None else jnp.maximum(best, s)
        gs.append(best)
    bg = jnp.zeros(gs[0].shape, I32)
    bv = gs[0]
    for g in range(1, N_GROUPS):
        upd = gs[g] > bv
        bv = jnp.where(upd, gs[g], bv)
        bg = jnp.where(upd, g, bg)
    bsel, ssel = [], []
    for j in range(npg):
        bj, sj = col(biased, j), col(scores, j)
        for g in range(1, N_GROUPS):
            bj = jnp.where(bg == g, col(biased, npg * g + j), bj)
            sj = jnp.where(bg == g, col(scores, npg * g + j), sj)
        bsel.append(bj)
        ssel.append(sj)
    i1, v1, s1 = jnp.zeros(bg.shape, I32), bsel[0], ssel[0]
    for j in range(1, npg):
        upd = bsel[j] > v1
        v1 = jnp.where(upd, bsel[j], v1)
        s1 = jnp.where(upd, ssel[j], s1)
        i1 = jnp.where(upd, j, i1)
    i2, v2, s2 = jnp.zeros(bg.shape, I32), jnp.where(i1 == 0, -jnp.inf, bsel[0]), ssel[0]
    for j in range(1, npg):
        cand = jnp.where(i1 == j, -jnp.inf, bsel[j])
        upd = cand > v2
        v2 = jnp.where(upd, cand, v2)
        s2 = jnp.where(upd, ssel[j], s2)
        i2 = jnp.where(upd, j, i2)
    tot = s1 + s2
    return bg, npg * bg + i1, s1 / tot, npg * bg + i2, s2 / tot


MOE_CHUNK = 128
MOE_POS_LANE = N_EXPERTS


def _route_kernel(x1_ref, mod_ref, g_ref, rwh_ref, rwl_ref, rb_ref, tri_ref, h2_ref, slab_ref, post_ref, cnt_ref):
    mod = mod_ref[0]
    h2 = _modulated_norm(x1_ref[...], g_ref[...], mod[4:5], mod[3:4])
    h2_ref[...] = h2.astype(BF16)
    hh, hl = _split_bf16(h2)
    tm = h2.shape[0]
    logits = _dot_nt(rwh_ref[...], hh) + _dot_nt(rwl_ref[...], hh) + _dot_nt(rwh_ref[...], hl)
    bg, e1, w1, e2, w2 = _route_combine(_sigmoid(logits[0:N_EXPERTS]), rb_ref[...])
    sub = lax.broadcasted_iota(I32, (V7X_SUBLANES, 1), 0)
    member = jnp.where(sub == bg, 1.0, 0.0)
    upto = _dot(member.astype(BF16), tri_ref[...])
    rank = jnp.sum(member * upto, axis=0, keepdims=True)
    sizes = upto[:, tm - 1:tm]
    padded = jnp.floor((sizes + (MOE_CHUNK - 1.0)) * (1.0 / MOE_CHUNK)) * MOE_CHUNK
    offs = [jnp.zeros((1, 1), F32)]
    for g in range(1, N_GROUPS):
        offs.append(offs[-1] + padded[g - 1:g])
    lane = lax.broadcasted_iota(I32, (1, V7X_LANES), 1)
    off = offs[0]
    info = jnp.where(lane == 0, sizes[0:1], 0.0)
    for g in range(1, N_GROUPS):
        off = jnp.where(bg == g, offs[g], off)
        info = info + jnp.where(lane == g, sizes[g:g + 1], 0.0) + jnp.where(lane == N_GROUPS + g, offs[g], 0.0)
    pos = off + rank - 1.0
    sub_l = lax.broadcasted_iota(I32, (V7X_LANES, 1), 0)
    slab_t = (jnp.where(sub_l == e1, w1, 0.0) + jnp.where(sub_l == e2, w2, 0.0)
              + jnp.where(sub_l == MOE_POS_LANE, pos, 0.0))
    slab_ref[...] = slab_t.T
    post_ref[...] = jnp.broadcast_to(pos, post_ref.shape)
    cnt_ref[...] = jnp.broadcast_to(info, cnt_ref.shape)


def _route(x1, mod_l, g, router_w, router_b, seq):
    n, d = x1.shape
    tm = min(1024, seq)
    tpb = seq // tm
    sl = V7X_SUBLANES
    rwh, rwl = _split_bf16(jnp.pad(router_w.T, ((0, V7X_LANES - N_EXPERTS), (0, 0))))
    tri = jnp.triu(jnp.ones((tm, tm), BF16))
    row = lambda width: pl.BlockSpec((tm, width), lambda i: (i, 0))
    full = lambda a: pl.BlockSpec(a.shape, lambda i: (0,) * a.ndim)
    rb = router_b.reshape(N_EXPERTS, 1)
    return pl.pallas_call(
        _route_kernel,
        grid=(n // tm,),
        in_specs=[row(d), pl.BlockSpec((1, 6, d), lambda i: (i // tpb, 0, 0)), full(g), full(rwh), full(rwl),
                  full(rb), full(tri)],
        out_specs=[row(d), row(V7X_LANES), pl.BlockSpec((sl, tm), lambda i: (0, i)),
                   pl.BlockSpec((sl, V7X_LANES), lambda i: (i, 0))],
        out_shape=[jax.ShapeDtypeStruct((n, d), BF16), jax.ShapeDtypeStruct((n, V7X_LANES), F32),
                   jax.ShapeDtypeStruct((sl, n), F32), jax.ShapeDtypeStruct((n // tm * sl, V7X_LANES), F32)],
        compiler_params=_cparams("parallel"),
        name="route",
    )(x1, mod_l, g, rwh, rwl, rb, tri)


def _moe_compact_kernel(goff_ref, gcnt_ref, h2_ref, slab_ref, post_ref, w1_ref, w3_ref, w2_ref, x1_ref, mod_ref,
                        fg_ref, o_ref, hc_ref, cc_ref, yacc_ref, *, final_norm):
    i = pl.program_id(0)
    e = pl.program_id(1)
    rows = hc_ref.shape[0]

    @pl.when(e == 0)
    def _():
        sel = jnp.where(lax.broadcasted_iota(I32, (rows, 1), 0).astype(F32) == post_ref[0:1, :], 1.0, 0.0).astype(BF16)
        hc_ref[...] = _dot(sel, h2_ref[...]).astype(BF16)
        sh, sl_ = _split_bf16(slab_ref[...])
        cc_ref[...] = _dot(sel, sh) + _dot(sel, sl_)
        yacc_ref[...] = jnp.zeros_like(yacc_ref)

    g = lax.div(e, EXP_PER_GROUP)
    off = goff_ref[i * N_GROUPS + g]
    nsteps = lax.div(gcnt_ref[i * N_GROUPS + g] + (MOE_CHUNK - 1), MOE_CHUNK)
    w1 = w1_ref[0, 0]
    w3 = w3_ref[0, 0]
    w2 = w2_ref[0, 0]

    def step(c, carry):
        r0 = pl.multiple_of(off + c * MOE_CHUNK, MOE_CHUNK)
        hg = hc_ref[pl.ds(r0, MOE_CHUNK), :]
        a = _dot(hg, w1)
        b = _dot(hg, w3)
        y = _dot((a * _sigmoid(a) * b).astype(BF16), w2)
        cg = cc_ref[pl.ds(r0, MOE_CHUNK), :]
        lane = lax.broadcasted_iota(I32, cg.shape, 1)
        yacc_ref[pl.ds(r0, MOE_CHUNK), :] += jnp.sum(jnp.where(lane == e, cg, 0.0), axis=1, keepdims=True) * y
        return carry

    lax.fori_loop(0, nsteps, step, 0)

    @pl.when(e == pl.num_programs(1) - 1)
    def _():
        pos = slab_ref[...][:, MOE_POS_LANE:MOE_POS_LANE + 1]
        back = jnp.where(lax.broadcasted_iota(I32, (1, rows), 1).astype(F32) == pos, 1.0, 0.0).astype(BF16)
        out = x1_ref[...] + mod_ref[0][5:6] * _dot(back, yacc_ref[...].astype(BF16))
        if final_norm:
            ms = jnp.mean(out * out, axis=-1, keepdims=True)
            out = out * lax.rsqrt(ms + NORM_EPS) * fg_ref[...]
        o_ref[...] = out


def _moe_compact(h2, slab, post, info, exp_w1, exp_w3, exp_w2, layer, x1, mod_l, final_g, seq, final_norm):
    n, d = x1.shape
    ne, f = exp_w1.shape[1], exp_w1.shape[3]
    tm = min(1024, seq)
    tpb = seq // tm
    rows = tm + N_GROUPS * MOE_CHUNK
    per_tile = info[::V7X_SUBLANES]
    gcnt = per_tile[:, 0:N_GROUPS].astype(I32).reshape(-1)
    goff = per_tile[:, N_GROUPS:2 * N_GROUPS].astype(I32).reshape(-1)
    row = lambda width: pl.BlockSpec((tm, width), lambda i, e, *_: (i, 0))
    return pl.pallas_call(
        functools.partial(_moe_compact_kernel, final_norm=final_norm),
        grid_spec=pltpu.PrefetchScalarGridSpec(
            num_scalar_prefetch=2,
            grid=(n // tm, ne),
            in_specs=[row(d), row(V7X_LANES), pl.BlockSpec((V7X_SUBLANES, tm), lambda i, e, *_: (0, i)),
                      pl.BlockSpec((1, 1, d, f), lambda i, e, *_: (layer, e, 0, 0)),
                      pl.BlockSpec((1, 1, d, f), lambda i, e, *_: (layer, e, 0, 0)),
                      pl.BlockSpec((1, 1, f, d), lambda i, e, *_: (layer, e, 0, 0)),
                      row(d), pl.BlockSpec((1, 6, d), lambda i, e, *_: (i // tpb, 0, 0)),
                      pl.BlockSpec((1, d), lambda i, e, *_: (0, 0))],
            out_specs=row(d),
            scratch_shapes=[pltpu.VMEM((rows, d), BF16), pltpu.VMEM((rows, V7X_LANES), F32),
                            pltpu.VMEM((rows, d), F32)]),
        out_shape=jax.ShapeDtypeStruct((n, d), F32),
        compiler_params=_cparams("parallel", "arbitrary"),
        name="moe_compact",
    )(goff, gcnt, h2, slab, post, exp_w1, exp_w3, exp_w2, x1, mod_l, final_g)


def kernel(x, c, positions, ada_w, ada_b, mix_norm_g, w_in, gate_w, gate_b, branch_w, out_w, s5_lam_re, s5_lam_im, s5_log_step, s5_b_re, s5_b_im, s5_c_re, s5_c_im, s5_d, s5_glu_w, s5_glu_b, rg_conv_w, rg_conv_b, rg_wr, rg_br, rg_wi, rg_bi, rg_lam, rw_mu, rw_w0, rw_w_up, rw_a0, rw_a_up, rw_g_up, rw_k_k, rw_k_a, rw_r_k, rw_gn_w, rw_gn_b, ffn_norm_g, router_w, router_b, exp_w1, exp_w3, exp_w2, final_norm_g):
    bsz, seq, d = x.shape
    depth = ada_w.shape[0]
    n = bsz * seq
    x2 = x.reshape(n, d)
    mod = _ada_mod(c, ada_w, ada_b)
    tabs = _rope_tables(positions)
    fg = final_norm_g.reshape(1, d)
    ew1, ew3, ew2 = exp_w1.astype(BF16), exp_w3.astype(BF16), exp_w2.astype(BF16)
    for l in range(depth):
        mod_l = mod[l]
        g1 = mix_norm_g[l].reshape(1, d)
        (u, q, k, v, iqh, iql, ikh, ikl, iw, rgx, rgg, rwf) = _inproj(x2, mod_l, g1, w_in[l], tabs, seq)
        s5p = _s5_params(s5_lam_re[l], s5_lam_im[l], s5_log_step[l], s5_b_re[l], s5_b_im[l], s5_c_re[l], s5_c_im[l])
        o_s5 = _s5_mixer(u, s5p, s5_d[l], s5_glu_w[l], s5_glu_b[l], bsz, seq)
        o_dsa = _dsa_mixer(q, k, v, iqh, iql, ikh, ikl, iw, bsz, seq)
        o_rg = _rglru_mixer(rgx, rgg, rg_conv_w[l], rg_conv_b[l], rg_wr[l], rg_br[l], rg_wi[l], rg_bi[l],
                            rg_lam[l], bsz, seq)
        o_rw = _rwkv_mixer(rwf, rw_mu[l], rw_w0[l], rw_w_up[l], rw_a0[l], rw_a_up[l], rw_g_up[l], rw_k_k[l],
                           rw_k_a[l], rw_r_k[l], rw_gn_w[l], rw_gn_b[l], bsz, seq)
        x1 = _merge(x2, mod_l, g1, (o_s5, o_dsa, o_rg, o_rw), gate_w[l], gate_b[l], branch_w[l], out_w[l], seq)
        h2, slab, post, info = _route(x1, mod_l, ffn_norm_g[l].reshape(1, d), router_w, router_b, seq)
        x2 = _moe_compact(h2, slab, post, info, ew1, ew3, ew2, l, x1, mod_l, fg, seq, l == depth - 1)
    return x2.reshape(bsz, seq, d)
```

```python
import functools
import math

import jax
import jax.numpy as jnp
from jax import lax
from jax.experimental import pallas as pl
from jax.experimental.pallas import tpu as pltpu

F32 = jnp.float32
BF16 = jnp.bfloat16
I32 = jnp.int32

W_BRANCH = 256
N_BRANCH = 4
NORM_EPS = 1e-6
S5_GROUP = 16
S5_GROUPS = W_BRANCH // S5_GROUP
S5_STATE = 64
ATT_HEADS = 4
ATT_DIM = W_BRANCH // ATT_HEADS
IDX_HEADS = 4
IDX_DIM = 32
TOPK_MAX = 256
ROPE_THETA = 10000.0
RG_BLOCKS = 4
RG_C = 8.0
CONV_WIDTH = 4
RW_HEADS = 4
RW_DIM = W_BRANCH // RW_HEADS
RW_DECAY_LORA = 32
RW_A_LORA = 32
RW_GATE_LORA = 64
RW_GN_EPS = 64e-5
RW_COLS = 3 * W_BRANCH + RW_DECAY_LORA + RW_A_LORA + RW_GATE_LORA
N_EXPERTS = 16
N_GROUPS = 4
EXP_PER_GROUP = N_EXPERTS // N_GROUPS
D_EXPERT = 512

V7X_SUBLANES = 8
V7X_LANES = 128
V7X_VMEM_LIMIT_BYTES = 56 * 1024 * 1024

INT_MIN = -(2 ** 31)
NEG_BIG = -1e30


def _cparams(*sem, flags=None):
    return pltpu.CompilerParams(dimension_semantics=sem, vmem_limit_bytes=V7X_VMEM_LIMIT_BYTES, flags=flags)


def _split_bf16(x):
    hi = x.astype(BF16)
    lo = (x - hi.astype(F32)).astype(BF16)
    return hi, lo


def _dot(a, b):
    return jnp.dot(a, b, preferred_element_type=F32)


def _dot_nt(a, b):
    return lax.dot_general(a, b, (((1,), (1,)), ((), ())), preferred_element_type=F32)


def _dot3(a, b):
    ah, al = _split_bf16(a)
    bh, bl = _split_bf16(b)
    return _dot(ah, bh) + _dot(ah, bl) + _dot(al, bh)


def _sigmoid(x):
    return 1.0 / (1.0 + jnp.exp(-x))


def _softplus(x):
    return jnp.maximum(x, 0.0) + jnp.log(1.0 + jnp.exp(-jnp.abs(x)))


def _gelu(x):
    c = math.sqrt(2.0 / math.pi)
    return 0.5 * x * (1.0 + jnp.tanh(c * (x + 0.044715 * (x * x * x))))


def _rot_half_cols(w, n_heads, dim):
    lead = w.shape[0]
    w4 = w.reshape(lead, n_heads, 2, dim // 2)
    return jnp.concatenate([-w4[:, :, 1:2], w4[:, :, 0:1]], axis=2).reshape(lead, n_heads * dim)


def _ada_kernel(c_ref, w_ref, b_ref, o_ref):
    c = c_ref[...]
    ca = c * _sigmoid(c)
    o_ref[0] = _dot3(ca, w_ref[0]) + b_ref[0]


def _ada_mod(c, ada_w, ada_b):
    depth, d, d6 = ada_w.shape
    bsz = c.shape[0]
    rows = -(-bsz // V7X_SUBLANES) * V7X_SUBLANES
    cp = jnp.zeros((rows, d), F32).at[:bsz].set(c)
    out = pl.pallas_call(
        _ada_kernel,
        grid=(depth, d6 // d),
        in_specs=[pl.BlockSpec((rows, d), lambda l, j: (0, 0)),
                  pl.BlockSpec((1, d, d), lambda l, j: (l, 0, j)),
                  pl.BlockSpec((1, 1, d), lambda l, j: (l, 0, j))],
        out_specs=pl.BlockSpec((1, rows, d), lambda l, j: (l, 0, j)),
        out_shape=jax.ShapeDtypeStruct((depth, rows, d6), F32),
        compiler_params=_cparams("parallel", "parallel"),
        name="ada_mod",
    )(cp, ada_w, ada_b.reshape(depth, 1, d6))
    return out[:, :bsz].reshape(depth, bsz, 6, d)


def _rope_kernel(pos_ref, fa_ref, fi_ref, ca_ref, sa_ref, ci_ref, si_ref):
    pos = pos_ref[...]
    ang_a = pos * fa_ref[...]
    ang_i = pos * fi_ref[...]
    ca_ref[...] = jnp.cos(ang_a)
    sa_ref[...] = jnp.sin(ang_a)
    ci_ref[...] = jnp.cos(ang_i)
    si_ref[...] = jnp.sin(ang_i)


def _rope_tables(positions):
    n = positions.size
    pos = positions.reshape(n, 1).astype(F32)

    def freq_row(dim):
        half = dim // 2
        inv = ROPE_THETA ** (-jnp.arange(half, dtype=F32) / half)
        return jnp.tile(inv, V7X_LANES // half).reshape(1, V7X_LANES)

    tm = min(n, 2048)
    row = pl.BlockSpec((tm, V7X_LANES), lambda i: (i, 0))
    frq = pl.BlockSpec((1, V7X_LANES), lambda i: (0, 0))
    shp = jax.ShapeDtypeStruct((n, V7X_LANES), F32)
    return pl.pallas_call(
        _rope_kernel,
        grid=(n // tm,),
        in_specs=[pl.BlockSpec((tm, 1), lambda i: (i, 0)), frq, frq],
        out_specs=[row, row, row, row],
        out_shape=[shp, shp, shp, shp],
        compiler_params=_cparams("parallel"),
        name="rope_tables",
    )(pos, freq_row(ATT_DIM), freq_row(IDX_DIM))


def _modulated_norm(x, g, scale, shift):
    ms = jnp.mean(x * x, axis=-1, keepdims=True)
    return (x * lax.rsqrt(ms + NORM_EPS)) * g * (1.0 + scale) + shift


def _inproj_kernel(x_ref, mod_ref, g_ref, wm_ref, wih_ref, wil_ref,
                   ca_ref, sa_ref, ci_ref, si_ref,
                   u_ref, q_ref, k_ref, v_ref, iqh_ref, iql_ref, ikh_ref, ikl_ref, iw_ref,
                   rgx_ref, rgg_ref, rw_ref):
    w = W_BRANCH
    mod = mod_ref[0]
    h = _modulated_norm(x_ref[...], g_ref[...], mod[1:2], mod[0:1])
    hh, hl = _split_bf16(h)
    main = _dot(hh, wm_ref[...])
    idx = _dot(hh, wih_ref[...]) + _dot(hh, wil_ref[...]) + _dot(hl, wih_ref[...])
    ca = jnp.concatenate([ca_ref[...], ca_ref[...]], axis=1)
    sa = jnp.concatenate([sa_ref[...], sa_ref[...]], axis=1)
    u_ref[...] = main[:, 0:w]
    q_ref[...] = (main[:, w:2 * w] * ca + main[:, 2 * w:3 * w] * sa).astype(BF16)
    k_ref[...] = (main[:, 3 * w:4 * w] * ca + main[:, 4 * w:5 * w] * sa).astype(BF16)
    v_ref[...] = main[:, 5 * w:6 * w].astype(BF16)
    rgx_ref[...] = main[:, 6 * w:7 * w]
    rgg_ref[...] = main[:, 7 * w:8 * w]
    rw_ref[...] = main[:, 8 * w:8 * w + RW_COLS]
    ci = ci_ref[...]
    si = si_ref[...]
    iq = idx[:, 0:128] * ci + idx[:, 128:256] * si
    ik = idx[:, 256:288] * ci[:, 0:IDX_DIM] + idx[:, 288:320] * si[:, 0:IDX_DIM]
    iqh, iql = _split_bf16(iq)
    ikh, ikl = _split_bf16(ik)
    iqh_ref[...] = iqh
    iql_ref[...] = iql
    ikh_ref[...] = ikh
    ikl_ref[...] = ikl
    iw_ref[...] = idx[:, 320:320 + IDX_HEADS]


def _inproj_weights(w_in):
    w = W_BRANCH
    o = 0
    parts = {}
    for name, width in (("u", w), ("q", w), ("k", w), ("v", w), ("iq", IDX_HEADS * IDX_DIM),
                        ("ik", IDX_DIM), ("iw", IDX_HEADS), ("rgx", w), ("rgg", w), ("rw", RW_COLS)):
        parts[name] = w_in[:, o:o + width]
        o += width
    wq = parts["q"] * (ATT_DIM ** -0.5 * math.log2(math.e))
    main = jnp.concatenate([parts["u"], wq, _rot_half_cols(wq, ATT_HEADS, ATT_DIM),
                            parts["k"], _rot_half_cols(parts["k"], ATT_HEADS, ATT_DIM),
                            parts["v"], parts["rgx"], parts["rgg"], parts["rw"]], axis=1)
    pad = (-main.shape[1]) % V7X_LANES
    main = jnp.pad(main, ((0, 0), (0, pad)))
    idx = jnp.concatenate([parts["iq"], _rot_half_cols(parts["iq"], IDX_HEADS, IDX_DIM),
                           parts["ik"], _rot_half_cols(parts["ik"], 1, IDX_DIM), parts["iw"]], axis=1)
    idx = jnp.pad(idx, ((0, 0), (0, 384 - idx.shape[1])))
    ih, il = _split_bf16(idx)
    return main.astype(BF16), ih, il


def _inproj(x2, mod_l, g, w_in, tabs, seq):
    n, d = x2.shape
    tm = min(512, seq)
    tpb = seq // tm
    wm, wih, wil = _inproj_weights(w_in)
    w = W_BRANCH
    row = lambda width: pl.BlockSpec((tm, width), lambda i: (i, 0))
    full = lambda a: pl.BlockSpec(a.shape, lambda i: (0,) * a.ndim)
    shp = lambda width, dt: jax.ShapeDtypeStruct((n, width), dt)
    tab = row(V7X_LANES)
    return pl.pallas_call(
        _inproj_kernel,
        grid=(n // tm,),
        in_specs=[row(d), pl.BlockSpec((1, 6, d), lambda i: (i // tpb, 0, 0)), full(g),
                  full(wm), full(wih), full(wil), tab, tab, tab, tab],
        out_specs=[row(w), row(w), row(w), row(w), row(128), row(128), row(IDX_DIM), row(IDX_DIM),
                   row(IDX_HEADS), row(w), row(w), row(RW_COLS)],
        out_shape=[shp(w, F32), shp(w, BF16), shp(w, BF16), shp(w, BF16), shp(128, BF16), shp(128, BF16),
                   shp(IDX_DIM, BF16), shp(IDX_DIM, BF16), shp(IDX_HEADS, F32),
                   shp(w, F32), shp(w, F32), shp(RW_COLS, F32)],
        compiler_params=_cparams("parallel"),
        name="inproj",
    )(x2, mod_l, g, wm, wih, wil, *tabs)


def _s5_prep_kernel(lre_r, lim_r, st_r, lre_c, lim_c, st_c, bre_ref, bim_ref,
                    pre_ref, pim_ref, bbre_ref, bbim_ref):
    kk = (lax.broadcasted_iota(I32, (V7X_SUBLANES, 1), 0) + 1).astype(F32)
    step = jnp.exp(st_r[...])
    mag = jnp.exp(kk * (lre_r[...] * step))
    ang = kk * (lim_r[...] * step)
    pre_ref[...] = mag * jnp.cos(ang)
    pim_ref[...] = mag * jnp.sin(ang)
    a = lre_c[...]
    b = lim_c[...]
    stc = jnp.exp(st_c[...])
    m1 = jnp.exp(a * stc)
    nr = m1 * jnp.cos(b * stc) - 1.0
    ni = m1 * jnp.sin(b * stc)
    den = a * a + b * b
    cr = (nr * a + ni * b) / den
    ci = (ni * a - nr * b) / den
    bre = bre_ref[...]
    bim = bim_ref[...]
    bbre_ref[...] = cr * bre - ci * bim
    bbim_ref[...] = cr * bim + ci * bre


def _block_diag(blocks):
    g, r, c = blocks.shape
    eye = jnp.eye(g, dtype=blocks.dtype)
    return (blocks[:, :, None, :] * eye[:, None, :, None]).reshape(g * r, g * c)


def _s5_params(lam_re, lam_im, log_step, b_re, b_im, c_re, c_im):
    g, p, c = b_re.shape
    ns = g * p
    st = jnp.repeat(log_step, p)
    rows = [a.reshape(1, ns) for a in (lam_re, lam_im, st)]
    cols = [a.reshape(ns, 1) for a in (lam_re, lam_im, st)]
    full = lambda a: pl.BlockSpec(a.shape, lambda: (0,) * a.ndim)
    args = rows + cols + [b_re.reshape(ns, c), b_im.reshape(ns, c)]
    pre, pim, bbre, bbim = pl.pallas_call(
        _s5_prep_kernel,
        in_specs=[full(a) for a in args],
        out_specs=[pl.BlockSpec((V7X_SUBLANES, ns), lambda: (0, 0))] * 2
        + [pl.BlockSpec((ns, c), lambda: (0, 0))] * 2,
        out_shape=[jax.ShapeDtypeStruct((V7X_SUBLANES, ns), F32)] * 2
        + [jax.ShapeDtypeStruct((ns, c), F32)] * 2,
        name="s5_prep",
    )(*args)
    bmat = jnp.concatenate([_block_diag(bbre.reshape(g, p, c).transpose(0, 2, 1)),
                            _block_diag(bbim.reshape(g, p, c).transpose(0, 2, 1))], axis=1)
    cmat = jnp.concatenate([_block_diag(c_re.transpose(0, 2, 1)),
                            -_block_diag(c_im.transpose(0, 2, 1))], axis=0)
    return pre, pim, bmat.astype(BF16), cmat.astype(BF16)


def _s5_kernel(u_ref, bmat_ref, cmat_ref, pre_ref, pim_ref, d_ref, gw_ref, gb_ref,
               o_ref, h_ref, carry_ref, *, chunk, ns):
    @pl.when(pl.program_id(1) == 0)
    def _():
        carry_ref[...] = jnp.zeros_like(carry_ref)

    u = u_ref[...]
    bu = _dot(u.astype(BF16), bmat_ref[...])
    nt = chunk // V7X_SUBLANES
    xre = bu[:, :ns].reshape(nt, V7X_SUBLANES, ns)
    xim = bu[:, ns:].reshape(nt, V7X_SUBLANES, ns)
    pre = pre_ref[...]
    pim = pim_ref[...]
    rowi = lax.broadcasted_iota(I32, (V7X_SUBLANES, 1), 0)
    for s in (1, 2, 4):
        cre = jnp.where(rowi >= s, pre[s - 1:s], 0.0)[None]
        cim = jnp.where(rowi >= s, pim[s - 1:s], 0.0)[None]
        sre = pltpu.roll(xre, s, 1)
        sim = pltpu.roll(xim, s, 1)
        xre, xim = xre + (cre * sre - cim * sim), xim + (cre * sim + cim * sre)
    cr = carry_ref[0:1, :]
    ci = carry_ref[1:2, :]
    for t in range(nt):
        hre = xre[t] + (pre * cr - pim * ci)
        him = xim[t] + (pre * ci + pim * cr)
        h_ref[t * V7X_SUBLANES:(t + 1) * V7X_SUBLANES, 0:ns] = hre
        h_ref[t * V7X_SUBLANES:(t + 1) * V7X_SUBLANES, ns:2 * ns] = him
        cr = hre[V7X_SUBLANES - 1:V7X_SUBLANES]
        ci = him[V7X_SUBLANES - 1:V7X_SUBLANES]
    carry_ref[0:1, :] = cr
    carry_ref[1:2, :] = ci
    y = _dot(h_ref[...].astype(BF16), cmat_ref[...]) + d_ref[...] * u
    y = _gelu(y)
    z = _dot(y.astype(BF16), gw_ref[...]) + gb_ref[...]
    o_ref[...] = y * _sigmoid(z)


def _s5_mixer(u, params, d_skip, glu_w, glu_b, bsz, seq):
    pre, pim, bmat, cmat = params
    n, w = u.shape
    ns = pre.shape[1]
    chunk = min(256, seq)
    cpb = seq // chunk
    full = lambda a: pl.BlockSpec(a.shape, lambda b, c: (0,) * a.ndim)
    row = pl.BlockSpec((chunk, w), lambda b, c: (b * cpb + c, 0))
    args = (u, bmat, cmat, pre, pim, d_skip.reshape(1, w), glu_w.astype(BF16), glu_b.reshape(1, w))
    return pl.pallas_call(
        functools.partial(_s5_kernel, chunk=chunk, ns=ns),
        grid=(bsz, cpb),
        in_specs=[row] + [full(a) for a in args[1:]],
        out_specs=row,
        out_shape=jax.ShapeDtypeStruct((n, w), F32),
        scratch_shapes=[pltpu.VMEM((chunk, 2 * ns), F32), pltpu.VMEM((V7X_SUBLANES, ns), F32)],
        compiler_params=_cparams("parallel", "arbitrary"),
        name="s5_mixer",
    )(*args)


def _rglru_kernel(x_ref, gate_ref, cw_ref, cb_ref, wr_ref, br_ref, wi_ref, bi_ref, lam_ref,
                  o_ref, tail_ref, carry_ref, *, chunk):
    w = W_BRANCH

    @pl.when(pl.program_id(1) == 0)
    def _():
        tail_ref[...] = jnp.zeros_like(tail_ref)
        carry_ref[...] = jnp.zeros_like(carry_ref)

    nt = chunk // V7X_SUBLANES
    x = x_ref[...]
    tail = tail_ref[...]
    rowc = lax.broadcasted_iota(I32, (chunk, 1), 0)
    cw = cw_ref[...]
    xc = cw[CONV_WIDTH - 1:CONV_WIDTH] * x + cb_ref[...]
    for sh in range(1, CONV_WIDTH):
        prev = jnp.broadcast_to(pltpu.roll(tail, sh, 0)[None], (nt, V7X_SUBLANES, w)).reshape(chunk, w)
        xs = jnp.where(rowc < sh, prev, pltpu.roll(x, sh, 0))
        xc = xc + cw[CONV_WIDTH - 1 - sh:CONV_WIDTH - sh] * xs
    tail_ref[...] = x[chunk - V7X_SUBLANES:chunk]
    xb = xc.astype(BF16)
    r = _sigmoid(_dot(xb, wr_ref[...]) + br_ref[...])
    i = _sigmoid(_dot(xb, wi_ref[...]) + bi_ref[...])
    log_a = (-RG_C) * r * _softplus(-lam_ref[...])
    a = jnp.exp(log_a)
    mult = jnp.sqrt(1.0 - jnp.exp(2.0 * log_a))
    b = mult * i * xc
    a3 = a.reshape(nt, V7X_SUBLANES, w)
    b3 = b.reshape(nt, V7X_SUBLANES, w)
    rowi = lax.broadcasted_iota(I32, (1, V7X_SUBLANES, 1), 1)
    for s in (1, 2, 4):
        keep = rowi >= s
        ash = jnp.where(keep, pltpu.roll(a3, s, 1), 1.0)
        bsh = jnp.where(keep, pltpu.roll(b3, s, 1), 0.0)
        b3 = b3 + a3 * bsh
        a3 = a3 * ash
    h = carry_ref[0:1, :]
    gate = _gelu(gate_ref[...])
    for t in range(nt):
        ht = b3[t] + a3[t] * h
        o_ref[t * V7X_SUBLANES:(t + 1) * V7X_SUBLANES, :] = ht * gate[t * V7X_SUBLANES:(t + 1) * V7X_SUBLANES]
        h = ht[V7X_SUBLANES - 1:V7X_SUBLANES]
    carry_ref[0:1, :] = h


def _rglru_mixer(xr, gate, conv_w, conv_b, w_r, b_r, w_i, b_i, lam, bsz, seq):
    n, w = xr.shape
    chunk = min(256, seq)
    cpb = seq // chunk
    full = lambda a: pl.BlockSpec(a.shape, lambda b, c: (0,) * a.ndim)
    row = pl.BlockSpec((chunk, w), lambda b, c: (b * cpb + c, 0))
    args = (xr, gate, conv_w, conv_b.reshape(1, w), _block_diag(w_r).astype(BF16), b_r.reshape(1, w),
            _block_diag(w_i).astype(BF16), b_i.reshape(1, w), lam.reshape(1, w))
    return pl.pallas_call(
        functools.partial(_rglru_kernel, chunk=chunk),
        grid=(bsz, cpb),
        in_specs=[row, row] + [full(a) for a in args[2:]],
        out_specs=row,
        out_shape=jax.ShapeDtypeStruct((n, w), F32),
        scratch_shapes=[pltpu.VMEM((V7X_SUBLANES, w), F32), pltpu.VMEM((V7X_SUBLANES, w), F32)],
        compiler_params=_cparams("parallel", "arbitrary"),
        name="rglru_mixer",
    )(*args)


RW_CHUNK = 64


def _segsum(x, ones_bf16):
    hi, lo = _split_bf16(x)
    return _dot(hi, ones_bf16) + _dot(lo, ones_bf16)


def _stack_heads(x, head_of_lane):
    return jnp.concatenate([jnp.where(head_of_lane == h, x, 0.0) for h in range(RW_HEADS)], axis=0)


def _rwkv_chunk_kernel(f_ref, fprev_ref, mu_ref, w0_ref, wup_ref, a0_ref, aup_ref, gup_ref, kk_ref, ka_ref,
                       rk_ref, ones_ref, rt2_ref, y1_ref, m_ref, s1_ref, pl_ref, g_ref, bonus_ref,
                       kt_s, rt_s, kib_s, bib_s, vsb_s, vst_s, kip_s, bip_s, a0_s, a1_s, a2_s, avk_s, bbk_s, bvk_s,
                       pw_s, kt2_s, kt2t_s, u0_s, u0t_s, tinv_s, *, bsz, chunk):
    w = W_BRANCH
    sl = V7X_SUBLANES
    n4 = RW_HEADS * chunk
    first = pl.program_id(0) == 0
    ones = ones_ref[...]
    rowc = lax.broadcasted_iota(I32, (chunk, 1), 0)
    head_of_lane = lax.broadcasted_iota(I32, (1, w), 1) // RW_DIM
    ri = lax.broadcasted_iota(I32, (n4, n4), 0)
    ci = lax.broadcasted_iota(I32, (n4, n4), 1)
    strict = ci < ri
    incl = ci <= ri
    same16 = (ri // 16) == (ci // 16)
    same32 = (ri // 32) == (ci // 32)
    eye = (ri == ci).astype(F32)
    tri_c = (lax.broadcasted_iota(I32, (chunk, chunk), 1)
             <= lax.broadcasted_iota(I32, (chunk, chunk), 0)).astype(BF16)

    for b in range(bsz):
        f = f_ref[b]
        tail = jnp.where(first, 0.0, fprev_ref[b])
        prev = jnp.where(rowc < 1,
                         jnp.broadcast_to(pltpu.roll(tail, 1, 0)[None], (chunk // sl, sl, RW_COLS)).reshape(chunk, RW_COLS),
                         pltpu.roll(f, 1, 0))
        f = f + (prev - f) * mu_ref[...]
        r = f[:, 0:w]
        k = f[:, w:2 * w]
        v = f[:, 2 * w:3 * w]
        wd = f[:, 3 * w:3 * w + RW_DECAY_LORA]
        ad = f[:, 3 * w + RW_DECAY_LORA:3 * w + RW_DECAY_LORA + RW_A_LORA]
        gd = f[:, 3 * w + RW_DECAY_LORA + RW_A_LORA:RW_COLS]
        wlog = -_softplus(-(w0_ref[...] + _dot(jnp.tanh(wd).astype(BF16), wup_ref[...]))) - 0.5
        logd = -jnp.exp(wlog)
        a = _sigmoid(a0_ref[...] + _dot(ad.astype(BF16), aup_ref[...]))
        g = _dot(_sigmoid(gd).astype(BF16), gup_ref[...])
        kk = k * kk_ref[...]
        kk = kk / jnp.maximum(jnp.sqrt(_segsum(kk * kk, ones)), 1e-12)
        k2 = k * (1.0 + (a - 1.0) * ka_ref[...])
        beta = kk * a
        l1 = logd.astype(BF16)
        rem = logd - l1.astype(F32)
        l2 = rem.astype(BF16)
        l3 = (rem - l2.astype(F32)).astype(BF16)
        cum = _dot(tri_c, l1) + _dot(tri_c, l2) + _dot(tri_c, l3)
        p = jnp.exp(cum)
        pinv = jnp.exp(-cum)
        kt = _stack_heads(kk * jnp.exp(cum - logd), head_of_lane).astype(BF16)
        rt = _stack_heads(r * p, head_of_lane).astype(BF16)
        ki = _stack_heads(k2 * pinv, head_of_lane)
        bi = _stack_heads(beta * pinv, head_of_lane)
        vs = _stack_heads(v, head_of_lane)
        pl_row = p[chunk - 1:chunk]
        kt_s[b] = kt
        rt_s[b] = rt
        kib_s[b] = ki.astype(BF16)
        bib_s[b] = bi.astype(BF16)
        vsb_s[b] = vs.astype(BF16)
        vst_s[b] = vs.T.astype(BF16)
        kip_s[b] = (ki * pl_row).astype(BF16)
        bip_s[b] = (bi * pl_row).astype(BF16)
        pl_ref[b, 0] = jnp.broadcast_to(pl_row, (sl, w))
        g_ref[b] = g
        bonus_ref[b] = _segsum(r * k2 * rk_ref[...], ones) * v

    for b in range(bsz):
        kt, rt, kib, bib = kt_s[b], rt_s[b], kib_s[b], bib_s[b]
        a_bk = jnp.where(strict, _dot_nt(kt, bib), 0.0)
        a0 = jnp.where(same16, a_bk, 0.0)
        a0_s[b] = a0.astype(BF16)
        a1_s[b] = jnp.where(same32 & jnp.logical_not(same16), a_bk, 0.0).astype(BF16)
        a2_s[b] = jnp.where(same32, 0.0, a_bk).astype(BF16)
        tinv_s[b] = eye - a0
        avk_s[b] = jnp.where(strict, _dot_nt(kt, kib), 0.0).astype(BF16)
        bbk_s[b] = jnp.where(incl, _dot_nt(rt, bib), 0.0).astype(BF16)
        bvk_s[b] = jnp.where(incl, _dot_nt(rt, kib), 0.0).astype(BF16)
    for b in range(bsz):
        pw_s[b] = _dot(a0_s[b], a0_s[b]).astype(BF16)
    for it in range(3):
        for b in range(bsz):
            tinv_s[b] = tinv_s[b] + _dot(tinv_s[b].astype(BF16), pw_s[b])
        if it < 2:
            for b in range(bsz):
                pw_s[b] = _dot(pw_s[b], pw_s[b]).astype(BF16)
    for off_s in (a1_s, a2_s):
        for b in range(bsz):
            pw_s[b] = _dot(tinv_s[b].astype(BF16), off_s[b]).astype(BF16)
        for b in range(bsz):
            tinv_s[b] = tinv_s[b] - _dot(pw_s[b], tinv_s[b].astype(BF16))
    for b in range(bsz):
        kt2 = _dot(tinv_s[b].astype(BF16), kt_s[b])
        kt2_s[b] = kt2.astype(BF16)
        kt2t_s[b] = kt2.T.astype(BF16)
        pw_s[b] = _dot(avk_s[b], vsb_s[b]).astype(BF16)
    for b in range(bsz):
        u0 = _dot(tinv_s[b].astype(BF16), pw_s[b])
        u0_s[b] = u0.astype(BF16)
        u0t_s[b] = u0.T.astype(BF16)
    for b in range(bsz):
        bbk = bbk_s[b]
        y1s = _dot(bvk_s[b], vsb_s[b]) - _dot(bbk, u0_s[b])
        y1 = y1s[0:chunk]
        for h in range(1, RW_HEADS):
            y1 = y1 + y1s[h * chunk:(h + 1) * chunk]
        y1_ref[b] = y1
        rt2_ref[b, 0] = (rt_s[b].astype(F32) - _dot(bbk, kt2_s[b])).astype(BF16)
        m_ref[b, 0] = _dot(kt2t_s[b], bip_s[b]).astype(BF16)
        s1_ref[b, 0] = _dot(vst_s[b], kip_s[b]) - _dot(u0t_s[b], bip_s[b])


def _rwkv_state_kernel(rt2_ref, y1_ref, m_ref, s1_ref, pl_ref, g_ref, bonus_ref, gnw_ref, gnb_ref, ones_ref,
                       o_ref, state_ref, *, bsz, chunk):
    @pl.when(pl.program_id(0) == 0)
    def _():
        state_ref[...] = jnp.zeros_like(state_ref)

    ones = ones_ref[...]
    ys_all, yc_all = [], []
    for b in range(bsz):
        st = state_ref[b]
        stb = st.astype(BF16)
        ys = _dot_nt(rt2_ref[b, 0], stb)
        y = y1_ref[b] + ys[0:chunk]
        for h in range(1, RW_HEADS):
            y = y + ys[h * chunk:(h + 1) * chunk]
        ys_all.append(y)
        state_ref[b] = st * pl_ref[b, 0][0:1] + s1_ref[b, 0] - _dot(stb, m_ref[b, 0])
    for b in range(bsz):
        yc_all.append(ys_all[b] - _segsum(ys_all[b], ones) * (1.0 / RW_DIM))
    for b in range(bsz):
        yc = yc_all[b]
        var = _segsum(yc * yc, ones) * (1.0 / RW_DIM)
        yn = yc * lax.rsqrt(var + RW_GN_EPS) * gnw_ref[...] + gnb_ref[...]
        o_ref[b] = (yn + bonus_ref[b]) * g_ref[b]


def _rwkv_mixer(feat, mu, w0, w_up, a0, a_up, g_up, k_k, k_a, r_k, gn_w, gn_b, bsz, seq):
    w = W_BRANCH
    sl = V7X_SUBLANES
    chunk = min(RW_CHUNK, seq)
    nch = seq // chunk
    n4 = RW_HEADS * chunk
    f3 = feat.reshape(bsz, seq, RW_COLS)
    row1 = lambda a: a.reshape(1, -1)
    ones = _block_diag(jnp.ones((RW_HEADS, RW_DIM, RW_DIM), F32)).astype(BF16)
    full = lambda a: pl.BlockSpec(a.shape, lambda c: (0,) * a.ndim)
    tok = lambda width: pl.BlockSpec((bsz, chunk, width), lambda c: (0, c, 0))
    per_chunk = lambda rows, cols: pl.BlockSpec((bsz, 1, rows, cols), lambda c: (0, c, 0, 0))
    a_args = (row1(mu), row1(w0), w_up.astype(BF16), row1(a0), a_up.astype(BF16), g_up.astype(BF16),
              row1(k_k), row1(k_a), row1(r_k), ones)
    mids = pl.pallas_call(
        functools.partial(_rwkv_chunk_kernel, bsz=bsz, chunk=chunk),
        grid=(nch,),
        in_specs=[tok(RW_COLS),
                  pl.BlockSpec((bsz, sl, RW_COLS), lambda c: (0, jnp.maximum(c * (chunk // sl) - 1, 0), 0))]
        + [full(a) for a in a_args],
        out_specs=[per_chunk(n4, w), tok(w), per_chunk(w, w), per_chunk(w, w), per_chunk(sl, w), tok(w), tok(w)],
        out_shape=[jax.ShapeDtypeStruct((bsz, nch, n4, w), BF16), jax.ShapeDtypeStruct((bsz, seq, w), F32),
                   jax.ShapeDtypeStruct((bsz, nch, w, w), BF16), jax.ShapeDtypeStruct((bsz, nch, w, w), F32),
                   jax.ShapeDtypeStruct((bsz, nch, sl, w), F32), jax.ShapeDtypeStruct((bsz, seq, w), F32),
                   jax.ShapeDtypeStruct((bsz, seq, w), F32)],
        scratch_shapes=[pltpu.VMEM((bsz, n4, w), BF16)] * 19 + [pltpu.VMEM((bsz, n4, w), F32)],
        compiler_params=_cparams("parallel"),
        name="rwkv_chunks",
    )(f3, f3, *a_args)
    b_args = (row1(gn_w), row1(gn_b), ones)
    out = pl.pallas_call(
        functools.partial(_rwkv_state_kernel, bsz=bsz, chunk=chunk),
        grid=(nch,),
        in_specs=[per_chunk(n4, w), tok(w), per_chunk(w, w), per_chunk(w, w), per_chunk(sl, w), tok(w), tok(w)]
        + [full(a) for a in b_args],
        out_specs=tok(w),
        out_shape=jax.ShapeDtypeStruct((bsz, seq, w), F32),
        scratch_shapes=[pltpu.VMEM((bsz, w, w), F32)],
        compiler_params=_cparams("arbitrary"),
        name="rwkv_state",
    )(*mids, *b_args)
    return out.reshape(bsz * seq, w)


DSA_TQ = 256
DSA_KC = 512


DSA_VROWS = ATT_DIM + 16
KEY_NEG_INF = INT_MIN + 0x7FFFFF


def _ordered_to_f32(key):
    return lax.bitcast_convert_type(jnp.where(key >= 0, key, key ^ 0x7FFFFFFF), F32)


def _dsa_kernel(*refs, seq, tq, kc, topk):
    v_ref, vx_ref = refs[2], refs[-1]
    hd = ATT_DIM

    def prep(c, carry):
        s0 = pl.multiple_of(c * kc, kc)
        vt = v_ref[pl.ds(s0, kc), :].astype(F32).T
        for h in range(ATT_HEADS):
            vx_ref[c, h, 0:hd, :] = vt[h * hd:(h + 1) * hd].astype(BF16)
            vx_ref[c, h, hd:DSA_VROWS, :] = jnp.ones((DSA_VROWS - hd, kc), BF16)
        return carry

    lax.fori_loop(0, seq // kc, prep, 0)

    def block(i, carry):
        _dsa_block(i, *refs, seq=seq, tq=tq, kc=kc, topk=topk)
        return carry

    lax.fori_loop(0, seq // tq, block, 0)


def _dsa_block(i, q_ref, k_ref, v_ref, iqh_ref, iql_ref, ik3_ref, iwT_ref, o_ref, sc_ref, lga_ref, lgb_ref, bias_ref,
               p_ref, acc_ref, vx_ref, *, seq, tq, kc, topk):
    nchunks = lax.div((i + 1) * tq + (kc - 1), kc)
    qpos = i * tq + lax.broadcasted_iota(I32, (1, tq), 1)
    rows = lax.broadcasted_iota(I32, (kc, 1), 0)
    wT = iwT_ref[0, i]
    q0 = pl.multiple_of(i * tq, tq)
    iqh_t = iqh_ref[pl.ds(q0, tq), :].astype(F32).T
    iql_t = iql_ref[pl.ds(q0, tq), :].astype(F32).T
    iq3 = jnp.concatenate(
        [jnp.concatenate([t[h * IDX_DIM:(h + 1) * IDX_DIM] for t in (iqh_t, iql_t, iqh_t)], axis=0)
         for h in range(IDX_HEADS)], axis=1).astype(BF16)

    def score_body(c, carry):
        s0 = pl.multiple_of(c * kc, kc)
        d = _dot(ik3_ref[pl.ds(s0, kc), :], iq3)
        sc = wT[0:1] * jnp.maximum(d[:, 0:tq], 0.0)
        for h in range(1, IDX_HEADS):
            sc = sc + wT[h:h + 1] * jnp.maximum(d[:, h * tq:(h + 1) * tq], 0.0)
        sc_ref[pl.ds(s0, kc), :] = jnp.where(s0 + rows <= qpos, sc, -jnp.inf)
        return carry

    lax.fori_loop(0, nchunks, score_body, 0)

    def count(ind_fn):
        def body(c, acc):
            s0 = pl.multiple_of(c * kc, kc)
            ind = ind_fn(sc_ref[pl.ds(s0, kc), :], s0 + rows)
            return acc + ind.reshape(kc // V7X_SUBLANES, V7X_SUBLANES, tq).sum(axis=0)
        acc = lax.fori_loop(0, nchunks, body, jnp.zeros((V7X_SUBLANES, tq), I32))
        return jnp.sum(acc, axis=0, keepdims=True)

    c0 = count(lambda s, idx: jnp.where(s >= 0.0, 1, 0))
    ans = jnp.where(c0 >= topk, 0, INT_MIN).astype(I32)

    def bit_body(j, ans):
        cand = ans | lax.shift_left(jnp.int32(1), 30 - j)
        cf = _ordered_to_f32(cand)
        cnt = count(lambda s, idx: jnp.where(s >= cf, 1, 0))
        return jnp.where(cand <= KEY_NEG_INF, cand, jnp.where(cnt >= topk, cand, ans))

    thr = _ordered_to_f32(lax.fori_loop(0, 31, bit_body, ans))
    cnt_gt = count(lambda s, idx: jnp.where(s > thr, 1, 0))
    need = (topk - cnt_gt).astype(F32)

    tri = (lax.broadcasted_iota(I32, (kc, kc), 1) <= lax.broadcasted_iota(I32, (kc, kc), 0)).astype(BF16)

    def tie_body(c, seen):
        s0 = pl.multiple_of(c * kc, kc)
        sch = sc_ref[pl.ds(s0, kc), :]
        eq = sch == thr
        rank = _dot(tri, jnp.where(eq, 1.0, 0.0).astype(BF16)) + seen
        sc_ref[pl.ds(s0, kc), :] = jnp.where(eq, jnp.where(rank > need, -jnp.inf, sch), sch)
        return rank[kc - 1:kc]

    lax.fori_loop(0, nchunks, tie_body, jnp.zeros((1, tq), F32))
    thr = jnp.maximum(thr, jnp.finfo(F32).min)

    q_all = q_ref[pl.ds(q0, tq), :].astype(F32).T.astype(BF16)
    hd = ATT_DIM

    def logits_into(buf_ref, c, live):
        s0 = pl.multiple_of(c * kc, kc)
        sch = sc_ref[pl.ds(s0, kc), :]
        dead = jnp.where(live, 0.0, NEG_BIG)
        bias_ref[...] = jnp.where(sch >= thr, dead, NEG_BIG)
        for h in range(ATT_HEADS):
            buf_ref[h] = _dot(k_ref[pl.ds(s0, kc), h * hd:(h + 1) * hd], q_all[h * hd:(h + 1) * hd]) + bias_ref[...]

    def softmax_step(buf_ref, c, state):
        for h in range(ATT_HEADS):
            m = state[h]
            part = buf_ref[h].reshape(kc // V7X_SUBLANES, V7X_SUBLANES, tq).max(axis=0)
            mn = jnp.maximum(m, jnp.max(part, axis=0, keepdims=True))
            state[h] = mn
            p_ref[h] = jnp.exp2(buf_ref[h] - mn).astype(BF16)
            acc_ref[h] = jnp.exp2(m - mn) * acc_ref[h] + _dot(vx_ref[c, h], p_ref[h])

    last = nchunks - 1
    acc_ref[...] = jnp.zeros_like(acc_ref)
    logits_into(lga_ref, 0, True)

    def att_body(t, carry):
        state = list(carry)
        c0 = 2 * t
        c1 = jnp.minimum(c0 + 1, last)
        logits_into(lgb_ref, c1, c0 + 1 <= last)
        softmax_step(lga_ref, c0, state)
        logits_into(lga_ref, jnp.minimum(c0 + 2, last), True)
        softmax_step(lgb_ref, c1, state)
        return tuple(state)

    init = (jnp.full((1, tq), NEG_BIG, F32),) * ATT_HEADS
    lax.fori_loop(0, lax.div(nchunks + 1, 2), att_body, init)
    for h in range(ATT_HEADS):
        acc = acc_ref[h]
        o_ref[0, i, h * hd:(h + 1) * hd, :] = acc[0:hd] / acc[hd:hd + 1]


def _dsa_mixer(q, k, v, iqh, iql, ikh, ikl, iw, bsz, seq):
    w = W_BRANCH
    tq = min(DSA_TQ, seq)
    kc = min(DSA_KC, seq)
    nq = seq // tq
    nc = seq // kc
    topk = min(TOPK_MAX, seq // 4)
    assert kc >= topk
    ik3 = jnp.concatenate([ikh, ikh, ikl], axis=1)
    iwT = iw.reshape(bsz, nq, tq, IDX_HEADS).transpose(0, 1, 3, 2)
    per_seq = lambda width: pl.BlockSpec((seq, width), lambda b: (b, 0))
    out = pl.pallas_call(
        functools.partial(_dsa_kernel, seq=seq, tq=tq, kc=kc, topk=topk),
        grid=(bsz,),
        in_specs=[per_seq(w), per_seq(w), per_seq(w), per_seq(IDX_HEADS * IDX_DIM), per_seq(IDX_HEADS * IDX_DIM),
                  per_seq(3 * IDX_DIM), pl.BlockSpec((1, nq, IDX_HEADS, tq), lambda b: (b, 0, 0, 0))],
        out_specs=pl.BlockSpec((1, nq, w, tq), lambda b: (b, 0, 0, 0)),
        out_shape=jax.ShapeDtypeStruct((bsz, nq, w, tq), F32),
        scratch_shapes=[pltpu.VMEM((seq, tq), F32), pltpu.VMEM((ATT_HEADS, kc, tq), F32),
                        pltpu.VMEM((ATT_HEADS, kc, tq), F32), pltpu.VMEM((kc, tq), F32),
                        pltpu.VMEM((ATT_HEADS, kc, tq), BF16), pltpu.VMEM((ATT_HEADS, DSA_VROWS, tq), F32),
                        pltpu.VMEM((nc, ATT_HEADS, DSA_VROWS, kc), BF16)],
        compiler_params=_cparams("parallel"),
        name="dsa_mixer",
    )(q, k, v, iqh, iql, ik3, iwT)
    return out


def _merge_kernel(x_ref, mod_ref, g_ref, o0_ref, o1_ref, o2_ref, o3_ref, gw_ref, gb_ref, bw_ref, ow_ref, x1_ref):
    d = x_ref.shape[1]
    mod = mod_ref[0]
    x = x_ref[...]
    hb = _modulated_norm(x, g_ref[...], mod[1:2], mod[0:1]).astype(BF16)
    mixed = jnp.zeros(x.shape, F32)
    o_dsa = jnp.concatenate([o1_ref[0, j].T for j in range(o1_ref.shape[1])], axis=0)
    for n, o in enumerate((o0_ref[...], o_dsa, o2_ref[...], o3_ref[...])):
        gate = _sigmoid(_dot(hb, gw_ref[:, n * d:(n + 1) * d]) + gb_ref[:, n * d:(n + 1) * d])
        mixed = mixed + gate * _dot(o.astype(BF16), bw_ref[n])
    x1_ref[...] = x + mod[2:3] * _dot(mixed.astype(BF16), ow_ref[...])


def _merge(x2, mod_l, g, branches, gate_w, gate_b, branch_w, out_w, seq):
    n, d = x2.shape
    w = W_BRANCH
    tm = min(512, seq)
    tpb = seq // tm
    row = lambda width: pl.BlockSpec((tm, width), lambda i: (i, 0))
    full = lambda a: pl.BlockSpec(a.shape, lambda i: (0,) * a.ndim)
    args = (gate_w.astype(BF16), gate_b.reshape(1, -1), branch_w.astype(BF16), out_w.astype(BF16))
    tq = branches[1].shape[-1]
    qpt = tm // tq
    dsa_spec = pl.BlockSpec((1, qpt, w, tq), lambda i: (i // tpb, i % tpb, 0, 0))
    return pl.pallas_call(
        _merge_kernel,
        grid=(n // tm,),
        in_specs=[row(d), pl.BlockSpec((1, 6, d), lambda i: (i // tpb, 0, 0)), full(g)]
        + [row(w), dsa_spec, row(w), row(w)] + [full(a) for a in args],
        out_specs=row(d),
        out_shape=jax.ShapeDtypeStruct((n, d), F32),
        compiler_params=_cparams("parallel"),
        name="merge",
    )(x2, mod_l, g, *branches, *args)


def _route_combine(scores, rb):
    biased = scores + rb
    col = lambda a, e: a[e:e + 1, :]
    npg = EXP_PER_GROUP
    gs = []
    for g in range(N_GROUPS):
        best = None
        for j1 in range(npg):
            for j2 in range(j1 + 1, npg):
                s = col(biased, npg * g + j1) + col(biased, npg * g + j2)
                best = s if best is None else jnp.maximum(best, s)
        gs.append(best)
    bg = jnp.zeros(gs[0].shape, I32)
    bv = gs[0]
    for g in range(1, N_GROUPS):
        upd = gs[g] > bv
        bv = jnp.where(upd, gs[g], bv)
        bg = jnp.where(upd, g, bg)
    bsel, ssel = [], []
    for j in range(npg):
        bj, sj = col(biased, j), col(scores, j)
        for g in range(1, N_GROUPS):
            bj = jnp.where(bg == g, col(biased, npg * g + j), bj)
            sj = jnp.where(bg == g, col(scores, npg * g + j), sj)
        bsel.append(bj)
        ssel.append(sj)
    i1, v1, s1 = jnp.zeros(bg.shape, I32), bsel[0], ssel[0]
    for j in range(1, npg):
        upd = bsel[j] > v1
        v1 = jnp.where(upd, bsel[j], v1)
        s1 = jnp.where(upd, ssel[j], s1)
        i1 = jnp.where(upd, j, i1)
    i2, v2, s2 = jnp.zeros(bg.shape, I32), jnp.where(i1 == 0, -jnp.inf, bsel[0]), ssel[0]
    for j in range(1, npg):
        cand = jnp.where(i1 == j, -jnp.inf, bsel[j])
        upd = cand > v2
        v2 = jnp.where(upd, cand, v2)
        s2 = jnp.where(upd, ssel[j], s2)
        i2 = jnp.where(upd, j, i2)
    tot = s1 + s2
    return bg, npg * bg + i1, s1 / tot, npg * bg + i2, s2 / tot


MOE_CHUNK = 256
MOE_POS_LANE = N_EXPERTS


def _route_kernel(x1_ref, mod_ref, g_ref, rwh_ref, rwl_ref, rb_ref, tri_ref, h2_ref, slab_ref, post_ref, cnt_ref):
    mod = mod_ref[0]
    h2 = _modulated_norm(x1_ref[...], g_ref[...], mod[4:5], mod[3:4])
    h2_ref[...] = h2.astype(BF16)
    hh, hl = _split_bf16(h2)
    tm = h2.shape[0]
    logits = _dot_nt(rwh_ref[...], hh) + _dot_nt(rwl_ref[...], hh) + _dot_nt(rwh_ref[...], hl)
    bg, e1, w1, e2, w2 = _route_combine(_sigmoid(logits[0:N_EXPERTS]), rb_ref[...])
    sub = lax.broadcasted_iota(I32, (V7X_SUBLANES, 1), 0)
    member = jnp.where(sub == bg, 1.0, 0.0)
    upto = _dot(member.astype(BF16), tri_ref[...])
    rank = jnp.sum(member * upto, axis=0, keepdims=True)
    sizes = upto[:, tm - 1:tm]
    padded = jnp.floor((sizes + (MOE_CHUNK - 1.0)) * (1.0 / MOE_CHUNK)) * MOE_CHUNK
    offs = [jnp.zeros((1, 1), F32)]
    for g in range(1, N_GROUPS):
        offs.append(offs[-1] + padded[g - 1:g])
    lane = lax.broadcasted_iota(I32, (1, V7X_LANES), 1)
    off = offs[0]
    info = jnp.where(lane == 0, sizes[0:1], 0.0)
    for g in range(1, N_GROUPS):
        off = jnp.where(bg == g, offs[g], off)
        info = info + jnp.where(lane == g, sizes[g:g + 1], 0.0) + jnp.where(lane == N_GROUPS + g, offs[g], 0.0)
    pos = off + rank - 1.0
    sub_l = lax.broadcasted_iota(I32, (V7X_LANES, 1), 0)
    slab_t = (jnp.where(sub_l == e1, w1, 0.0) + jnp.where(sub_l == e2, w2, 0.0)
              + jnp.where(sub_l == MOE_POS_LANE, pos, 0.0))
    slab_ref[...] = slab_t.T
    post_ref[...] = jnp.broadcast_to(pos, post_ref.shape)
    cnt_ref[...] = jnp.broadcast_to(info, cnt_ref.shape)


def _route(x1, mod_l, g, router_w, router_b, seq):
    n, d = x1.shape
    tm = min(1024, seq)
    tpb = seq // tm
    sl = V7X_SUBLANES
    rwh, rwl = _split_bf16(jnp.pad(router_w.T, ((0, V7X_LANES - N_EXPERTS), (0, 0))))
    tri = jnp.triu(jnp.ones((tm, tm), BF16))
    row = lambda width: pl.BlockSpec((tm, width), lambda i: (i, 0))
    full = lambda a: pl.BlockSpec(a.shape, lambda i: (0,) * a.ndim)
    rb = router_b.reshape(N_EXPERTS, 1)
    return pl.pallas_call(
        _route_kernel,
        grid=(n // tm,),
        in_specs=[row(d), pl.BlockSpec((1, 6, d), lambda i: (i // tpb, 0, 0)), full(g), full(rwh), full(rwl),
                  full(rb), full(tri)],
        out_specs=[row(d), row(V7X_LANES), pl.BlockSpec((sl, tm), lambda i: (0, i)),
                   pl.BlockSpec((sl, V7X_LANES), lambda i: (i, 0))],
        out_shape=[jax.ShapeDtypeStruct((n, d), BF16), jax.ShapeDtypeStruct((n, V7X_LANES), F32),
                   jax.ShapeDtypeStruct((sl, n), F32), jax.ShapeDtypeStruct((n // tm * sl, V7X_LANES), F32)],
        compiler_params=_cparams("parallel"),
        name="route",
    )(x1, mod_l, g, rwh, rwl, rb, tri)


def _moe_compact_kernel(goff_ref, gcnt_ref, h2_ref, slab_ref, post_ref, w1_ref, w3_ref, w2_ref, x1_ref, mod_ref,
                        fg_ref, o_ref, hc_ref, cc_ref, yacc_ref, *, final_norm):
    i = pl.program_id(0)
    e = pl.program_id(1)
    rows = hc_ref.shape[0]

    @pl.when(e == 0)
    def _():
        sel = jnp.where(lax.broadcasted_iota(I32, (rows, 1), 0).astype(F32) == post_ref[0:1, :], 1.0, 0.0).astype(BF16)
        hc_ref[...] = _dot(sel, h2_ref[...]).astype(BF16)
        sh, sl_ = _split_bf16(slab_ref[...])
        cc_ref[...] = _dot(sel, sh) + _dot(sel, sl_)
        yacc_ref[...] = jnp.zeros_like(yacc_ref)

    g = lax.div(e, EXP_PER_GROUP)
    off = goff_ref[i * N_GROUPS + g]
    nsteps = lax.div(gcnt_ref[i * N_GROUPS + g] + (MOE_CHUNK - 1), MOE_CHUNK)
    w1 = w1_ref[0, 0]
    w3 = w3_ref[0, 0]
    w2 = w2_ref[0, 0]

    def step(c, carry):
        r0 = pl.multiple_of(off + c * MOE_CHUNK, MOE_CHUNK)
        hg = hc_ref[pl.ds(r0, MOE_CHUNK), :]
        a = _dot(hg, w1)
        b = _dot(hg, w3)
        y = _dot((a * _sigmoid(a) * b).astype(BF16), w2)
        cg = cc_ref[pl.ds(r0, MOE_CHUNK), :]
        lane = lax.broadcasted_iota(I32, cg.shape, 1)
        yacc_ref[pl.ds(r0, MOE_CHUNK), :] += jnp.sum(jnp.where(lane == e, cg, 0.0), axis=1, keepdims=True) * y
        return carry

    lax.fori_loop(0, nsteps, step, 0)

    @pl.when(e == pl.num_programs(1) - 1)
    def _():
        pos = slab_ref[...][:, MOE_POS_LANE:MOE_POS_LANE + 1]
        back = jnp.where(lax.broadcasted_iota(I32, (1, rows), 1).astype(F32) == pos, 1.0, 0.0).astype(BF16)
        out = x1_ref[...] + mod_ref[0][5:6] * _dot(back, yacc_ref[...].astype(BF16))
        if final_norm:
            ms = jnp.mean(out * out, axis=-1, keepdims=True)
            out = out * lax.rsqrt(ms + NORM_EPS) * fg_ref[...]
        o_ref[...] = out


def _moe_compact(h2, slab, post, info, exp_w1, exp_w3, exp_w2, layer, x1, mod_l, final_g, seq, final_norm):
    n, d = x1.shape
    ne, f = exp_w1.shape[1], exp_w1.shape[3]
    tm = min(1024, seq)
    tpb = seq // tm
    rows = tm + N_GROUPS * MOE_CHUNK
    per_tile = info[::V7X_SUBLANES]
    gcnt = per_tile[:, 0:N_GROUPS].astype(I32).reshape(-1)
    goff = per_tile[:, N_GROUPS:2 * N_GROUPS].astype(I32).reshape(-1)
    row = lambda width: pl.BlockSpec((tm, width), lambda i, e, *_: (i, 0))
    return pl.pallas_call(
        functools.partial(_moe_compact_kernel, final_norm=final_norm),
        grid_spec=pltpu.PrefetchScalarGridSpec(
            num_scalar_prefetch=2,
            grid=(n // tm, ne),
            in_specs=[row(d), row(V7X_LANES), pl.BlockSpec((V7X_SUBLANES, tm), lambda i, e, *_: (0, i)),
                      pl.BlockSpec((1, 1, d, f), lambda i, e, *_: (layer, e, 0, 0)),
                      pl.BlockSpec((1, 1, d, f), lambda i, e, *_: (layer, e, 0, 0)),
                      pl.BlockSpec((1, 1, f, d), lambda i, e, *_: (layer, e, 0, 0)),
                      row(d), pl.BlockSpec((1, 6, d), lambda i, e, *_: (i // tpb, 0, 0)),
                      pl.BlockSpec((1, d), lambda i, e, *_: (0, 0))],
            out_specs=row(d),
            scratch_shapes=[pltpu.VMEM((rows, d), BF16), pltpu.VMEM((rows, V7X_LANES), F32),
                            pltpu.VMEM((rows, d), F32)]),
        out_shape=jax.ShapeDtypeStruct((n, d), F32),
        compiler_params=_cparams("parallel", "arbitrary"),
        name="moe_compact",
    )(goff, gcnt, h2, slab, post, exp_w1, exp_w3, exp_w2, x1, mod_l, final_g)


def kernel(x, c, positions, ada_w, ada_b, mix_norm_g, w_in, gate_w, gate_b, branch_w, out_w, s5_lam_re, s5_lam_im, s5_log_step, s5_b_re, s5_b_im, s5_c_re, s5_c_im, s5_d, s5_glu_w, s5_glu_b, rg_conv_w, rg_conv_b, rg_wr, rg_br, rg_wi, rg_bi, rg_lam, rw_mu, rw_w0, rw_w_up, rw_a0, rw_a_up, rw_g_up, rw_k_k, rw_k_a, rw_r_k, rw_gn_w, rw_gn_b, ffn_norm_g, router_w, router_b, exp_w1, exp_w3, exp_w2, final_norm_g):
    bsz, seq, d = x.shape
    depth = ada_w.shape[0]
    n = bsz * seq
    x2 = x.reshape(n, d)
    mod = _ada_mod(c, ada_w, ada_b)
    tabs = _rope_tables(positions)
    fg = final_norm_g.reshape(1, d)
    ew1, ew3, ew2 = exp_w1.astype(BF16), exp_w3.astype(BF16), exp_w2.astype(BF16)
    for l in range(depth):
        mod_l = mod[l]
        g1 = mix_norm_g[l].reshape(1, d)
        (u, q, k, v, iqh, iql, ikh, ikl, iw, rgx, rgg, rwf) = _inproj(x2, mod_l, g1, w_in[l], tabs, seq)
        s5p = _s5_params(s5_lam_re[l], s5_lam_im[l], s5_log_step[l], s5_b_re[l], s5_b_im[l], s5_c_re[l], s5_c_im[l])
        o_s5 = _s5_mixer(u, s5p, s5_d[l], s5_glu_w[l], s5_glu_b[l], bsz, seq)
        o_dsa = _dsa_mixer(q, k, v, iqh, iql, ikh, ikl, iw, bsz, seq)
        o_rg = _rglru_mixer(rgx, rgg, rg_conv_w[l], rg_conv_b[l], rg_wr[l], rg_br[l], rg_wi[l], rg_bi[l],
                            rg_lam[l], bsz, seq)
        o_rw = _rwkv_mixer(rwf, rw_mu[l], rw_w0[l], rw_w_up[l], rw_a0[l], rw_a_up[l], rw_g_up[l], rw_k_k[l],
                           rw_k_a[l], rw_r_k[l], rw_gn_w[l], rw_gn_b[l], bsz, seq)
        x1 = _merge(x2, mod_l, g1, (o_s5, o_dsa, o_rg, o_rw), gate_w[l], gate_b[l], branch_w[l], out_w[l], seq)
        h2, slab, post, info = _route(x1, mod_l, ffn_norm_g[l].reshape(1, d), router_w, router_b, seq)
        x2 = _moe_compact(h2, slab, post, info, ew1, ew3, ew2, l, x1, mod_l, fg, seq, l == depth - 1)
    return x2.reshape(bsz, seq, d)
```
